```python
import math
import jax, jax.numpy as jnp
from jax import lax
import numpy as np

D_MODEL = 1024
BATCH = 4
SEQ = 4096
DEPTH = 2
DEC_BATCH = 8
DEC_SEQ = 32
PAST_LEN = 4096

CHUNK = 64
N_MIXERS = 2
N_SSM_LAYERS = (DEPTH + 1) // 2
N_ATTN_LAYERS = DEPTH // 2
SSM_GROUP = 16
SSM_GROUPS = D_MODEL // SSM_GROUP
SSM_STATE = 64
DT_MIN = 1e-3
DT_MAX = 1e-1
N_HEADS = 16
N_KV_HEADS = 4
HEAD_DIM = 64
ATTN_WIDTH = N_HEADS * HEAD_DIM
KV_WIDTH = N_KV_HEADS * HEAD_DIM
IDX_HEADS = 8
IDX_DIM = 64
TOPK_MAX = 256
Q_BLOCK = 128
ROT_DIM = HEAD_DIM // 4
ROPE_THETA = 500000.0
OFF_K = ATTN_WIDTH
OFF_V = OFF_K + KV_WIDTH
OFF_QI = OFF_V + KV_WIDTH
OFF_KI = OFF_QI + IDX_HEADS * IDX_DIM
OFF_WI = OFF_KI + IDX_DIM
IN_COLS = OFF_WI + IDX_HEADS
D_FF = 2816
CONV_W = 3
EPS = 1e-6
NEG = -1e30

kernel_name = "hybrid_s5_dsa_convffn_stream_step"


def rms_norm(x, g):
    xf = x.astype(jnp.float32)
    y = xf * lax.rsqrt(jnp.mean(xf * xf, axis=-1, keepdims=True) + EPS)
    return (y * g.astype(jnp.float32)).astype(x.dtype)


def rotary_partial(x, pos):
    half = ROT_DIM // 2
    inv = ROPE_THETA ** (-jnp.arange(half, dtype=jnp.float32) / half)
    ang = pos.astype(jnp.float32)[:, None] * inv[None, :]
    cos = jnp.cos(ang)[None, :, None, :]
    sin = jnp.sin(ang)[None, :, None, :]
    xr = x[..., :ROT_DIM].astype(jnp.float32)
    x1, x2 = xr[..., :half], xr[..., half:]
    rot = jnp.concatenate([x1 * cos - x2 * sin, x2 * cos + x1 * sin], axis=-1).astype(x.dtype)
    return jnp.concatenate([rot, x[..., ROT_DIM:]], axis=-1)


def s5_discretize(lam_re, lam_im, log_dt, b_re, b_im):
    lre = jnp.minimum(lam_re.astype(jnp.float32), -1e-4)
    lim = lam_im.astype(jnp.float32)
    dt = jnp.exp(log_dt.astype(jnp.float32))[:, None]
    mag = jnp.exp(lre * dt)
    a_re = mag * jnp.cos(lim * dt)
    a_im = mag * jnp.sin(lim * dt)
    den = lre * lre + lim * lim
    n_re = a_re - 1.0
    f_re = (n_re * lre + a_im * lim) / den
    f_im = (a_im * lre - n_re * lim) / den
    br = b_re.astype(jnp.float32)
    bi = b_im.astype(jnp.float32)
    bb_re = f_re[..., None] * br - f_im[..., None] * bi
    bb_im = f_re[..., None] * bi + f_im[..., None] * br
    return a_re, a_im, bb_re, bb_im


def _cplx_combine(e1, e2):
    a1r, a1i, b1r, b1i = e1
    a2r, a2i, b2r, b2i = e2
    return (a2r * a1r - a2i * a1i,
            a2r * a1i + a2i * a1r,
            a2r * b1r - a2i * b1i + b2r,
            a2r * b1i + a2i * b1r + b2i)


def s5_mixer(u, h0_re, h0_im, lam_re, lam_im, log_dt, b_re, b_im, c_re, c_im, d_skip, w_glu, b_glu):
    b, t, _ = u.shape
    uf = u.astype(jnp.float32).reshape(b, t, SSM_GROUPS, SSM_GROUP)
    a_re, a_im, bb_re, bb_im = s5_discretize(lam_re, lam_im, log_dt, b_re, b_im)
    bu_re = jnp.einsum('btgc,gpc->btgp', uf, bb_re)
    bu_im = jnp.einsum('btgc,gpc->btgp', uf, bb_im)
    h0r = h0_re.astype(jnp.float32)
    h0i = h0_im.astype(jnp.float32)
    bu_re = bu_re.at[:, 0].add(a_re * h0r - a_im * h0i)
    bu_im = bu_im.at[:, 0].add(a_re * h0i + a_im * h0r)
    ar = jnp.broadcast_to(a_re, bu_re.shape)
    ai = jnp.broadcast_to(a_im, bu_im.shape)
    _, _, h_re, h_im = lax.associative_scan(_cplx_combine, (ar, ai, bu_re, bu_im), axis=1)
    y = (jnp.einsum('btgp,gcp->btgc', h_re, c_re.astype(jnp.float32))
         - jnp.einsum('btgp,gcp->btgc', h_im, c_im.astype(jnp.float32)))
    y = y + d_skip.astype(jnp.float32).reshape(SSM_GROUPS, SSM_GROUP) * uf
    z = jax.nn.gelu(y.reshape(b, t, D_MODEL)).astype(u.dtype)
    g = z @ w_glu + b_glu
    out = g[..., :D_MODEL] * jax.nn.sigmoid(g[..., D_MODEL:])
    return out.astype(u.dtype), h_re[:, -1], h_im[:, -1]


def dsa_project(xn, pos, w_in, q_norm, k_norm):
    b, t, _ = xn.shape
    proj = xn @ w_in
    q = proj[..., :OFF_K].reshape(b, t, N_HEADS, HEAD_DIM)
    k = proj[..., OFF_K:OFF_V].reshape(b, t, N_KV_HEADS, HEAD_DIM)
    v = proj[..., OFF_V:OFF_QI].reshape(b, t, N_KV_HEADS, HEAD_DIM)
    qi = proj[..., OFF_QI:OFF_KI].reshape(b, t, IDX_HEADS, IDX_DIM)
    ki = proj[..., OFF_KI:OFF_WI]
    wi = proj[..., OFF_WI:]
    q = rotary_partial(rms_norm(q, q_norm), pos)
    k = rotary_partial(rms_norm(k, k_norm), pos)
    qi = rotary_partial(qi, pos)
    ki = rotary_partial(ki[:, :, None, :], pos)[:, :, 0]
    return q, k, v, qi, ki, wi


def dsa_attend_block(q, qi, wi, qpos, k, v, ki, kpos, topk):
    b, tb = q.shape[0], q.shape[1]
    logits = jnp.einsum('bthd,bsd->bths', qi.astype(jnp.float32), ki.astype(jnp.float32)) * (IDX_DIM ** -0.5)
    score = jnp.einsum('bths,bth->bts', jax.nn.relu(logits), wi.astype(jnp.float32) * (IDX_HEADS ** -0.5))
    allowed = (kpos[None, :] // CHUNK) <= (qpos[:, None] // CHUNK)
    score = jnp.where(allowed[None], score, NEG)
    vals, idx = lax.top_k(score, topk)
    valid = vals > (0.5 * NEG)
    gather = jax.vmap(lambda rows, ii: rows[ii])
    ks = gather(k, idx).astype(jnp.float32)
    vs = gather(v, idx).astype(jnp.float32)
    qg = q.astype(jnp.float32).reshape(b, tb, N_KV_HEADS, N_HEADS // N_KV_HEADS, HEAD_DIM)
    s = jnp.einsum('btkgd,btjkd->btkgj', qg, ks) * (HEAD_DIM ** -0.5)
    s = jnp.where(valid[:, :, None, None, :], s, NEG)
    p = jax.nn.softmax(s, axis=-1)
    o = jnp.einsum('btkgj,btjkd->btkgd', p, vs)
    return o.reshape(b, tb, ATTN_WIDTH).astype(q.dtype)


def dsa_prompt_attend(q, qi, wi, pos, k, v, ki, topk):
    b, t = q.shape[0], q.shape[1]
    nb = t // Q_BLOCK
    to_blocks = lambda a: a.reshape((b, nb, Q_BLOCK) + a.shape[2:]).swapaxes(0, 1)

    def one_block(args):
        qb, qib, wib, pb = args
        return dsa_attend_block(qb, qib, wib, pb, k, v, ki, pos, topk)

    out = lax.map(one_block, (to_blocks(q), to_blocks(qi), to_blocks(wi), pos.reshape(nb, Q_BLOCK)))
    return out.swapaxes(0, 1).reshape(b, t, ATTN_WIDTH)


def conv_ffn(xn, conv_state, w_up, conv_w, conv_b, w_down):
    t = xn.shape[1]
    h = xn @ w_up
    a, bv = h[..., :D_FF], h[..., D_FF:]
    padded = jnp.concatenate([conv_state.astype(a.dtype), a], axis=1)
    c = conv_b + sum(conv_w[j] * padded[:, j:j + t] for j in range(CONV_W))
    out = (jax.nn.gelu(c) * bv) @ w_down
    return out, padded[:, -(CONV_W - 1):]


def setup_inputs(seed: int = 0) -> dict:
    key = jax.random.key(seed)
    ks = jax.random.split(key, 32)
    nrm = lambda k, shape, s: jax.random.normal(k, shape, jnp.float32) * s
    lam_im0 = math.pi * jnp.arange(SSM_STATE, dtype=jnp.float32)
    return {
        "x_prompt": nrm(ks[0], (BATCH, SEQ, D_MODEL), 1.0),
        "x_sample": nrm(ks[1], (DEC_BATCH, DEC_SEQ, D_MODEL), 1.0),
        "state_ssm_re": nrm(ks[2], (N_SSM_LAYERS, DEC_BATCH, SSM_GROUPS, SSM_STATE), 0.1),
        "state_ssm_im": nrm(ks[3], (N_SSM_LAYERS, DEC_BATCH, SSM_GROUPS, SSM_STATE), 0.1),
        "cache_k": nrm(ks[4], (N_ATTN_LAYERS, DEC_BATCH, PAST_LEN, N_KV_HEADS, HEAD_DIM), 1.0),
        "cache_v": nrm(ks[5], (N_ATTN_LAYERS, DEC_BATCH, PAST_LEN, N_KV_HEADS, HEAD_DIM), 1.0),
        "cache_kidx": nrm(ks[6], (N_ATTN_LAYERS, DEC_BATCH, PAST_LEN, IDX_DIM), 1.0),
        "cache_conv": nrm(ks[7], (DEPTH, DEC_BATCH, CONV_W - 1, D_FF), 1.0),
        "norm_mix": 1.0 + nrm(ks[8], (DEPTH, D_MODEL), 0.02),
        "norm_ffn": 1.0 + nrm(ks[9], (DEPTH, D_MODEL), 0.02),
        "ssm_lambda_re": -0.5 + nrm(ks[10], (N_SSM_LAYERS, SSM_GROUPS, SSM_STATE), 0.01),
        "ssm_lambda_im": lam_im0 + nrm(ks[11], (N_SSM_LAYERS, SSM_GROUPS, SSM_STATE), 0.01),
        "ssm_log_dt": jax.random.uniform(ks[12], (N_SSM_LAYERS, SSM_GROUPS), jnp.float32,
                                         math.log(DT_MIN), math.log(DT_MAX)),
        "ssm_b_re": nrm(ks[13], (N_SSM_LAYERS, SSM_GROUPS, SSM_STATE, SSM_GROUP), (2.0 * SSM_GROUP) ** -0.5),
        "ssm_b_im": nrm(ks[14], (N_SSM_LAYERS, SSM_GROUPS, SSM_STATE, SSM_GROUP), (2.0 * SSM_GROUP) ** -0.5),
        "ssm_c_re": nrm(ks[15], (N_SSM_LAYERS, SSM_GROUPS, SSM_GROUP, SSM_STATE), SSM_STATE ** -0.5),
        "ssm_c_im": nrm(ks[16], (N_SSM_LAYERS, SSM_GROUPS, SSM_GROUP, SSM_STATE), SSM_STATE ** -0.5),
        "ssm_d": nrm(ks[17], (N_SSM_LAYERS, D_MODEL), 1.0),
        "ssm_w_glu": nrm(ks[18], (N_SSM_LAYERS, D_MODEL, 2 * D_MODEL), D_MODEL ** -0.5),
        "ssm_b_glu": nrm(ks[19], (N_SSM_LAYERS, 2 * D_MODEL), 0.01),
        "attn_w_in": nrm(ks[20], (N_ATTN_LAYERS, D_MODEL, IN_COLS), D_MODEL ** -0.5),
        "attn_q_norm": 1.0 + nrm(ks[21], (N_ATTN_LAYERS, HEAD_DIM), 0.02),
        "attn_k_norm": 1.0 + nrm(ks[22], (N_ATTN_LAYERS, HEAD_DIM), 0.02),
        "attn_w_o": nrm(ks[23], (N_ATTN_LAYERS, ATTN_WIDTH, D_MODEL), ATTN_WIDTH ** -0.5),
        "ffn_w_up": nrm(ks[24], (DEPTH, D_MODEL, 2 * D_FF), D_MODEL ** -0.5),
        "ffn_conv_w": nrm(ks[25], (DEPTH, CONV_W, D_FF), CONV_W ** -0.5),
        "ffn_conv_b": nrm(ks[26], (DEPTH, D_FF), 0.01),
        "ffn_w_down": nrm(ks[27], (DEPTH, D_FF, D_MODEL), D_FF ** -0.5),
    }


def reference(x_prompt, x_sample, state_ssm_re, state_ssm_im, cache_k, cache_v, cache_kidx, cache_conv,
              norm_mix, norm_ffn, ssm_lambda_re, ssm_lambda_im, ssm_log_dt, ssm_b_re, ssm_b_im,
              ssm_c_re, ssm_c_im, ssm_d, ssm_w_glu, ssm_b_glu, attn_w_in, attn_q_norm, attn_k_norm,
              attn_w_o, ffn_w_up, ffn_conv_w, ffn_conv_b, ffn_w_down):
    b_p, t_p = x_prompt.shape[0], x_prompt.shape[1]
    b_s, t_s = x_sample.shape[0], x_sample.shape[1]
    past = cache_k.shape[2]
    pos_p = jnp.arange(t_p, dtype=jnp.int32)
    pos_s = past + jnp.arange(t_s, dtype=jnp.int32)
    kpos_s = jnp.arange(past + t_s, dtype=jnp.int32)
    topk_p = min(TOPK_MAX, t_p // 4)
    topk_s = min(TOPK_MAX, (past + t_s) // 4)

    xp, xs = x_prompt, x_sample
    ssm_re_p, ssm_im_p, ssm_re_s, ssm_im_s = [], [], [], []
    k_p, v_p, ki_p, k_s, v_s, ki_s = [], [], [], [], [], []
    conv_p, conv_s = [], []
    for i in range(DEPTH):
        j = i // N_MIXERS
        hp = rms_norm(xp, norm_mix[i])
        hs = rms_norm(xs, norm_mix[i])
        if i % N_MIXERS == 0:
            sp = (ssm_lambda_re[j], ssm_lambda_im[j], ssm_log_dt[j], ssm_b_re[j], ssm_b_im[j],
                  ssm_c_re[j], ssm_c_im[j], ssm_d[j], ssm_w_glu[j], ssm_b_glu[j])
            zero = jnp.zeros((b_p, SSM_GROUPS, SSM_STATE), jnp.float32)
            yp, hpr, hpi = s5_mixer(hp, zero, zero, *sp)
            ys, hsr, hsi = s5_mixer(hs, state_ssm_re[j], state_ssm_im[j], *sp)
            ssm_re_p.append(hpr.astype(x_prompt.dtype))
            ssm_im_p.append(hpi.astype(x_prompt.dtype))
            ssm_re_s.append(hsr.astype(state_ssm_re.dtype))
            ssm_im_s.append(hsi.astype(state_ssm_im.dtype))
        else:
            qp, kp, vp, qip, kip, wip = dsa_project(hp, pos_p, attn_w_in[j], attn_q_norm[j], attn_k_norm[j])
            yp = dsa_prompt_attend(qp, qip, wip, pos_p, kp, vp, kip, topk_p) @ attn_w_o[j]
            qs, ks_, vs_, qis, kis, wis = dsa_project(hs, pos_s, attn_w_in[j], attn_q_norm[j], attn_k_norm[j])
            k_all = jnp.concatenate([cache_k[j].astype(ks_.dtype), ks_], axis=1)
            v_all = jnp.concatenate([cache_v[j].astype(vs_.dtype), vs_], axis=1)
            ki_all = jnp.concatenate([cache_kidx[j].astype(kis.dtype), kis], axis=1)
            ys = dsa_attend_block(qs, qis, wis, pos_s, k_all, v_all, ki_all, kpos_s, topk_s) @ attn_w_o[j]
            k_p.append(kp); v_p.append(vp); ki_p.append(kip)
            k_s.append(ks_); v_s.append(vs_); ki_s.append(kis)
        xp = xp + yp.astype(xp.dtype)
        xs = xs + ys.astype(xs.dtype)
        zero_conv = jnp.zeros((b_p, CONV_W - 1, D_FF), xp.dtype)
        fp, cp = conv_ffn(rms_norm(xp, norm_ffn[i]), zero_conv, ffn_w_up[i], ffn_conv_w[i], ffn_conv_b[i], ffn_w_down[i])
        fs, cs = conv_ffn(rms_norm(xs, norm_ffn[i]), cache_conv[i], ffn_w_up[i], ffn_conv_w[i], ffn_conv_b[i], ffn_w_down[i])
        xp = xp + fp.astype(xp.dtype)
        xs = xs + fs.astype(xs.dtype)
        conv_p.append(cp)
        conv_s.append(cs)

    new_ssm_re_p = jnp.stack(ssm_re_p)
    new_ssm_im_p = jnp.stack(ssm_im_p)
    new_ssm_re_s = jnp.stack(ssm_re_s)
    new_ssm_im_s = jnp.stack(ssm_im_s)
    new_k_p = jnp.stack(k_p)
    new_v_p = jnp.stack(v_p)
    new_ki_p = jnp.stack(ki_p)
    new_k_s = jnp.stack(k_s)
    new_v_s = jnp.stack(v_s)
    new_ki_s = jnp.stack(ki_s)
    new_conv_p = jnp.stack(conv_p)
    new_conv_s = jnp.stack(conv_s)
    return (xp, xs, new_ssm_re_p, new_ssm_im_p, new_ssm_re_s, new_ssm_im_s,
            new_k_p, new_v_p, new_ki_p, new_k_s, new_v_s, new_ki_s, new_conv_p, new_conv_s)
```

```python
import functools
import math

import jax
import jax.numpy as jnp
import numpy as np
from jax import lax
from jax.experimental import pallas as pl
from jax.experimental.pallas import tpu as pltpu

F32 = jnp.float32
BF16 = jnp.bfloat16
I32 = jnp.int32

D_MODEL = 1024
CHUNK = 64
SSM_GROUP = 16
SSM_GROUPS = D_MODEL // SSM_GROUP
SSM_STATE = 64
N_HEADS = 16
N_KV_HEADS = 4
HEAD_DIM = 64
ATTN_WIDTH = N_HEADS * HEAD_DIM
KV_WIDTH = N_KV_HEADS * HEAD_DIM
IDX_HEADS = 8
IDX_DIM = 64
TOPK_MAX = 256
ROT_DIM = HEAD_DIM // 4
ROPE_THETA = 500000.0
OFF_K = ATTN_WIDTH
OFF_V = OFF_K + KV_WIDTH
OFF_QI = OFF_V + KV_WIDTH
OFF_KI = OFF_QI + IDX_HEADS * IDX_DIM
OFF_WI = OFF_KI + IDX_DIM
IN_COLS = OFF_WI + IDX_HEADS
D_FF = 2816
CONV_W = 3
EPS = 1e-6
NEG = -1e30

LANES = 128
SUBLANES = 8
VMEM_LIMIT_BYTES = 56 * 1024 * 1024

S5_L = 16
S5_OCT = D_MODEL // LANES
S5_OCT_GROUPS = LANES // SSM_GROUP
S5_HALF = S5_OCT_GROUPS * SSM_STATE
S5_PW_ROWS = 24

FFN_FC = 256
FFN_NC = D_FF // FFN_FC
IN_COLS_PAD = 2176
INT_MIN = -(2**31)
_HALF_NEG_KEY = int(np.float32(0.5 * NEG).view(np.int32)) ^ 0x7FFFFFFF


def _cparams(sem):
    return pltpu.CompilerParams(dimension_semantics=sem, vmem_limit_bytes=VMEM_LIMIT_BYTES)


def _rms(x, g):
    ms = jnp.mean(x * x, axis=-1, keepdims=True)
    return x * lax.rsqrt(ms + EPS) * g


def _norm_body(x_ref, g_ref, o_ref):
    o_ref[...] = _rms(x_ref[...], g_ref[...]).astype(BF16)


def _norm_cast(x2d, g):
    rows = x2d.shape[0]
    tm = min(rows, 1024)
    return pl.pallas_call(
        _norm_body,
        out_shape=jax.ShapeDtypeStruct((rows, D_MODEL), BF16),
        grid=(rows // tm,),
        in_specs=[pl.BlockSpec((tm, D_MODEL), lambda i: (i, 0)),
                  pl.BlockSpec((1, D_MODEL), lambda i: (0, 0))],
        out_specs=pl.BlockSpec((tm, D_MODEL), lambda i: (i, 0)),
        compiler_params=_cparams(("arbitrary",)),
        name="rmsnorm_cast",
    )(x2d, g.reshape(1, D_MODEL))


def _s5_operators(lam_re, lam_im, log_dt, b_re, b_im, c_re, c_im, d_skip):
    hp = lax.Precision.HIGHEST
    L = S5_L
    lre = jnp.minimum(lam_re.astype(F32), -1e-4)
    lim = lam_im.astype(F32)
    dt = jnp.exp(log_dt.astype(F32))[:, None]
    n = jnp.arange(L + 1, dtype=F32)[:, None, None]
    mag = jnp.exp(n * (lre * dt)[None])
    ang = n * (lim * dt)[None]
    pr = mag * jnp.cos(ang)
    pi = mag * jnp.sin(ang)
    a_re, a_im = pr[1], pi[1]
    den = lre * lre + lim * lim
    n_re = a_re - 1.0
    f_re = (n_re * lre + a_im * lim) / den
    f_im = (a_im * lre - n_re * lim) / den
    br = b_re.astype(F32)
    bi = b_im.astype(F32)
    bb_re = f_re[..., None] * br - f_im[..., None] * bi
    bb_im = f_re[..., None] * bi + f_im[..., None] * br
    cr = c_re.astype(F32)
    ci = c_im.astype(F32)
    eye = jnp.eye(S5_OCT_GROUPS, dtype=F32)
    og = (S5_OCT, S5_OCT_GROUPS)

    car = cr[None] * pr[:L, :, None, :] - ci[None] * pi[:L, :, None, :]
    cai = cr[None] * pi[:L, :, None, :] + ci[None] * pr[:L, :, None, :]
    kl = (jnp.einsum("lgdp,gpc->lgdc", car, bb_re, precision=hp)
          - jnp.einsum("lgdp,gpc->lgdc", cai, bb_im, precision=hp))
    kbd = jnp.einsum("ljgdc,gh->ljgchd", kl.reshape((L,) + og + (SSM_GROUP, SSM_GROUP)), eye)
    kbd = kbd.reshape(L, S5_OCT, LANES, LANES)
    kpad = jnp.concatenate([jnp.zeros_like(kbd[:1]), kbd], axis=0)
    k2 = []
    for dlt in range(L // 2):
        top = jnp.concatenate([kpad[2 * dlt + 1], kpad[2 * dlt + 2]], axis=-1)
        bot = jnp.concatenate([kpad[2 * dlt], kpad[2 * dlt + 1]], axis=-1)
        k2.append(jnp.concatenate([top, bot], axis=-2))
    k2 = jnp.stack(k2, axis=1).astype(BF16)

    def bd_in(m):
        m = m.reshape(og + (SSM_STATE, SSM_GROUP))
        return jnp.einsum("jgpc,gh->jgchp", m, eye).reshape(S5_OCT, LANES, S5_HALF)

    ab_re = a_re[..., None] * bb_re - a_im[..., None] * bb_im
    ab_im = a_re[..., None] * bb_im + a_im[..., None] * bb_re
    f2 = jnp.concatenate([
        jnp.concatenate([bd_in(ab_re), bd_in(ab_im)], axis=-1),
        jnp.concatenate([bd_in(bb_re), bd_in(bb_im)], axis=-1)], axis=-2).astype(BF16)

    def bd_out(m):
        m = m.reshape(og + (SSM_GROUP, SSM_STATE))
        return jnp.einsum("jgcp,gh->jgphc", m, eye).reshape(S5_OCT, S5_HALF, LANES)

    c1r = cr * a_re[:, None, :] - ci * a_im[:, None, :]
    c1i = cr * a_im[:, None, :] + ci * a_re[:, None, :]
    c2 = jnp.concatenate([
        jnp.concatenate([bd_out(cr), bd_out(c1r)], axis=-1),
        jnp.concatenate([-bd_out(ci), -bd_out(c1i)], axis=-1)], axis=-2).astype(BF16)

    def oct_lanes(p):
        return p.reshape((L + 1,) + og + (SSM_STATE,)).transpose(1, 0, 2, 3).reshape(S5_OCT, L + 1, S5_HALF)

    pw = jnp.concatenate([oct_lanes(pr), oct_lanes(pi)], axis=-1)
    pw = jnp.pad(pw, ((0, 0), (0, S5_PW_ROWS - (L + 1)), (0, 0)))
    dsk = d_skip.astype(F32).reshape(S5_OCT, 1, LANES)
    dsk2 = jnp.concatenate([dsk, dsk], axis=-1)
    return f2, k2, c2, pw, dsk2


def _s5_body(*refs, bpt, nck):
    L = S5_L
    H = S5_HALF
    u_refs = refs[:L]
    f2_ref, k2_ref, c2_ref, pw_ref, dsk_ref, h0_ref, z_ref, ht_ref, v_scr, hp_scr = refs[L:]

    def cmul(xr, xi, n):
        pr = pw_ref[0, n:n + 1, :H]
        pi = pw_ref[0, n:n + 1, H:]
        return xr * pr - xi * pi, xr * pi + xi * pr

    upair = [jnp.concatenate([u_refs[2 * s][...], u_refs[2 * s + 1][...]], axis=1)
             for s in range(L // 2)]

    vr = vi = None
    for s2 in range(L // 2):
        p = jnp.dot(upair[s2], f2_ref[0], preferred_element_type=F32)
        qr, qi = cmul(p[:, :H], p[:, H:], L - 2 - 2 * s2)
        vr = qr if vr is None else vr + qr
        vi = qi if vi is None else vi + qi
    v_scr[:, :H] = vr
    v_scr[:, H:] = vi

    alr = pw_ref[0, L:L + 1, :H]
    ali = pw_ref[0, L:L + 1, H:]
    for b in range(bpt):
        def step(k, carry, b=b):
            hr, hi = carry
            row = b * nck + k
            hp_scr[pl.ds(row, 1), :H] = hr
            hp_scr[pl.ds(row, 1), H:] = hi
            wr = v_scr[pl.ds(row, 1), :H]
            wi = v_scr[pl.ds(row, 1), H:]
            return hr * alr - hi * ali + wr, hr * ali + hi * alr + wi

        h0r = h0_ref[0, 0, b:b + 1, :H]
        h0i = h0_ref[0, 0, b:b + 1, H:]
        hr, hi = lax.fori_loop(0, nck, step, (h0r, h0i))
        ht_ref[0, 0, b:b + 1, :H] = hr
        ht_ref[0, 0, b:b + 1, H:] = hi

    hpr = hp_scr[:, :H]
    hpi = hp_scr[:, H:]
    dsk = dsk_ref[0]
    for t2 in range(L // 2):
        gr, gi = cmul(hpr, hpi, 2 * t2 + 1)
        g = jnp.concatenate([gr, gi], axis=1).astype(BF16)
        y = jnp.dot(g, c2_ref[0], preferred_element_type=F32)
        for s2 in range(t2 + 1):
            y = y + jnp.dot(upair[s2], k2_ref[0, t2 - s2], preferred_element_type=F32)
        y = y + dsk * upair[t2].astype(F32)
        z_ref[0, :, t2 * 2 * LANES:(t2 + 1) * 2 * LANES] = jax.nn.gelu(y).astype(BF16)


def _s5_scan(u2d, ops, h0, *, batch, seq):
    L = S5_L
    f2, k2, c2, pw, dsk2 = ops
    nck = seq // L
    n_rows = batch * nck
    bpt = 1 if nck % 16 == 0 else batch
    tr = bpt * nck
    n_tiles = n_rows // tr
    u_rows = u2d.reshape(n_rows, L * D_MODEL)
    h0 = h0.reshape(S5_OCT, n_tiles, bpt, 2 * S5_HALF)
    in_specs = [pl.BlockSpec((tr, LANES), functools.partial(lambda j, i, s: (i, s * S5_OCT + j), s=s))
                for s in range(L)]
    in_specs += [
        pl.BlockSpec((1, 2 * LANES, 2 * S5_HALF), lambda j, i: (j, 0, 0)),
        pl.BlockSpec((1, L // 2, 2 * LANES, 2 * LANES), lambda j, i: (j, 0, 0, 0)),
        pl.BlockSpec((1, 2 * S5_HALF, 2 * LANES), lambda j, i: (j, 0, 0)),
        pl.BlockSpec((1, S5_PW_ROWS, 2 * S5_HALF), lambda j, i: (j, 0, 0)),
        pl.BlockSpec((1, 1, 2 * LANES), lambda j, i: (j, 0, 0)),
        pl.BlockSpec((1, 1, bpt, 2 * S5_HALF), lambda j, i: (j, i, 0, 0)),
    ]
    z, ht = pl.pallas_call(
        functools.partial(_s5_body, bpt=bpt, nck=nck),
        out_shape=(jax.ShapeDtypeStruct((S5_OCT, n_rows, L * LANES), BF16),
                   jax.ShapeDtypeStruct((S5_OCT, n_tiles, bpt, 2 * S5_HALF), F32)),
        grid=(S5_OCT, n_tiles),
        in_specs=in_specs,
        out_specs=(pl.BlockSpec((1, tr, L * LANES), lambda j, i: (j, i, 0)),
                   pl.BlockSpec((1, 1, bpt, 2 * S5_HALF), lambda j, i: (j, i, 0, 0))),
        scratch_shapes=[pltpu.VMEM((tr, 2 * S5_HALF), F32), pltpu.VMEM((tr, 2 * S5_HALF), F32)],
        compiler_params=_cparams(("arbitrary", "arbitrary")),
        name="s5_scan",
    )(*([u_rows] * L), f2, k2, c2, pw, dsk2, h0)
    return z, ht


def _state_to_oct(s_re, s_im):
    b = s_re.shape[0]
    r = s_re.astype(F32).reshape(b, S5_OCT, S5_HALF).transpose(1, 0, 2)
    i = s_im.astype(F32).reshape(b, S5_OCT, S5_HALF).transpose(1, 0, 2)
    return jnp.concatenate([r, i], axis=-1)


def _oct_to_state(ht, batch):
    ht = ht.reshape(S5_OCT, batch, 2 * S5_HALF)
    r = ht[..., :S5_HALF].transpose(1, 0, 2).reshape(batch, SSM_GROUPS, SSM_STATE)
    i = ht[..., S5_HALF:].transpose(1, 0, 2).reshape(batch, SSM_GROUPS, SSM_STATE)
    return r, i


def _glu_body(z_ref, x_ref, w_ref, b_ref, o_ref, lhs_scr, *, tn):
    L = S5_L
    for t in range(L):
        for j in range(S5_OCT):
            lhs_scr[t * tn:(t + 1) * tn, j * LANES:(j + 1) * LANES] = z_ref[j, :, t * LANES:(t + 1) * LANES]
    g = jnp.dot(lhs_scr[...], w_ref[...], preferred_element_type=F32) + b_ref[...]
    o = g[:, :D_MODEL] * jax.nn.sigmoid(g[:, D_MODEL:])
    for t in range(L):
        sl = slice(t * D_MODEL, (t + 1) * D_MODEL)
        o_ref[:, sl] = x_ref[:, sl] + o[t * tn:(t + 1) * tn, :]


def _glu_residual(z, x2d, w_glu, b_glu):
    L = S5_L
    n_rows = z.shape[1]
    tn = 32 if n_rows % 32 == 0 else 16
    x_rows = x2d.reshape(n_rows, L * D_MODEL)
    out = pl.pallas_call(
        functools.partial(_glu_body, tn=tn),
        out_shape=jax.ShapeDtypeStruct((n_rows, L * D_MODEL), F32),
        grid=(n_rows // tn,),
        in_specs=[pl.BlockSpec((S5_OCT, tn, L * LANES), lambda i: (0, i, 0)),
                  pl.BlockSpec((tn, L * D_MODEL), lambda i: (i, 0)),
                  pl.BlockSpec((D_MODEL, 2 * D_MODEL), lambda i: (0, 0)),
                  pl.BlockSpec((1, 2 * D_MODEL), lambda i: (0, 0))],
        out_specs=pl.BlockSpec((tn, L * D_MODEL), lambda i: (i, 0)),
        scratch_shapes=[pltpu.VMEM((L * tn, D_MODEL), BF16)],
        compiler_params=_cparams(("arbitrary",)),
        name="glu_residual",
    )(z, x_rows, w_glu.astype(BF16), b_glu.astype(F32).reshape(1, 2 * D_MODEL))
    return out.reshape(x2d.shape)


def _ffn_body(x_ref, xh_ref, cs_ref, g_ref, wa_ref, wb_ref, cw_ref, cb_ref, wd_ref,
              o_ref, co_ref, *, tm, tiles_per_seq):
    i = pl.program_id(0)
    first = (i % tiles_per_seq) == 0
    x = x_ref[...]
    g = g_ref[...]
    xn = _rms(x, g).astype(BF16)
    xnh = _rms(xh_ref[...], g).astype(BF16)
    rows = lax.broadcasted_iota(I32, (tm, FFN_FC), 0)

    def chunk(c, acc):
        ha = jnp.dot(xn, wa_ref[c], preferred_element_type=F32)
        hh = jnp.dot(xnh, wa_ref[c], preferred_element_type=F32)
        hh = jnp.where(first, cs_ref[0, c], hh)
        co_ref[0, c] = ha[tm - SUBLANES:, :]
        h6 = hh[6:7, :]
        h7 = hh[7:8, :]
        prev1 = jnp.where(rows == 0, h7, pltpu.roll(ha, 1, 0))
        prev2 = jnp.where(rows == 0, h6, jnp.where(rows == 1, h7, pltpu.roll(ha, 2, 0)))
        cw = cw_ref[c]
        cv = cb_ref[c] + cw[0:1, :] * prev2 + cw[1:2, :] * prev1 + cw[2:3, :] * ha
        hb = jnp.dot(xn, wb_ref[c], preferred_element_type=F32)
        gate = (jax.nn.gelu(cv) * hb).astype(BF16)
        return acc + jnp.dot(gate, wd_ref[c], preferred_element_type=F32)

    acc = lax.fori_loop(0, FFN_NC, chunk, jnp.zeros((tm, D_MODEL), F32))
    o_ref[...] = x + acc


def _ffn_weights(w_up, conv_w, conv_b, w_down):
    wa = w_up[:, :D_FF].astype(BF16).reshape(D_MODEL, FFN_NC, FFN_FC).transpose(1, 0, 2)
    wb = w_up[:, D_FF:].astype(BF16).reshape(D_MODEL, FFN_NC, FFN_FC).transpose(1, 0, 2)
    cw = jnp.pad(conv_w.astype(F32), ((0, SUBLANES - CONV_W), (0, 0)))
    cw = cw.reshape(SUBLANES, FFN_NC, FFN_FC).transpose(1, 0, 2)
    cb = conv_b.astype(F32).reshape(FFN_NC, 1, FFN_FC)
    wd = w_down.astype(BF16).reshape(FFN_NC, FFN_FC, D_MODEL)
    return wa, wb, cw, cb, wd


def _conv_ffn(x2d, conv_state, g, weights, *, batch, seq):
    wa, wb, cw, cb, wd = weights
    tm = min(seq, 512)
    tiles_per_seq = seq // tm
    hb = tm // SUBLANES
    cs = jnp.pad(conv_state.astype(F32), ((0, 0), (SUBLANES - (CONV_W - 1), 0), (0, 0)))
    cs = cs.reshape(batch, SUBLANES, FFN_NC, FFN_FC).transpose(0, 2, 1, 3)
    const3 = lambda i: (0, 0, 0)
    out, co = pl.pallas_call(
        functools.partial(_ffn_body, tm=tm, tiles_per_seq=tiles_per_seq),
        out_shape=(jax.ShapeDtypeStruct(x2d.shape, F32),
                   jax.ShapeDtypeStruct((batch, FFN_NC, SUBLANES, FFN_FC), F32)),
        grid=(batch * tiles_per_seq,),
        in_specs=[pl.BlockSpec((tm, D_MODEL), lambda i: (i, 0)),
                  pl.BlockSpec((SUBLANES, D_MODEL), lambda i: (jnp.maximum(i * hb - 1, 0), 0)),
                  pl.BlockSpec((1, FFN_NC, SUBLANES, FFN_FC), lambda i: (i // tiles_per_seq, 0, 0, 0)),
                  pl.BlockSpec((1, D_MODEL), lambda i: (0, 0)),
                  pl.BlockSpec((FFN_NC, D_MODEL, FFN_FC), const3),
                  pl.BlockSpec((FFN_NC, D_MODEL, FFN_FC), const3),
                  pl.BlockSpec((FFN_NC, SUBLANES, FFN_FC), const3),
                  pl.BlockSpec((FFN_NC, 1, FFN_FC), const3),
                  pl.BlockSpec((FFN_NC, FFN_FC, D_MODEL), const3)],
        out_specs=(pl.BlockSpec((tm, D_MODEL), lambda i: (i, 0)),
                   pl.BlockSpec((1, FFN_NC, SUBLANES, FFN_FC), lambda i: (i // tiles_per_seq, 0, 0, 0))),
        compiler_params=_cparams(("arbitrary",)),
        name="conv_ffn",
    )(x2d, x2d, cs, g.astype(F32).reshape(1, D_MODEL), wa, wb, cw, cb, wd)
    new_state = co[:, :, SUBLANES - (CONV_W - 1):, :].transpose(0, 2, 1, 3).reshape(batch, CONV_W - 1, D_FF)
    return out, new_state


def _rope_tables(pos):
    half = ROT_DIM // 2
    inv = ROPE_THETA ** (-jnp.arange(half, dtype=F32) / half)
    ang = pos.astype(F32)[:, None] * inv[None, :]
    cos = jnp.cos(ang)
    sin = jnp.sin(ang)
    lane = jnp.arange(LANES) % HEAD_DIM
    idx = lane % half
    cc = jnp.where(lane[None] < ROT_DIM, cos[:, idx], 1.0)
    s1 = jnp.where((lane[None] >= half) & (lane[None] < ROT_DIM), sin[:, idx], 0.0)
    s2 = jnp.where(lane[None] < half, -sin[:, idx], 0.0)
    return cc.astype(F32), s1.astype(F32), s2.astype(F32)


def _proj_body(x_ref, g_ref, w_ref, qn_ref, kn_ref, cc_ref, s1_ref, s2_ref, bd_ref,
               q_ref, khm_ref, vhm_ref, kf_ref, vf_ref, qi_ref, kib_ref, kif_ref, wi_ref):
    xn = _rms(x_ref[...], g_ref[...]).astype(BF16)
    proj = jnp.dot(xn, w_ref[...], preferred_element_type=F32)
    cc = cc_ref[...]
    s1 = s1_ref[...]
    s2 = s2_ref[...]
    bd = bd_ref[...]

    def rope(y):
        return y * cc + pltpu.roll(y, ROT_DIM // 2, 1) * s1 + pltpu.roll(y, LANES - ROT_DIM // 2, 1) * s2

    def head_norm(y, gain):
        sq = y * y
        hi = sq.astype(BF16)
        lo = (sq - hi.astype(F32)).astype(BF16)
        ms = (jnp.dot(hi, bd, preferred_element_type=F32) + jnp.dot(lo, bd, preferred_element_type=F32))
        return y * lax.rsqrt(ms + EPS) * gain

    qn = qn_ref[...]
    kn = kn_ref[...]
    scale = HEAD_DIM ** -0.5
    for c in range(ATTN_WIDTH // LANES):
        y = proj[:, c * LANES:(c + 1) * LANES]
        q_ref[:, c * LANES:(c + 1) * LANES] = (rope(head_norm(y, qn)) * scale).astype(BF16)
    for c in range(KV_WIDTH // LANES):
        y = proj[:, OFF_K + c * LANES:OFF_K + (c + 1) * LANES]
        k = rope(head_norm(y, kn))
        kf_ref[:, c * LANES:(c + 1) * LANES] = k
        khm_ref[2 * c] = k[:, :HEAD_DIM].astype(BF16)
        khm_ref[2 * c + 1] = k[:, HEAD_DIM:].astype(BF16)
        v = proj[:, OFF_V + c * LANES:OFF_V + (c + 1) * LANES]
        vf_ref[:, c * LANES:(c + 1) * LANES] = v
        vhm_ref[2 * c] = v[:, :HEAD_DIM].astype(BF16)
        vhm_ref[2 * c + 1] = v[:, HEAD_DIM:].astype(BF16)
    iscale = IDX_DIM ** -0.5
    for c in range(IDX_HEADS * IDX_DIM // LANES):
        y = proj[:, OFF_QI + c * LANES:OFF_QI + (c + 1) * LANES]
        qi_ref[:, c * LANES:(c + 1) * LANES] = (rope(y) * iscale).astype(BF16)
    tail = proj[:, OFF_KI:OFF_KI + LANES]
    ki = rope(tail)[:, :IDX_DIM]
    kif_ref[...] = ki
    kib_ref[...] = ki.astype(BF16)
    wi_ref[...] = tail[:, IDX_DIM:IDX_DIM + IDX_HEADS] * (IDX_HEADS ** -0.5)


def _dsa_project(x2d, g, w_in, q_norm, k_norm, pos):
    rows = x2d.shape[0]
    tm = min(rows, 256)
    w = jnp.pad(w_in.astype(BF16), ((0, 0), (0, IN_COLS_PAD - IN_COLS)))
    if pos.shape[0] < tm:
        pos = jnp.tile(pos, tm // pos.shape[0])
    ntab = pos.shape[0] // tm
    cc, s1, s2 = _rope_tables(pos)
    lane = jnp.arange(LANES)
    bd = jnp.where((lane[:, None] // HEAD_DIM) == (lane[None, :] // HEAD_DIM), 1.0 / HEAD_DIM, 0.0).astype(BF16)
    qn = jnp.tile(q_norm.astype(F32), LANES // HEAD_DIM).reshape(1, LANES)
    kn = jnp.tile(k_norm.astype(F32), LANES // HEAD_DIM).reshape(1, LANES)
    row_spec = lambda width: pl.BlockSpec((tm, width), lambda i: (i, 0))
    const = lambda shape: pl.BlockSpec(shape, lambda i: (0,) * len(shape))
    hm_spec = pl.BlockSpec((N_KV_HEADS, tm, HEAD_DIM), lambda i: (0, i, 0))
    tab_spec = pl.BlockSpec((tm, LANES), lambda i: (i % ntab, 0))
    return pl.pallas_call(
        _proj_body,
        out_shape=(jax.ShapeDtypeStruct((rows, ATTN_WIDTH), BF16),
                   jax.ShapeDtypeStruct((N_KV_HEADS, rows, HEAD_DIM), BF16),
                   jax.ShapeDtypeStruct((N_KV_HEADS, rows, HEAD_DIM), BF16),
                   jax.ShapeDtypeStruct((rows, KV_WIDTH), F32),
                   jax.ShapeDtypeStruct((rows, KV_WIDTH), F32),
                   jax.ShapeDtypeStruct((rows, IDX_HEADS * IDX_DIM), BF16),
                   jax.ShapeDtypeStruct((rows, IDX_DIM), BF16),
                   jax.ShapeDtypeStruct((rows, IDX_DIM), F32),
                   jax.ShapeDtypeStruct((rows, IDX_HEADS), F32)),
        grid=(rows // tm,),
        in_specs=[row_spec(D_MODEL), const((1, D_MODEL)), const((D_MODEL, IN_COLS_PAD)),
                  const((1, LANES)), const((1, LANES)),
                  tab_spec, tab_spec, tab_spec, const((LANES, LANES))],
        out_specs=(row_spec(ATTN_WIDTH), hm_spec, hm_spec, row_spec(KV_WIDTH), row_spec(KV_WIDTH),
                   row_spec(IDX_HEADS * IDX_DIM), row_spec(IDX_DIM), row_spec(IDX_DIM), row_spec(IDX_HEADS)),
        compiler_params=_cparams(("arbitrary",)),
        name="dsa_project",
    )(x2d, g.astype(F32).reshape(1, D_MODEL), w, qn, kn, cc, s1, s2, bd)


def _sortable(x):
    x = jnp.where(x == 0.0, 0.0, x)
    bits = lax.bitcast_convert_type(x, I32)
    return jnp.where(bits < 0, bits ^ jnp.int32(0x7FFFFFFF), bits)


def _attn_body(nk_ref, q_ref, qi_ref, wi_ref, lim_ref, x_ref, k_ref, v_ref, ki_ref, wo_ref,
               o_ref, keys_scr, bias_scr, *, tq, tk, topk):
    i = pl.program_id(1)
    nb = nk_ref[i]
    ncol = tk // LANES
    qi = qi_ref[...].astype(F32)
    wi = wi_ref[...]
    lim = lim_ref[...]
    qih = [qi[:, h * IDX_DIM:(h + 1) * IDX_DIM].astype(BF16) for h in range(IDX_HEADS)]
    wih = [wi[:, h:h + 1] for h in range(IDX_HEADS)]
    nt = (((1,), (1,)), ((), ()))
    lane_tk = lax.broadcasted_iota(I32, (tq, tk), 1)

    def score_blk(kb, carry):
        off = pl.multiple_of(kb * tk, tk)
        kib = ki_ref[pl.ds(off, tk), :]
        sc = jnp.zeros((tq, tk), F32)
        for h in range(IDX_HEADS):
            lg = lax.dot_general(qih[h], kib, nt, preferred_element_type=F32)
            sc = sc + jnp.maximum(lg, 0.0) * wih[h]
        sc = jnp.where(lane_tk + off < lim, sc, NEG)
        keys_scr[kb] = _sortable(sc)
        return carry

    lax.fori_loop(0, nb, score_blk, 0)

    def count_ge(cand):
        def body(kb, acc):
            m = jnp.where(keys_scr[kb] >= cand, 1.0, 0.0)
            part = m[:, :LANES]
            for c in range(1, ncol):
                part = part + m[:, c * LANES:(c + 1) * LANES]
            return acc + part

        acc = lax.fori_loop(0, nb, body, jnp.zeros((tq, LANES), F32))
        return jnp.sum(acc, axis=1, keepdims=True)

    kf = float(topk)

    def bisect(it, thr):
        cand = thr + lax.shift_left(jnp.int32(1), 31 - it)
        return jnp.where(count_ge(cand) >= kf, cand, thr)

    thr = lax.fori_loop(0, 32, bisect, jnp.full((tq, 1), INT_MIN, I32))
    need = kf - count_ge(thr + 1)

    tri = (lax.broadcasted_iota(I32, (LANES, LANES), 0)
           <= lax.broadcasted_iota(I32, (LANES, LANES), 1)).astype(BF16)
    valid_key = jnp.int32(_HALF_NEG_KEY)

    def select_blk(kb, seen):
        keys = keys_scr[kb]
        cols = []
        for c in range(ncol):
            kc = keys[:, c * LANES:(c + 1) * LANES]
            eq = kc == thr
            eqf = jnp.where(eq, 1.0, 0.0)
            incl = jnp.dot(eqf.astype(BF16), tri, preferred_element_type=F32)
            take = eq & ((seen + incl - eqf) < need)
            sel = ((kc > thr) | take) & (kc > valid_key)
            cols.append(jnp.where(sel, 0.0, NEG))
            seen = seen + incl[:, LANES - 1:]
        bias_scr[kb] = jnp.concatenate(cols, axis=1)
        return seen

    lax.fori_loop(0, nb, select_blk, jnp.zeros((tq, 1), F32))

    q = q_ref[...].astype(F32)
    gq = N_HEADS // N_KV_HEADS
    outs = []
    for g in range(N_KV_HEADS):
        qg = jnp.concatenate([q[:, (g * gq + r) * HEAD_DIM:(g * gq + r + 1) * HEAD_DIM] for r in range(gq)],
                             axis=0).astype(BF16)

        def attn_blk(kb, carry, g=g, qg=qg):
            m, l, acc = carry
            off = pl.multiple_of(kb * tk, tk)
            kblk = k_ref[g, pl.ds(off, tk), :]
            vblk = v_ref[g, pl.ds(off, tk), :]
            s = lax.dot_general(qg, kblk, nt, preferred_element_type=F32)
            bias = bias_scr[kb]
            s = s + jnp.concatenate([bias] * gq, axis=0)
            m_new = jnp.maximum(m, jnp.max(s, axis=1, keepdims=True))
            alpha = jnp.exp(m - m_new)
            p = jnp.exp(s - m_new)
            l_new = alpha * l + jnp.sum(p, axis=1, keepdims=True)
            acc_new = alpha * acc + jnp.dot(p.astype(BF16), vblk, preferred_element_type=F32)
            return m_new, l_new, acc_new

        init = (jnp.full((gq * tq, 1), NEG, F32), jnp.zeros((gq * tq, 1), F32),
                jnp.zeros((gq * tq, HEAD_DIM), F32))
        m, l, acc = lax.fori_loop(0, nb, attn_blk, init)
        og = acc / l
        outs += [og[r * tq:(r + 1) * tq, :] for r in range(gq)]
    o = jnp.concatenate(outs, axis=1).astype(BF16)
    o_ref[...] = x_ref[...] + jnp.dot(o, wo_ref[...], preferred_element_type=F32)


def _dsa_attend(x2d, q, qi, wi, limit, k_hm, v_hm, ki, w_o, nkeys, *, batch, tq, tk, topk):
    rows = x2d.shape[0]
    per_batch = rows // batch
    nqt = per_batch // tq
    s_len = ki.shape[0] // batch
    nblk = s_len // tk
    row_spec = lambda width: pl.BlockSpec((tq, width), lambda b, i, nk: (b * nqt + i, 0))
    kv_spec = pl.BlockSpec((N_KV_HEADS, s_len, HEAD_DIM), lambda b, i, nk: (0, b, 0))
    grid_spec = pltpu.PrefetchScalarGridSpec(
        num_scalar_prefetch=1,
        grid=(batch, nqt),
        in_specs=[row_spec(ATTN_WIDTH), row_spec(IDX_HEADS * IDX_DIM), row_spec(IDX_HEADS), row_spec(1),
                  row_spec(D_MODEL), kv_spec, kv_spec,
                  pl.BlockSpec((s_len, IDX_DIM), lambda b, i, nk: (b, 0)),
                  pl.BlockSpec((ATTN_WIDTH, D_MODEL), lambda b, i, nk: (0, 0))],
        out_specs=row_spec(D_MODEL),
        scratch_shapes=[pltpu.VMEM((nblk, tq, tk), I32), pltpu.VMEM((nblk, tq, tk), F32)],
    )
    return pl.pallas_call(
        functools.partial(_attn_body, tq=tq, tk=tk, topk=topk),
        out_shape=jax.ShapeDtypeStruct(x2d.shape, F32),
        grid_spec=grid_spec,
        compiler_params=_cparams(("arbitrary", "arbitrary")),
        name="dsa_attend",
    )(nkeys, q, qi, wi, limit, x2d, k_hm, v_hm, ki, w_o.astype(BF16))


def _round_up(x, m):
    return (x + m - 1) // m * m


def kernel(x_prompt, x_sample, state_ssm_re, state_ssm_im, cache_k, cache_v, cache_kidx, cache_conv,
           norm_mix, norm_ffn, ssm_lambda_re, ssm_lambda_im, ssm_log_dt, ssm_b_re, ssm_b_im,
           ssm_c_re, ssm_c_im, ssm_d, ssm_w_glu, ssm_b_glu, attn_w_in, attn_q_norm, attn_k_norm,
           attn_w_o, ffn_w_up, ffn_conv_w, ffn_conv_b, ffn_w_down):
    b_p, t_p, _ = x_prompt.shape
    b_s, t_s, _ = x_sample.shape
    past = cache_k.shape[2]
    xp = x_prompt.astype(F32).reshape(b_p * t_p, D_MODEL)
    xs = x_sample.astype(F32).reshape(b_s * t_s, D_MODEL)

    ops = _s5_operators(ssm_lambda_re[0], ssm_lambda_im[0], ssm_log_dt[0], ssm_b_re[0], ssm_b_im[0],
                        ssm_c_re[0], ssm_c_im[0], ssm_d[0])
    zero_state = jnp.zeros((S5_OCT, b_p, 2 * S5_HALF), F32)
    zp, htp = _s5_scan(_norm_cast(xp, norm_mix[0]), ops, zero_state, batch=b_p, seq=t_p)
    zs, hts = _s5_scan(_norm_cast(xs, norm_mix[0]), ops, _state_to_oct(state_ssm_re[0], state_ssm_im[0]),
                       batch=b_s, seq=t_s)
    xp = _glu_residual(zp, xp, ssm_w_glu[0], ssm_b_glu[0])
    xs = _glu_residual(zs, xs, ssm_w_glu[0], ssm_b_glu[0])
    re_p, im_p = _oct_to_state(htp, b_p)
    re_s, im_s = _oct_to_state(hts, b_s)

    ffn0 = _ffn_weights(ffn_w_up[0], ffn_conv_w[0], ffn_conv_b[0], ffn_w_down[0])
    xp, conv_p0 = _conv_ffn(xp, jnp.zeros((b_p, CONV_W - 1, D_FF), F32), norm_ffn[0], ffn0, batch=b_p, seq=t_p)
    xs, conv_s0 = _conv_ffn(xs, cache_conv[0], norm_ffn[0], ffn0, batch=b_s, seq=t_s)

    pos_p = jnp.arange(t_p, dtype=I32)
    pos_s = past + jnp.arange(t_s, dtype=I32)
    topk_p = min(TOPK_MAX, t_p // 4)
    topk_s = min(TOPK_MAX, (past + t_s) // 4)

    (q_p, khm_p, vhm_p, kf_p, vf_p, qi_p, kib_p, kif_p, wi_p) = _dsa_project(
        xp, norm_mix[1], attn_w_in[0], attn_q_norm[0], attn_k_norm[0], pos_p)
    tq_p = min(t_p, 128)
    tk_p = min(t_p, 512)
    lim_p = jnp.tile((pos_p // CHUNK + 1) * CHUNK, b_p).reshape(b_p * t_p, 1)
    q_end = (jnp.arange(t_p // tq_p, dtype=I32) + 1) * tq_p
    nk_p = jnp.minimum((q_end + tk_p - 1) // tk_p, t_p // tk_p).astype(I32)
    xp = _dsa_attend(xp, q_p, qi_p, wi_p, lim_p, khm_p, vhm_p, kib_p, attn_w_o[0], nk_p,
                     batch=b_p, tq=tq_p, tk=tk_p, topk=topk_p)

    (q_s, khm_s, vhm_s, kf_s, vf_s, qi_s, kib_s, kif_s, wi_s) = _dsa_project(
        xs, norm_mix[1], attn_w_in[0], attn_q_norm[0], attn_k_norm[0], pos_s)
    tk_s = 3 * LANES
    s_all = past + t_s
    s_pad = _round_up(s_all, tk_s)

    def with_cache(cache, new_hm):
        heads = new_hm.shape[0]
        c = cache.astype(BF16).transpose(2, 0, 1, 3)
        n = new_hm.reshape(heads, b_s, t_s, new_hm.shape[-1])
        z = jnp.zeros((heads, b_s, s_pad - s_all, new_hm.shape[-1]), BF16)
        return jnp.concatenate([c, n, z], axis=2).reshape(heads, b_s * s_pad, new_hm.shape[-1])

    k_all = with_cache(cache_k[0], khm_s)
    v_all = with_cache(cache_v[0], vhm_s)
    ki_all = with_cache(cache_kidx[0][:, :, None, :], kib_s[None])[0]
    lim_s = jnp.full((b_s * t_s, 1), s_all, I32)
    nk_s = jnp.full((1,), s_pad // tk_s, I32)
    xs = _dsa_attend(xs, q_s, qi_s, wi_s, lim_s, k_all, v_all, ki_all, attn_w_o[0], nk_s,
                     batch=b_s, tq=t_s, tk=tk_s, topk=topk_s)

    ffn1 = _ffn_weights(ffn_w_up[1], ffn_conv_w[1], ffn_conv_b[1], ffn_w_down[1])
    xp, conv_p1 = _conv_ffn(xp, jnp.zeros((b_p, CONV_W - 1, D_FF), F32), norm_ffn[1], ffn1, batch=b_p, seq=t_p)
    xs, conv_s1 = _conv_ffn(xs, cache_conv[1], norm_ffn[1], ffn1, batch=b_s, seq=t_s)

    dt = x_prompt.dtype
    kv_shape_p = (1, b_p, t_p, N_KV_HEADS, HEAD_DIM)
    kv_shape_s = (1, b_s, t_s, N_KV_HEADS, HEAD_DIM)
    return (xp.reshape(x_prompt.shape).astype(dt), xs.reshape(x_sample.shape).astype(x_sample.dtype),
            re_p[None].astype(dt), im_p[None].astype(dt),
            re_s[None].astype(state_ssm_re.dtype), im_s[None].astype(state_ssm_im.dtype),
            kf_p.reshape(kv_shape_p).astype(dt), vf_p.reshape(kv_shape_p).astype(dt),
            kif_p.reshape(1, b_p, t_p, IDX_DIM).astype(dt),
            kf_s.reshape(kv_shape_s).astype(dt), vf_s.reshape(kv_shape_s).astype(dt),
            kif_s.reshape(1, b_s, t_s, IDX_DIM).astype(dt),
            jnp.stack([conv_p0, conv_p1]).astype(dt), jnp.stack([conv_s0, conv_s1]).astype(dt))
```

```python
import functools
import math

import jax
import jax.numpy as jnp
from jax import lax
from jax.experimental import pallas as pl
from jax.experimental.pallas import tpu as pltpu

F32 = jnp.float32
BF16 = jnp.bfloat16
I32 = jnp.int32

D_MODEL = 1024
CHUNK = 64
SSM_GROUP = 16
SSM_GROUPS = D_MODEL // SSM_GROUP
SSM_STATE = 64
N_HEADS = 16
N_KV_HEADS = 4
HEAD_DIM = 64
ATTN_WIDTH = N_HEADS * HEAD_DIM
KV_WIDTH = N_KV_HEADS * HEAD_DIM
IDX_HEADS = 8
IDX_DIM = 64
TOPK_MAX = 256
ROT_DIM = HEAD_DIM // 4
ROPE_THETA = 500000.0
OFF_K = ATTN_WIDTH
OFF_V = OFF_K + KV_WIDTH
OFF_QI = OFF_V + KV_WIDTH
OFF_KI = OFF_QI + IDX_HEADS * IDX_DIM
OFF_WI = OFF_KI + IDX_DIM
IN_COLS = OFF_WI + IDX_HEADS
D_FF = 2816
CONV_W = 3
EPS = 1e-6
NEG = -1e30

LANES = 128
SUBLANES = 8
VMEM_LIMIT_BYTES = 56 * 1024 * 1024

S5_L = 16
S5_OCT = D_MODEL // LANES
S5_OCT_GROUPS = LANES // SSM_GROUP
S5_HALF = S5_OCT_GROUPS * SSM_STATE
S5_PW_ROWS = 24

FFN_FC = 256
FFN_NC = D_FF // FFN_FC
IN_COLS_PAD = 2176
LOG2E = math.log2(math.e)
BISECT_MAX_ITERS = 40
ATTN_ROW_BLOCK = 64


def _cparams(sem):
    return pltpu.CompilerParams(dimension_semantics=sem, vmem_limit_bytes=VMEM_LIMIT_BYTES)


def _rms(x, g):
    ms = jnp.mean(x * x, axis=-1, keepdims=True)
    return x * lax.rsqrt(ms + EPS) * g


def _norm_body(x_ref, g_ref, o_ref):
    o_ref[...] = _rms(x_ref[...], g_ref[...]).astype(BF16)


def _norm_cast(x2d, g):
    rows = x2d.shape[0]
    tm = min(rows, 1024)
    return pl.pallas_call(
        _norm_body,
        out_shape=jax.ShapeDtypeStruct((rows, D_MODEL), BF16),
        grid=(rows // tm,),
        in_specs=[pl.BlockSpec((tm, D_MODEL), lambda i: (i, 0)),
                  pl.BlockSpec((1, D_MODEL), lambda i: (0, 0))],
        out_specs=pl.BlockSpec((tm, D_MODEL), lambda i: (i, 0)),
        compiler_params=_cparams(("arbitrary",)),
        name="rmsnorm_cast",
    )(x2d, g.reshape(1, D_MODEL))


def _s5_operators(lam_re, lam_im, log_dt, b_re, b_im, c_re, c_im, d_skip):
    hp = lax.Precision.HIGHEST
    L = S5_L
    lre = jnp.minimum(lam_re.astype(F32), -1e-4)
    lim = lam_im.astype(F32)
    dt = jnp.exp(log_dt.astype(F32))[:, None]
    n = jnp.arange(L + 1, dtype=F32)[:, None, None]
    mag = jnp.exp(n * (lre * dt)[None])
    ang = n * (lim * dt)[None]
    pr = mag * jnp.cos(ang)
    pi = mag * jnp.sin(ang)
    a_re, a_im = pr[1], pi[1]
    den = lre * lre + lim * lim
    n_re = a_re - 1.0
    f_re = (n_re * lre + a_im * lim) / den
    f_im = (a_im * lre - n_re * lim) / den
    br = b_re.astype(F32)
    bi = b_im.astype(F32)
    bb_re = f_re[..., None] * br - f_im[..., None] * bi
    bb_im = f_re[..., None] * bi + f_im[..., None] * br
    cr = c_re.astype(F32)
    ci = c_im.astype(F32)
    eye = jnp.eye(S5_OCT_GROUPS, dtype=F32)
    og = (S5_OCT, S5_OCT_GROUPS)

    car = cr[None] * pr[:L, :, None, :] - ci[None] * pi[:L, :, None, :]
    cai = cr[None] * pi[:L, :, None, :] + ci[None] * pr[:L, :, None, :]
    kl = (jnp.einsum("lgdp,gpc->lgdc", car, bb_re, precision=hp)
          - jnp.einsum("lgdp,gpc->lgdc", cai, bb_im, precision=hp))
    kbd = jnp.einsum("ljgdc,gh->ljgchd", kl.reshape((L,) + og + (SSM_GROUP, SSM_GROUP)), eye)
    kbd = kbd.reshape(L, S5_OCT, LANES, LANES)
    kpad = jnp.concatenate([jnp.zeros_like(kbd[:1]), kbd], axis=0)
    k2 = []
    for dlt in range(L // 2):
        top = jnp.concatenate([kpad[2 * dlt + 1], kpad[2 * dlt + 2]], axis=-1)
        bot = jnp.concatenate([kpad[2 * dlt], kpad[2 * dlt + 1]], axis=-1)
        k2.append(jnp.concatenate([top, bot], axis=-2))
    k2 = jnp.stack(k2, axis=1).astype(BF16)

    def bd_in(m):
        m = m.reshape(og + (SSM_STATE, SSM_GROUP))
        return jnp.einsum("jgpc,gh->jgchp", m, eye).reshape(S5_OCT, LANES, S5_HALF)

    ab_re = a_re[..., None] * bb_re - a_im[..., None] * bb_im
    ab_im = a_re[..., None] * bb_im + a_im[..., None] * bb_re
    f2 = jnp.concatenate([
        jnp.concatenate([bd_in(ab_re), bd_in(ab_im)], axis=-1),
        jnp.concatenate([bd_in(bb_re), bd_in(bb_im)], axis=-1)], axis=-2).astype(BF16)

    def bd_out(m):
        m = m.reshape(og + (SSM_GROUP, SSM_STATE))
        return jnp.einsum("jgcp,gh->jgphc", m, eye).reshape(S5_OCT, S5_HALF, LANES)

    c1r = cr * a_re[:, None, :] - ci * a_im[:, None, :]
    c1i = cr * a_im[:, None, :] + ci * a_re[:, None, :]
    c2 = jnp.concatenate([
        jnp.concatenate([bd_out(cr), bd_out(c1r)], axis=-1),
        jnp.concatenate([-bd_out(ci), -bd_out(c1i)], axis=-1)], axis=-2).astype(BF16)

    def oct_lanes(p):
        return p.reshape((L + 1,) + og + (SSM_STATE,)).transpose(1, 0, 2, 3).reshape(S5_OCT, L + 1, S5_HALF)

    pw = jnp.concatenate([oct_lanes(pr), oct_lanes(pi)], axis=-1)
    pw = jnp.pad(pw, ((0, 0), (0, S5_PW_ROWS - (L + 1)), (0, 0)))
    dsk = d_skip.astype(F32).reshape(S5_OCT, 1, LANES)
    dsk2 = jnp.concatenate([dsk, dsk], axis=-1)
    return f2, k2, c2, pw, dsk2


def _s5_body(*refs, bpt, nck):
    L = S5_L
    H = S5_HALF
    u_refs = refs[:L]
    f2_ref, k2_ref, c2_ref, pw_ref, dsk_ref, h0_ref, z_ref, ht_ref, v_scr, hp_scr = refs[L:]

    def cmul(xr, xi, n):
        pr = pw_ref[0, n:n + 1, :H]
        pi = pw_ref[0, n:n + 1, H:]
        return xr * pr - xi * pi, xr * pi + xi * pr

    upair = [jnp.concatenate([u_refs[2 * s][...], u_refs[2 * s + 1][...]], axis=1)
             for s in range(L // 2)]

    vr = vi = None
    for s2 in range(L // 2):
        p = jnp.dot(upair[s2], f2_ref[0], preferred_element_type=F32)
        qr, qi = cmul(p[:, :H], p[:, H:], L - 2 - 2 * s2)
        vr = qr if vr is None else vr + qr
        vi = qi if vi is None else vi + qi
    v_scr[:, :H] = vr
    v_scr[:, H:] = vi

    alr = pw_ref[0, L:L + 1, :H]
    ali = pw_ref[0, L:L + 1, H:]
    for b in range(bpt):
        def step(k, carry, b=b):
            hr, hi = carry
            row = b * nck + k
            hp_scr[pl.ds(row, 1), :H] = hr
            hp_scr[pl.ds(row, 1), H:] = hi
            wr = v_scr[pl.ds(row, 1), :H]
            wi = v_scr[pl.ds(row, 1), H:]
            return hr * alr - hi * ali + wr, hr * ali + hi * alr + wi

        h0r = h0_ref[0, 0, b:b + 1, :H]
        h0i = h0_ref[0, 0, b:b + 1, H:]
        hr, hi = lax.fori_loop(0, nck, step, (h0r, h0i))
        ht_ref[0, 0, b:b + 1, :H] = hr
        ht_ref[0, 0, b:b + 1, H:] = hi

    hpr = hp_scr[:, :H]
    hpi = hp_scr[:, H:]
    dsk = dsk_ref[0]
    for t2 in range(L // 2):
        gr, gi = cmul(hpr, hpi, 2 * t2 + 1)
        g = jnp.concatenate([gr, gi], axis=1).astype(BF16)
        y = jnp.dot(g, c2_ref[0], preferred_element_type=F32)
        for s2 in range(t2 + 1):
            y = y + jnp.dot(upair[s2], k2_ref[0, t2 - s2], preferred_element_type=F32)
        y = y + dsk * upair[t2].astype(F32)
        z_ref[0, :, t2 * 2 * LANES:(t2 + 1) * 2 * LANES] = jax.nn.gelu(y).astype(BF16)


def _s5_scan(u2d, ops, h0, *, batch, seq):
    L = S5_L
    f2, k2, c2, pw, dsk2 = ops
    nck = seq // L
    n_rows = batch * nck
    bpt = 1 if nck % 16 == 0 else batch
    tr = bpt * nck
    n_tiles = n_rows // tr
    u_rows = u2d.reshape(n_rows, L * D_MODEL)
    h0 = h0.reshape(S5_OCT, n_tiles, bpt, 2 * S5_HALF)
    in_specs = [pl.BlockSpec((tr, LANES), functools.partial(lambda j, i, s: (i, s * S5_OCT + j), s=s))
                for s in range(L)]
    in_specs += [
        pl.BlockSpec((1, 2 * LANES, 2 * S5_HALF), lambda j, i: (j, 0, 0)),
        pl.BlockSpec((1, L // 2, 2 * LANES, 2 * LANES), lambda j, i: (j, 0, 0, 0)),
        pl.BlockSpec((1, 2 * S5_HALF, 2 * LANES), lambda j, i: (j, 0, 0)),
        pl.BlockSpec((1, S5_PW_ROWS, 2 * S5_HALF), lambda j, i: (j, 0, 0)),
        pl.BlockSpec((1, 1, 2 * LANES), lambda j, i: (j, 0, 0)),
        pl.BlockSpec((1, 1, bpt, 2 * S5_HALF), lambda j, i: (j, i, 0, 0)),
    ]
    z, ht = pl.pallas_call(
        functools.partial(_s5_body, bpt=bpt, nck=nck),
        out_shape=(jax.ShapeDtypeStruct((S5_OCT, n_rows, L * LANES), BF16),
                   jax.ShapeDtypeStruct((S5_OCT, n_tiles, bpt, 2 * S5_HALF), F32)),
        grid=(S5_OCT, n_tiles),
        in_specs=in_specs,
        out_specs=(pl.BlockSpec((1, tr, L * LANES), lambda j, i: (j, i, 0)),
                   pl.BlockSpec((1, 1, bpt, 2 * S5_HALF), lambda j, i: (j, i, 0, 0))),
        scratch_shapes=[pltpu.VMEM((tr, 2 * S5_HALF), F32), pltpu.VMEM((tr, 2 * S5_HALF), F32)],
        compiler_params=_cparams(("arbitrary", "arbitrary")),
        name="s5_scan",
    )(*([u_rows] * L), f2, k2, c2, pw, dsk2, h0)
    return z, ht


def _state_to_oct(s_re, s_im):
    b = s_re.shape[0]
    r = s_re.astype(F32).reshape(b, S5_OCT, S5_HALF).transpose(1, 0, 2)
    i = s_im.astype(F32).reshape(b, S5_OCT, S5_HALF).transpose(1, 0, 2)
    return jnp.concatenate([r, i], axis=-1)


def _oct_to_state(ht, batch):
    ht = ht.reshape(S5_OCT, batch, 2 * S5_HALF)
    r = ht[..., :S5_HALF].transpose(1, 0, 2).reshape(batch, SSM_GROUPS, SSM_STATE)
    i = ht[..., S5_HALF:].transpose(1, 0, 2).reshape(batch, SSM_GROUPS, SSM_STATE)
    return r, i


def _glu_body(z_ref, x_ref, w_ref, b_ref, o_ref, lhs_scr, *, tn):
    L = S5_L
    for t in range(L):
        for j in range(S5_OCT):
            lhs_scr[t * tn:(t + 1) * tn, j * LANES:(j + 1) * LANES] = z_ref[j, :, t * LANES:(t + 1) * LANES]
    g = jnp.dot(lhs_scr[...], w_ref[...], preferred_element_type=F32) + b_ref[...]
    o = g[:, :D_MODEL] * jax.nn.sigmoid(g[:, D_MODEL:])
    for t in range(L):
        sl = slice(t * D_MODEL, (t + 1) * D_MODEL)
        o_ref[:, sl] = x_ref[:, sl] + o[t * tn:(t + 1) * tn, :]


def _glu_residual(z, x2d, w_glu, b_glu):
    L = S5_L
    n_rows = z.shape[1]
    tn = 32 if n_rows % 32 == 0 else 16
    x_rows = x2d.reshape(n_rows, L * D_MODEL)
    out = pl.pallas_call(
        functools.partial(_glu_body, tn=tn),
        out_shape=jax.ShapeDtypeStruct((n_rows, L * D_MODEL), F32),
        grid=(n_rows // tn,),
        in_specs=[pl.BlockSpec((S5_OCT, tn, L * LANES), lambda i: (0, i, 0)),
                  pl.BlockSpec((tn, L * D_MODEL), lambda i: (i, 0)),
                  pl.BlockSpec((D_MODEL, 2 * D_MODEL), lambda i: (0, 0)),
                  pl.BlockSpec((1, 2 * D_MODEL), lambda i: (0, 0))],
        out_specs=pl.BlockSpec((tn, L * D_MODEL), lambda i: (i, 0)),
        scratch_shapes=[pltpu.VMEM((L * tn, D_MODEL), BF16)],
        compiler_params=_cparams(("arbitrary",)),
        name="glu_residual",
    )(z, x_rows, w_glu.astype(BF16), b_glu.astype(F32).reshape(1, 2 * D_MODEL))
    return out.reshape(x2d.shape)


def _ffn_body(x_ref, xh_ref, cs_ref, g_ref, wa_ref, wb_ref, cw_ref, cb_ref, wd_ref,
              o_ref, co_ref, *, tm, tiles_per_seq):
    i = pl.program_id(0)
    first = (i % tiles_per_seq) == 0
    x = x_ref[...]
    g = g_ref[...]
    xn = _rms(x, g).astype(BF16)
    xnh = _rms(xh_ref[...], g).astype(BF16)
    rows = lax.broadcasted_iota(I32, (tm, FFN_FC), 0)

    def chunk(c, acc):
        ha = jnp.dot(xn, wa_ref[c], preferred_element_type=F32)
        hh = jnp.dot(xnh, wa_ref[c], preferred_element_type=F32)
        hh = jnp.where(first, cs_ref[0, c], hh)
        co_ref[0, c] = ha[tm - SUBLANES:, :]
        h6 = hh[6:7, :]
        h7 = hh[7:8, :]
        prev1 = jnp.where(rows == 0, h7, pltpu.roll(ha, 1, 0))
        prev2 = jnp.where(rows == 0, h6, jnp.where(rows == 1, h7, pltpu.roll(ha, 2, 0)))
        cw = cw_ref[c]
        cv = cb_ref[c] + cw[0:1, :] * prev2 + cw[1:2, :] * prev1 + cw[2:3, :] * ha
        hb = jnp.dot(xn, wb_ref[c], preferred_element_type=F32)
        gate = (jax.nn.gelu(cv) * hb).astype(BF16)
        return acc + jnp.dot(gate, wd_ref[c], preferred_element_type=F32)

    acc = lax.fori_loop(0, FFN_NC, chunk, jnp.zeros((tm, D_MODEL), F32))
    o_ref[...] = x + acc


def _ffn_weights(w_up, conv_w, conv_b, w_down):
    wa = w_up[:, :D_FF].astype(BF16).reshape(D_MODEL, FFN_NC, FFN_FC).transpose(1, 0, 2)
    wb = w_up[:, D_FF:].astype(BF16).reshape(D_MODEL, FFN_NC, FFN_FC).transpose(1, 0, 2)
    cw = jnp.pad(conv_w.astype(F32), ((0, SUBLANES - CONV_W), (0, 0)))
    cw = cw.reshape(SUBLANES, FFN_NC, FFN_FC).transpose(1, 0, 2)
    cb = conv_b.astype(F32).reshape(FFN_NC, 1, FFN_FC)
    wd = w_down.astype(BF16).reshape(FFN_NC, FFN_FC, D_MODEL)
    return wa, wb, cw, cb, wd


def _conv_ffn(x2d, conv_state, g, weights, *, batch, seq):
    wa, wb, cw, cb, wd = weights
    tm = min(seq, 512)
    tiles_per_seq = seq // tm
    hb = tm // SUBLANES
    cs = jnp.pad(conv_state.astype(F32), ((0, 0), (SUBLANES - (CONV_W - 1), 0), (0, 0)))
    cs = cs.reshape(batch, SUBLANES, FFN_NC, FFN_FC).transpose(0, 2, 1, 3)
    const3 = lambda i: (0, 0, 0)
    out, co = pl.pallas_call(
        functools.partial(_ffn_body, tm=tm, tiles_per_seq=tiles_per_seq),
        out_shape=(jax.ShapeDtypeStruct(x2d.shape, F32),
                   jax.ShapeDtypeStruct((batch, FFN_NC, SUBLANES, FFN_FC), F32)),
        grid=(batch * tiles_per_seq,),
        in_specs=[pl.BlockSpec((tm, D_MODEL), lambda i: (i, 0)),
                  pl.BlockSpec((SUBLANES, D_MODEL), lambda i: (jnp.maximum(i * hb - 1, 0), 0)),
                  pl.BlockSpec((1, FFN_NC, SUBLANES, FFN_FC), lambda i: (i // tiles_per_seq, 0, 0, 0)),
                  pl.BlockSpec((1, D_MODEL), lambda i: (0, 0)),
                  pl.BlockSpec((FFN_NC, D_MODEL, FFN_FC), const3),
                  pl.BlockSpec((FFN_NC, D_MODEL, FFN_FC), const3),
                  pl.BlockSpec((FFN_NC, SUBLANES, FFN_FC), const3),
                  pl.BlockSpec((FFN_NC, 1, FFN_FC), const3),
                  pl.BlockSpec((FFN_NC, FFN_FC, D_MODEL), const3)],
        out_specs=(pl.BlockSpec((tm, D_MODEL), lambda i: (i, 0)),
                   pl.BlockSpec((1, FFN_NC, SUBLANES, FFN_FC), lambda i: (i // tiles_per_seq, 0, 0, 0))),
        compiler_params=_cparams(("arbitrary",)),
        name="conv_ffn",
    )(x2d, x2d, cs, g.astype(F32).reshape(1, D_MODEL), wa, wb, cw, cb, wd)
    new_state = co[:, :, SUBLANES - (CONV_W - 1):, :].transpose(0, 2, 1, 3).reshape(batch, CONV_W - 1, D_FF)
    return out, new_state


def _rope_tables(pos):
    half = ROT_DIM // 2
    inv = ROPE_THETA ** (-jnp.arange(half, dtype=F32) / half)
    ang = pos.astype(F32)[:, None] * inv[None, :]
    cos = jnp.cos(ang)
    sin = jnp.sin(ang)
    lane = jnp.arange(LANES) % HEAD_DIM
    idx = lane % half
    cc = jnp.where(lane[None] < ROT_DIM, cos[:, idx], 1.0)
    s1 = jnp.where((lane[None] >= half) & (lane[None] < ROT_DIM), sin[:, idx], 0.0)
    s2 = jnp.where(lane[None] < half, -sin[:, idx], 0.0)
    return cc.astype(F32), s1.astype(F32), s2.astype(F32)


def _proj_body(x_ref, g_ref, w_ref, qn_ref, kn_ref, cc_ref, s1_ref, s2_ref, bd_ref,
               q_ref, khm_ref, vhm_ref, kf_ref, vf_ref, qi_ref, kib_ref, kif_ref, wi_ref):
    xn = _rms(x_ref[...], g_ref[...]).astype(BF16)
    proj = jnp.dot(xn, w_ref[...], preferred_element_type=F32)
    cc = cc_ref[...]
    s1 = s1_ref[...]
    s2 = s2_ref[...]
    bd = bd_ref[...]

    def rope(y):
        return y * cc + pltpu.roll(y, ROT_DIM // 2, 1) * s1 + pltpu.roll(y, LANES - ROT_DIM // 2, 1) * s2

    def head_norm(y, gain):
        sq = y * y
        hi = sq.astype(BF16)
        lo = (sq - hi.astype(F32)).astype(BF16)
        ms = (jnp.dot(hi, bd, preferred_element_type=F32) + jnp.dot(lo, bd, preferred_element_type=F32))
        return y * lax.rsqrt(ms + EPS) * gain

    qn = qn_ref[...]
    kn = kn_ref[...]
    scale = HEAD_DIM ** -0.5 * LOG2E
    one_col = jnp.where(lax.broadcasted_iota(I32, (x_ref.shape[0], HEAD_DIM), 1) == 0, 1.0, 0.0)
    for c in range(ATTN_WIDTH // LANES):
        y = proj[:, c * LANES:(c + 1) * LANES]
        q_ref[:, c * LANES:(c + 1) * LANES] = (rope(head_norm(y, qn)) * scale).astype(BF16)
    for c in range(KV_WIDTH // LANES):
        y = proj[:, OFF_K + c * LANES:OFF_K + (c + 1) * LANES]
        k = rope(head_norm(y, kn))
        kf_ref[:, c * LANES:(c + 1) * LANES] = k
        khm_ref[2 * c] = k[:, :HEAD_DIM].astype(BF16)
        khm_ref[2 * c + 1] = k[:, HEAD_DIM:].astype(BF16)
        v = proj[:, OFF_V + c * LANES:OFF_V + (c + 1) * LANES]
        vf_ref[:, c * LANES:(c + 1) * LANES] = v
        vhm_ref[2 * c] = jnp.concatenate([v[:, :HEAD_DIM], one_col], axis=1).astype(BF16)
        vhm_ref[2 * c + 1] = jnp.concatenate([v[:, HEAD_DIM:], one_col], axis=1).astype(BF16)
    iscale = IDX_DIM ** -0.5
    for c in range(IDX_HEADS * IDX_DIM // LANES):
        y = proj[:, OFF_QI + c * LANES:OFF_QI + (c + 1) * LANES]
        qi_ref[:, c * LANES:(c + 1) * LANES] = (rope(y) * iscale).astype(BF16)
    tail = proj[:, OFF_KI:OFF_KI + LANES]
    ki = rope(tail)[:, :IDX_DIM]
    kif_ref[...] = ki
    kib_ref[...] = ki.astype(BF16)
    wi_ref[...] = tail[:, IDX_DIM:IDX_DIM + IDX_HEADS] * (IDX_HEADS ** -0.5)


def _dsa_project(x2d, g, w_in, q_norm, k_norm, pos):
    rows = x2d.shape[0]
    tm = min(rows, 256)
    w = jnp.pad(w_in.astype(BF16), ((0, 0), (0, IN_COLS_PAD - IN_COLS)))
    if pos.shape[0] < tm:
        pos = jnp.tile(pos, tm // pos.shape[0])
    ntab = pos.shape[0] // tm
    cc, s1, s2 = _rope_tables(pos)
    lane = jnp.arange(LANES)
    bd = jnp.where((lane[:, None] // HEAD_DIM) == (lane[None, :] // HEAD_DIM), 1.0 / HEAD_DIM, 0.0).astype(BF16)
    qn = jnp.tile(q_norm.astype(F32), LANES // HEAD_DIM).reshape(1, LANES)
    kn = jnp.tile(k_norm.astype(F32), LANES // HEAD_DIM).reshape(1, LANES)
    row_spec = lambda width: pl.BlockSpec((tm, width), lambda i: (i, 0))
    const = lambda shape: pl.BlockSpec(shape, lambda i: (0,) * len(shape))
    hm_spec = pl.BlockSpec((N_KV_HEADS, tm, HEAD_DIM), lambda i: (0, i, 0))
    vhm_spec = pl.BlockSpec((N_KV_HEADS, tm, LANES), lambda i: (0, i, 0))
    tab_spec = pl.BlockSpec((tm, LANES), lambda i: (i % ntab, 0))
    return pl.pallas_call(
        _proj_body,
        out_shape=(jax.ShapeDtypeStruct((rows, ATTN_WIDTH), BF16),
                   jax.ShapeDtypeStruct((N_KV_HEADS, rows, HEAD_DIM), BF16),
                   jax.ShapeDtypeStruct((N_KV_HEADS, rows, LANES), BF16),
                   jax.ShapeDtypeStruct((rows, KV_WIDTH), F32),
                   jax.ShapeDtypeStruct((rows, KV_WIDTH), F32),
                   jax.ShapeDtypeStruct((rows, IDX_HEADS * IDX_DIM), BF16),
                   jax.ShapeDtypeStruct((rows, IDX_DIM), BF16),
                   jax.ShapeDtypeStruct((rows, IDX_DIM), F32),
                   jax.ShapeDtypeStruct((rows, IDX_HEADS), F32)),
        grid=(rows // tm,),
        in_specs=[row_spec(D_MODEL), const((1, D_MODEL)), const((D_MODEL, IN_COLS_PAD)),
                  const((1, LANES)), const((1, LANES)),
                  tab_spec, tab_spec, tab_spec, const((LANES, LANES))],
        out_specs=(row_spec(ATTN_WIDTH), hm_spec, vhm_spec, row_spec(KV_WIDTH), row_spec(KV_WIDTH),
                   row_spec(IDX_HEADS * IDX_DIM), row_spec(IDX_DIM), row_spec(IDX_DIM), row_spec(IDX_HEADS)),
        compiler_params=_cparams(("arbitrary",)),
        name="dsa_project",
    )(x2d, g.astype(F32).reshape(1, D_MODEL), w, qn, kn, cc, s1, s2, bd)


def _attn_body(nk_ref, q_ref, qi_ref, wi_ref, lim_ref, x_ref, k_ref, v_ref, ki_ref, wo_ref,
               o_ref, sc_scr, bias_scr, qg_scr, s_scr, p_scr, m_scr, acc_scr, *, tq, tk, topk):
    i = pl.program_id(1)
    nb = nk_ref[i]
    ncol = tk // LANES
    kf = float(topk)
    qi = qi_ref[...].astype(F32)
    wi = wi_ref[...]
    lim = lim_ref[...]
    qih = [qi[:, h * IDX_DIM:(h + 1) * IDX_DIM].astype(BF16) for h in range(IDX_HEADS)]
    wih = [wi[:, h:h + 1] for h in range(IDX_HEADS)]
    nt = (((1,), (1,)), ((), ()))
    lane_tk = lax.broadcasted_iota(I32, (tq, tk), 1)

    def fold(x, op):
        part = x[:, :LANES]
        for c in range(1, ncol):
            part = op(part, x[:, c * LANES:(c + 1) * LANES])
        return part

    def score_blk(kb, carry):
        mn, mx = carry
        off = pl.multiple_of(kb * tk, tk)
        kib = ki_ref[pl.ds(off, tk), :]
        sc = jnp.zeros((tq, tk), F32)
        for h in range(IDX_HEADS):
            lg = lax.dot_general(qih[h], kib, nt, preferred_element_type=F32)
            sc = sc + jnp.maximum(lg, 0.0) * wih[h]
        allowed = lane_tk + off < lim
        sc_scr[kb] = jnp.where(allowed, sc, NEG)
        mn = jnp.minimum(mn, fold(jnp.where(allowed, sc, -NEG), jnp.minimum))
        mx = jnp.maximum(mx, fold(jnp.where(allowed, sc, NEG), jnp.maximum))
        return mn, mx

    mn, mx = lax.fori_loop(0, nb, score_blk,
                           (jnp.full((tq, LANES), -NEG, F32), jnp.full((tq, LANES), NEG, F32)))
    lo0 = jnp.min(mn, axis=1, keepdims=True)
    hi0 = jnp.max(mx, axis=1, keepdims=True)
    hi0 = hi0 + (jnp.abs(hi0) + 1e-30) * 1e-6

    def count_ge(x):
        def body(kb, acc):
            return acc + fold(jnp.where(sc_scr[kb] >= x, 1.0, 0.0), jnp.add)

        acc = lax.fori_loop(0, nb, body, jnp.zeros((tq, LANES), F32))
        return jnp.sum(acc, axis=1, keepdims=True)

    def any_over(cnt_lo):
        return jnp.max(jnp.where(cnt_lo > kf, 1.0, 0.0)) > 0.5

    def bis_cond(c):
        return c[4]

    def bis_body(c):
        it, lo, hi, cnt_lo, _ = c
        mid = 0.5 * (lo + hi)
        cnt = count_ge(mid)
        up = cnt >= kf
        lo = jnp.where(up, mid, lo)
        hi = jnp.where(up, hi, mid)
        cnt_lo = jnp.where(up, cnt, cnt_lo)
        return it + 1, lo, hi, cnt_lo, jnp.logical_and(any_over(cnt_lo), it + 1 < BISECT_MAX_ITERS)

    cnt0 = count_ge(lo0)
    _, lo, hi, cnt_lo, _ = lax.while_loop(bis_cond, bis_body, (jnp.int32(0), lo0, hi0, cnt0, any_over(cnt0)))
    ties = any_over(cnt_lo)

    @pl.when(jnp.logical_not(ties))
    def _():
        def body(kb, carry):
            s = sc_scr[kb]
            bias_scr[kb] = jnp.where((s >= lo) & (s > 0.5 * NEG), 0.0, NEG)
            return carry

        lax.fori_loop(0, nb, body, 0)

    @pl.when(ties)
    def _():
        need = kf - count_ge(hi)
        tri = (lax.broadcasted_iota(I32, (LANES, LANES), 0)
               <= lax.broadcasted_iota(I32, (LANES, LANES), 1)).astype(BF16)

        def body(kb, seen):
            s = sc_scr[kb]
            cols = []
            for c in range(ncol):
                sc = s[:, c * LANES:(c + 1) * LANES]
                tie = (sc >= lo) & (sc < hi)
                tief = jnp.where(tie, 1.0, 0.0)
                incl = jnp.dot(tief.astype(BF16), tri, preferred_element_type=F32)
                take = tie & ((seen + incl - tief) < need)
                sel = ((sc >= hi) | take) & (sc > 0.5 * NEG)
                cols.append(jnp.where(sel, 0.0, NEG))
                seen = seen + incl[:, LANES - 1:]
            bias_scr[kb] = jnp.concatenate(cols, axis=1)
            return seen

        lax.fori_loop(0, nb, body, jnp.zeros((tq, 1), F32))

    q = q_ref[...].astype(F32)
    gq = N_HEADS // N_KV_HEADS
    rb = min(tq, ATTN_ROW_BLOCK)
    for g in range(N_KV_HEADS):
        for r in range(gq):
            h = g * gq + r
            qg_scr[g, r * tq:(r + 1) * tq, :] = q[:, h * HEAD_DIM:(h + 1) * HEAD_DIM].astype(BF16)
    m_scr[...] = jnp.full(m_scr.shape, NEG, F32)
    acc_scr[...] = jnp.zeros(acc_scr.shape, F32)

    def attn_blk(kb, carry):
        off = pl.multiple_of(kb * tk, tk)
        for g in range(N_KV_HEADS):
            kblk = k_ref[g, pl.ds(off, tk), :]
            vblk = v_ref[g, pl.ds(off, tk), :]
            s_scr[g] = lax.dot_general(qg_scr[g], kblk, nt, preferred_element_type=F32)
            for r0 in range(0, gq * tq, rb):
                rows = slice(r0, r0 + rb)
                s = s_scr[g, rows, :] + bias_scr[kb, r0 % tq:r0 % tq + rb, :]
                m_old = m_scr[g, rows, :]
                m_new = jnp.maximum(m_old, jnp.max(s, axis=1, keepdims=True))
                p_scr[g, rows, :] = jnp.exp2(s - jnp.concatenate([m_new] * ncol, axis=1)).astype(BF16)
                acc_scr[g, rows, :] = jnp.exp2(m_old - m_new) * acc_scr[g, rows, :]
                m_scr[g, rows, :] = m_new
            acc_scr[g] += jnp.dot(p_scr[g], vblk, preferred_element_type=F32)
        return carry

    lax.fori_loop(0, nb, attn_blk, 0)
    outs = []
    for g in range(N_KV_HEADS):
        for r in range(gq):
            acc = acc_scr[g, r * tq:(r + 1) * tq, :]
            outs.append(acc[:, :HEAD_DIM] / acc[:, HEAD_DIM:HEAD_DIM + 1])
    o = jnp.concatenate(outs, axis=1).astype(BF16)
    o_ref[...] = x_ref[...] + jnp.dot(o, wo_ref[...], preferred_element_type=F32)


def _dsa_attend(x2d, q, qi, wi, limit, k_hm, v_hm, ki, w_o, nkeys, *, batch, tq, tk, topk):
    rows = x2d.shape[0]
    per_batch = rows // batch
    nqt = per_batch // tq
    s_len = ki.shape[0] // batch
    nblk = s_len // tk
    gq = N_HEADS // N_KV_HEADS
    row_spec = lambda width: pl.BlockSpec((tq, width), lambda b, i, nk: (b * nqt + i, 0))
    k_spec = pl.BlockSpec((N_KV_HEADS, s_len, HEAD_DIM), lambda b, i, nk: (0, b, 0))
    v_spec = pl.BlockSpec((N_KV_HEADS, s_len, LANES), lambda b, i, nk: (0, b, 0))
    grid_spec = pltpu.PrefetchScalarGridSpec(
        num_scalar_prefetch=1,
        grid=(batch, nqt),
        in_specs=[row_spec(ATTN_WIDTH), row_spec(IDX_HEADS * IDX_DIM), row_spec(IDX_HEADS), row_spec(1),
                  row_spec(D_MODEL), k_spec, v_spec,
                  pl.BlockSpec((s_len, IDX_DIM), lambda b, i, nk: (b, 0)),
                  pl.BlockSpec((ATTN_WIDTH, D_MODEL), lambda b, i, nk: (0, 0))],
        out_specs=row_spec(D_MODEL),
        scratch_shapes=[pltpu.VMEM((nblk, tq, tk), F32), pltpu.VMEM((nblk, tq, tk), F32),
                        pltpu.VMEM((N_KV_HEADS, gq * tq, HEAD_DIM), BF16),
                        pltpu.VMEM((N_KV_HEADS, gq * tq, tk), F32), pltpu.VMEM((N_KV_HEADS, gq * tq, tk), BF16),
                        pltpu.VMEM((N_KV_HEADS, gq * tq, LANES), F32), pltpu.VMEM((N_KV_HEADS, gq * tq, LANES), F32)],
    )
    return pl.pallas_call(
        functools.partial(_attn_body, tq=tq, tk=tk, topk=topk),
        out_shape=jax.ShapeDtypeStruct(x2d.shape, F32),
        grid_spec=grid_spec,
        compiler_params=_cparams(("arbitrary", "arbitrary")),
        name="dsa_attend",
    )(nkeys, q, qi, wi, limit, x2d, k_hm, v_hm, ki, w_o.astype(BF16))


def _round_up(x, m):
    return (x + m - 1) // m * m


def kernel(x_prompt, x_sample, state_ssm_re, state_ssm_im, cache_k, cache_v, cache_kidx, cache_conv,
           norm_mix, norm_ffn, ssm_lambda_re, ssm_lambda_im, ssm_log_dt, ssm_b_re, ssm_b_im,
           ssm_c_re, ssm_c_im, ssm_d, ssm_w_glu, ssm_b_glu, attn_w_in, attn_q_norm, attn_k_norm,
           attn_w_o, ffn_w_up, ffn_conv_w, ffn_conv_b, ffn_w_down):
    b_p, t_p, _ = x_prompt.shape
    b_s, t_s, _ = x_sample.shape
    past = cache_k.shape[2]
    xp = x_prompt.astype(F32).reshape(b_p * t_p, D_MODEL)
    xs = x_sample.astype(F32).reshape(b_s * t_s, D_MODEL)

    ops = _s5_operators(ssm_lambda_re[0], ssm_lambda_im[0], ssm_log_dt[0], ssm_b_re[0], ssm_b_im[0],
                        ssm_c_re[0], ssm_c_im[0], ssm_d[0])
    zero_state = jnp.zeros((S5_OCT, b_p, 2 * S5_HALF), F32)
    zp, htp = _s5_scan(_norm_cast(xp, norm_mix[0]), ops, zero_state, batch=b_p, seq=t_p)
    zs, hts = _s5_scan(_norm_cast(xs, norm_mix[0]), ops, _state_to_oct(state_ssm_re[0], state_ssm_im[0]),
                       batch=b_s, seq=t_s)
    xp = _glu_residual(zp, xp, ssm_w_glu[0], ssm_b_glu[0])
    xs = _glu_residual(zs, xs, ssm_w_glu[0], ssm_b_glu[0])
    re_p, im_p = _oct_to_state(htp, b_p)
    re_s, im_s = _oct_to_state(hts, b_s)

    ffn0 = _ffn_weights(ffn_w_up[0], ffn_conv_w[0], ffn_conv_b[0], ffn_w_down[0])
    xp, conv_p0 = _conv_ffn(xp, jnp.zeros((b_p, CONV_W - 1, D_FF), F32), norm_ffn[0], ffn0, batch=b_p, seq=t_p)
    xs, conv_s0 = _conv_ffn(xs, cache_conv[0], norm_ffn[0], ffn0, batch=b_s, seq=t_s)

    pos_p = jnp.arange(t_p, dtype=I32)
    pos_s = past + jnp.arange(t_s, dtype=I32)
    topk_p = min(TOPK_MAX, t_p // 4)
    topk_s = min(TOPK_MAX, (past + t_s) // 4)

    (q_p, khm_p, vhm_p, kf_p, vf_p, qi_p, kib_p, kif_p, wi_p) = _dsa_project(
        xp, norm_mix[1], attn_w_in[0], attn_q_norm[0], attn_k_norm[0], pos_p)
    tq_p = min(t_p, 128)
    tk_p = min(t_p, 512)
    lim_p = jnp.tile((pos_p // CHUNK + 1) * CHUNK, b_p).reshape(b_p * t_p, 1)
    q_end = (jnp.arange(t_p // tq_p, dtype=I32) + 1) * tq_p
    nk_p = jnp.minimum((q_end + tk_p - 1) // tk_p, t_p // tk_p).astype(I32)
    xp = _dsa_attend(xp, q_p, qi_p, wi_p, lim_p, khm_p, vhm_p, kib_p, attn_w_o[0], nk_p,
                     batch=b_p, tq=tq_p, tk=tk_p, topk=topk_p)

    (q_s, khm_s, vhm_s, kf_s, vf_s, qi_s, kib_s, kif_s, wi_s) = _dsa_project(
        xs, norm_mix[1], attn_w_in[0], attn_q_norm[0], attn_k_norm[0], pos_s)
    tk_s = 3 * LANES
    s_all = past + t_s
    s_pad = _round_up(s_all, tk_s)

    def with_cache(cache, new_hm):
        heads = new_hm.shape[0]
        c = cache.astype(BF16).transpose(2, 0, 1, 3)
        n = new_hm.reshape(heads, b_s, t_s, new_hm.shape[-1])
        z = jnp.zeros((heads, b_s, s_pad - s_all, new_hm.shape[-1]), BF16)
        return jnp.concatenate([c, n, z], axis=2).reshape(heads, b_s * s_pad, new_hm.shape[-1])

    ones_col = jnp.zeros(cache_v[0].shape[:-1] + (LANES - HEAD_DIM,), cache_v.dtype).at[..., 0].set(1.0)
    k_all = with_cache(cache_k[0], khm_s)
    v_all = with_cache(jnp.concatenate([cache_v[0], ones_col], axis=-1), vhm_s)
    ki_all = with_cache(cache_kidx[0][:, :, None, :], kib_s[None])[0]
    lim_s = jnp.full((b_s * t_s, 1), s_all, I32)
    nk_s = jnp.full((1,), s_pad // tk_s, I32)
    xs = _dsa_attend(xs, q_s, qi_s, wi_s, lim_s, k_all, v_all, ki_all, attn_w_o[0], nk_s,
                     batch=b_s, tq=t_s, tk=tk_s, topk=topk_s)

    ffn1 = _ffn_weights(ffn_w_up[1], ffn_conv_w[1], ffn_conv_b[1], ffn_w_down[1])
    xp, conv_p1 = _conv_ffn(xp, jnp.zeros((b_p, CONV_W - 1, D_FF), F32), norm_ffn[1], ffn1, batch=b_p, seq=t_p)
    xs, conv_s1 = _conv_ffn(xs, cache_conv[1], norm_ffn[1], ffn1, batch=b_s, seq=t_s)

    dt = x_prompt.dtype
    kv_shape_p = (1, b_p, t_p, N_KV_HEADS, HEAD_DIM)
    kv_shape_s = (1, b_s, t_s, N_KV_HEADS, HEAD_DIM)
    return (xp.reshape(x_prompt.shape).astype(dt), xs.reshape(x_sample.shape).astype(x_sample.dtype),
            re_p[None].astype(dt), im_p[None].astype(dt),
            re_s[None].astype(state_ssm_re.dtype), im_s[None].astype(state_ssm_im.dtype),
            kf_p.reshape(kv_shape_p).astype(dt), vf_p.reshape(kv_shape_p).astype(dt),
            kif_p.reshape(1, b_p, t_p, IDX_DIM).astype(dt),
            kf_s.reshape(kv_shape_s).astype(dt), vf_s.reshape(kv_shape_s).astype(dt),
            kif_s.reshape(1, b_s, t_s, IDX_DIM).astype(dt),
            jnp.stack([conv_p0, conv_p1]).astype(dt), jnp.stack([conv_s0, conv_s1]).astype(dt))
```

```python
import functools
import math

import jax
import jax.numpy as jnp
from jax import lax
from jax.experimental import pallas as pl
from jax.experimental.pallas import tpu as pltpu

F32 = jnp.float32
BF16 = jnp.bfloat16
I32 = jnp.int32

D_MODEL = 1024
CHUNK = 64
SSM_GROUP = 16
SSM_GROUPS = D_MODEL // SSM_GROUP
SSM_STATE = 64
N_HEADS = 16
N_KV_HEADS = 4
HEAD_DIM = 64
ATTN_WIDTH = N_HEADS * HEAD_DIM
KV_WIDTH = N_KV_HEADS * HEAD_DIM
IDX_HEADS = 8
IDX_DIM = 64
TOPK_MAX = 256
ROT_DIM = HEAD_DIM // 4
ROPE_THETA = 500000.0
OFF_K = ATTN_WIDTH
OFF_V = OFF_K + KV_WIDTH
OFF_QI = OFF_V + KV_WIDTH
OFF_KI = OFF_QI + IDX_HEADS * IDX_DIM
OFF_WI = OFF_KI + IDX_DIM
IN_COLS = OFF_WI + IDX_HEADS
D_FF = 2816
CONV_W = 3
EPS = 1e-6
NEG = -1e30

LANES = 128
SUBLANES = 8
VMEM_LIMIT_BYTES = 56 * 1024 * 1024

S5_L = 16
S5_OCT = D_MODEL // LANES
S5_OCT_GROUPS = LANES // SSM_GROUP
S5_HALF = S5_OCT_GROUPS * SSM_STATE
S5_PW_ROWS = 24

FFN_FC = 256
FFN_NC = D_FF // FFN_FC
assert FFN_NC % 2 == 1
IN_COLS_PAD = 2176
LOG2E = math.log2(math.e)
BISECT_MAX_ITERS = 40
ATTN_ROW_BLOCK = 64


def _cparams(sem):
    return pltpu.CompilerParams(dimension_semantics=sem, vmem_limit_bytes=VMEM_LIMIT_BYTES)


def _rms(x, g):
    ms = jnp.mean(x * x, axis=-1, keepdims=True)
    return x * lax.rsqrt(ms + EPS) * g


def _norm_body(x_ref, g_ref, o_ref):
    o_ref[...] = _rms(x_ref[...], g_ref[...])


def _norm(x2d, g):
    rows = x2d.shape[0]
    tm = min(rows, 1024)
    return pl.pallas_call(
        _norm_body,
        out_shape=jax.ShapeDtypeStruct((rows, D_MODEL), F32),
        grid=(rows // tm,),
        in_specs=[pl.BlockSpec((tm, D_MODEL), lambda i: (i, 0)),
                  pl.BlockSpec((1, D_MODEL), lambda i: (0, 0))],
        out_specs=pl.BlockSpec((tm, D_MODEL), lambda i: (i, 0)),
        compiler_params=_cparams(("arbitrary",)),
        name="rmsnorm",
    )(x2d, g.astype(F32).reshape(1, D_MODEL))


def _s5_operators(lam_re, lam_im, log_dt, b_re, b_im, c_re, c_im, d_skip):
    L = S5_L
    lre = jnp.minimum(lam_re.astype(F32), -1e-4)
    lim = lam_im.astype(F32)
    dt = jnp.exp(log_dt.astype(F32))[:, None]
    n = jnp.arange(L + 1, dtype=F32)[:, None, None]
    mag = jnp.exp(n * (lre * dt)[None])
    ang = n * (lim * dt)[None]
    pr = mag * jnp.cos(ang)
    pi = mag * jnp.sin(ang)
    a_re, a_im = pr[1], pi[1]
    den = lre * lre + lim * lim
    n_re = a_re - 1.0
    f_re = (n_re * lre + a_im * lim) / den
    f_im = (a_im * lre - n_re * lim) / den
    br = b_re.astype(F32)
    bi = b_im.astype(F32)
    bb_re = f_re[..., None] * br - f_im[..., None] * bi
    bb_im = f_re[..., None] * bi + f_im[..., None] * br
    cr = c_re.astype(F32)
    ci = c_im.astype(F32)
    eye = jnp.eye(S5_OCT_GROUPS, dtype=F32)
    og = (S5_OCT, S5_OCT_GROUPS)

    tg = lambda m: jnp.moveaxis(m, 0, -1)
    crt, cit = tg(cr)[None, :, None], tg(ci)[None, :, None]
    prt, pit = tg(pr[:L].transpose(1, 0, 2))[:, None, None], tg(pi[:L].transpose(1, 0, 2))[:, None, None]
    brt, bit = tg(bb_re).transpose(1, 0, 2)[None, None], tg(bb_im).transpose(1, 0, 2)[None, None]
    kl = jnp.sum((crt * prt - cit * pit) * brt - (crt * pit + cit * prt) * bit, axis=3)
    kl = kl.transpose(0, 3, 1, 2)
    kbd = jnp.einsum("ljgdc,gh->ljgchd", kl.reshape((L,) + og + (SSM_GROUP, SSM_GROUP)), eye)
    kbd = kbd.reshape(L, S5_OCT, LANES, LANES)
    kpad = jnp.concatenate([jnp.zeros_like(kbd[:1]), kbd], axis=0)
    k2 = []
    for dlt in range(L // 2):
        top = jnp.concatenate([kpad[2 * dlt + 1], kpad[2 * dlt + 2]], axis=-1)
        bot = jnp.concatenate([kpad[2 * dlt], kpad[2 * dlt + 1]], axis=-1)
        k2.append(jnp.concatenate([top, bot], axis=-2))
    k2 = jnp.stack(k2, axis=1).astype(BF16)

    def bd_in(m):
        m = m.reshape(og + (SSM_STATE, SSM_GROUP))
        return jnp.einsum("jgpc,gh->jgchp", m, eye).reshape(S5_OCT, LANES, S5_HALF)

    ab_re = a_re[..., None] * bb_re - a_im[..., None] * bb_im
    ab_im = a_re[..., None] * bb_im + a_im[..., None] * bb_re
    f2 = jnp.concatenate([
        jnp.concatenate([bd_in(ab_re), bd_in(ab_im)], axis=-1),
        jnp.concatenate([bd_in(bb_re), bd_in(bb_im)], axis=-1)], axis=-2).astype(BF16)

    def bd_out(m):
        m = m.reshape(og + (SSM_GROUP, SSM_STATE))
        return jnp.einsum("jgcp,gh->jgphc", m, eye).reshape(S5_OCT, S5_HALF, LANES)

    c1r = cr * a_re[:, None, :] - ci * a_im[:, None, :]
    c1i = cr * a_im[:, None, :] + ci * a_re[:, None, :]
    c2 = jnp.concatenate([
        jnp.concatenate([bd_out(cr), bd_out(c1r)], axis=-1),
        jnp.concatenate([-bd_out(ci), -bd_out(c1i)], axis=-1)], axis=-2).astype(BF16)

    def oct_lanes(p):
        return p.reshape((L + 1,) + og + (SSM_STATE,)).transpose(1, 0, 2, 3).reshape(S5_OCT, L + 1, S5_HALF)

    pw = jnp.concatenate([oct_lanes(pr), oct_lanes(pi)], axis=-1)
    pw = jnp.pad(pw, ((0, 0), (0, S5_PW_ROWS - (L + 1)), (0, 0)))
    dsk = d_skip.astype(F32).reshape(S5_OCT, 1, LANES)
    dsk2 = jnp.concatenate([dsk, dsk], axis=-1)
    return f2, k2, c2, pw, dsk2


def _s5_body(u_ref, f2_ref, k2_ref, c2_ref, pw_ref, dsk_ref, h0_ref, z_ref, ht_ref, v_scr, hp_scr, *, bpt, nck):
    L = S5_L
    H = S5_HALF
    tr = bpt * nck

    def cmul(xr, xi, n):
        pr = pw_ref[0, n:n + 1, :H]
        pi = pw_ref[0, n:n + 1, H:]
        return xr * pr - xi * pi, xr * pi + xi * pr

    ufp = [jnp.concatenate([u_ref[pl.ds(2 * s, tr, stride=L), :], u_ref[pl.ds(2 * s + 1, tr, stride=L), :]], axis=1)
           for s in range(L // 2)]
    upair = [u.astype(BF16) for u in ufp]

    vr = vi = None
    for s2 in range(L // 2):
        p = jnp.dot(upair[s2], f2_ref[0], preferred_element_type=F32)
        qr, qi = cmul(p[:, :H], p[:, H:], L - 2 - 2 * s2)
        vr = qr if vr is None else vr + qr
        vi = qi if vi is None else vi + qi
    v_scr[:, :H] = vr
    v_scr[:, H:] = vi

    alr = pw_ref[0, L:L + 1, :H]
    ali = pw_ref[0, L:L + 1, H:]
    for b in range(bpt):
        def step(k, carry, b=b):
            hr, hi = carry
            row = b * nck + k
            hp_scr[pl.ds(row, 1), :H] = hr
            hp_scr[pl.ds(row, 1), H:] = hi
            wr = v_scr[pl.ds(row, 1), :H]
            wi = v_scr[pl.ds(row, 1), H:]
            return hr * alr - hi * ali + wr, hr * ali + hi * alr + wi

        h0r = h0_ref[0, 0, b:b + 1, :H]
        h0i = h0_ref[0, 0, b:b + 1, H:]
        hr, hi = lax.fori_loop(0, nck, step, (h0r, h0i))
        ht_ref[0, 0, b:b + 1, :H] = hr
        ht_ref[0, 0, b:b + 1, H:] = hi

    hpr = hp_scr[:, :H]
    hpi = hp_scr[:, H:]
    dsk = dsk_ref[0]
    for t2 in range(L // 2):
        gr, gi = cmul(hpr, hpi, 2 * t2 + 1)
        g = jnp.concatenate([gr, gi], axis=1).astype(BF16)
        y = jnp.dot(g, c2_ref[0], preferred_element_type=F32)
        for s2 in range(t2 + 1):
            y = y + jnp.dot(upair[s2], k2_ref[0, t2 - s2], preferred_element_type=F32)
        z = jax.nn.gelu(y + dsk * ufp[t2])
        z_ref[pl.ds(2 * t2, tr, stride=L), :] = z[:, :LANES]
        z_ref[pl.ds(2 * t2 + 1, tr, stride=L), :] = z[:, LANES:]


def _s5_scan(u2d, ops, h0, *, batch, seq):
    L = S5_L
    f2, k2, c2, pw, dsk2 = ops
    nck = seq // L
    n_rows = batch * nck
    bpt = 1 if nck % SUBLANES == 0 else batch
    tr = bpt * nck
    n_tiles = n_rows // tr
    h0 = h0.reshape(S5_OCT, n_tiles, bpt, 2 * S5_HALF)
    tok_spec = pl.BlockSpec((tr * L, LANES), lambda j, i: (i, j))
    in_specs = [
        tok_spec,
        pl.BlockSpec((1, 2 * LANES, 2 * S5_HALF), lambda j, i: (j, 0, 0)),
        pl.BlockSpec((1, L // 2, 2 * LANES, 2 * LANES), lambda j, i: (j, 0, 0, 0)),
        pl.BlockSpec((1, 2 * S5_HALF, 2 * LANES), lambda j, i: (j, 0, 0)),
        pl.BlockSpec((1, S5_PW_ROWS, 2 * S5_HALF), lambda j, i: (j, 0, 0)),
        pl.BlockSpec((1, 1, 2 * LANES), lambda j, i: (j, 0, 0)),
        pl.BlockSpec((1, 1, bpt, 2 * S5_HALF), lambda j, i: (j, i, 0, 0)),
    ]
    z, ht = pl.pallas_call(
        functools.partial(_s5_body, bpt=bpt, nck=nck),
        out_shape=(jax.ShapeDtypeStruct(u2d.shape, F32),
                   jax.ShapeDtypeStruct((S5_OCT, n_tiles, bpt, 2 * S5_HALF), F32)),
        grid=(S5_OCT, n_tiles),
        in_specs=in_specs,
        out_specs=(tok_spec, pl.BlockSpec((1, 1, bpt, 2 * S5_HALF), lambda j, i: (j, i, 0, 0))),
        scratch_shapes=[pltpu.VMEM((tr, 2 * S5_HALF), F32), pltpu.VMEM((tr, 2 * S5_HALF), F32)],
        compiler_params=_cparams(("arbitrary", "arbitrary")),
        name="s5_scan",
    )(u2d, f2, k2, c2, pw, dsk2, h0)
    return z, ht


def _state_to_oct(s_re, s_im):
    b = s_re.shape[0]
    r = s_re.astype(F32).reshape(b, S5_OCT, S5_HALF).transpose(1, 0, 2)
    i = s_im.astype(F32).reshape(b, S5_OCT, S5_HALF).transpose(1, 0, 2)
    return jnp.concatenate([r, i], axis=-1)


def _oct_to_state(ht, batch):
    ht = ht.reshape(S5_OCT, batch, 2 * S5_HALF)
    r = ht[..., :S5_HALF].transpose(1, 0, 2).reshape(batch, SSM_GROUPS, SSM_STATE)
    i = ht[..., S5_HALF:].transpose(1, 0, 2).reshape(batch, SSM_GROUPS, SSM_STATE)
    return r, i


def _glu_body(z_ref, x_ref, w_ref, b_ref, o_ref):
    g = jnp.dot(z_ref[...].astype(BF16), w_ref[...], preferred_element_type=F32) + b_ref[...]
    o_ref[...] = x_ref[...] + g[:, :D_MODEL] * jax.nn.sigmoid(g[:, D_MODEL:])


def _glu_residual(z, x2d, w_glu, b_glu):
    rows = x2d.shape[0]
    tm = min(rows, 512)
    row_spec = pl.BlockSpec((tm, D_MODEL), lambda i: (i, 0))
    return pl.pallas_call(
        _glu_body,
        out_shape=jax.ShapeDtypeStruct(x2d.shape, F32),
        grid=(rows // tm,),
        in_specs=[row_spec, row_spec,
                  pl.BlockSpec((D_MODEL, 2 * D_MODEL), lambda i: (0, 0)),
                  pl.BlockSpec((1, 2 * D_MODEL), lambda i: (0, 0))],
        out_specs=row_spec,
        compiler_params=_cparams(("arbitrary",)),
        name="glu_residual",
    )(z, x2d, w_glu.astype(BF16), b_glu.astype(F32).reshape(1, 2 * D_MODEL))


def _ffn_body(x_ref, xh_ref, cs_ref, g_ref, wa_ref, wb_ref, cw_ref, cb_ref, wd_ref,
              o_ref, co_ref, xn_scr, ha0, hb0, hh0, ha1, hb1, hh1, gt0, gt1, acc_scr, *, tm, tiles_per_seq):
    i = pl.program_id(0)
    first = (i % tiles_per_seq) == 0
    g = g_ref[...]
    xn_scr[...] = _rms(x_ref[...], g).astype(BF16)
    xnh = _rms(xh_ref[...], g).astype(BF16)
    rows = lax.broadcasted_iota(I32, (tm, FFN_FC), 0)
    sets = ((ha0, hb0, hh0, gt0), (ha1, hb1, hh1, gt1))

    def up(c, s):
        ha, hb, hh, _ = sets[s]
        xn = xn_scr[...]
        ha[...] = jnp.dot(xn, wa_ref[c], preferred_element_type=F32)
        hb[...] = jnp.dot(xn, wb_ref[c], preferred_element_type=F32)
        hh[...] = jnp.dot(xnh, wa_ref[c], preferred_element_type=F32)

    def act(c, s):
        ha_ref, hb_ref, hh_ref, gt = sets[s]
        ha = ha_ref[...]
        hh = jnp.where(first, cs_ref[0, c], hh_ref[...])
        co_ref[0, c] = ha[tm - SUBLANES:, :]
        h6 = hh[6:7, :]
        h7 = hh[7:8, :]
        prev1 = jnp.where(rows == 0, h7, pltpu.roll(ha, 1, 0))
        prev2 = jnp.where(rows == 0, h6, jnp.where(rows == 1, h7, pltpu.roll(ha, 2, 0)))
        cw = cw_ref[c]
        cv = cb_ref[c] + cw[0:1, :] * prev2 + cw[1:2, :] * prev1 + cw[2:3, :] * ha
        gt[...] = (jax.nn.gelu(cv) * hb_ref[...]).astype(BF16)

    def down(c, s):
        acc_scr[...] += jnp.dot(sets[s][3][...], wd_ref[c], preferred_element_type=F32)

    acc_scr[...] = jnp.zeros(acc_scr.shape, F32)
    gt1[...] = jnp.zeros(gt1.shape, BF16)
    up(0, 0)

    def body(k, carry):
        c = 2 * k
        up(c + 1, 1)
        act(c, 0)
        down(jnp.maximum(c - 1, 0), 1)
        up(c + 2, 0)
        act(c + 1, 1)
        down(c, 0)
        return carry

    lax.fori_loop(0, (FFN_NC - 1) // 2, body, 0)
    act(FFN_NC - 1, 0)
    down(FFN_NC - 2, 1)
    down(FFN_NC - 1, 0)
    o_ref[...] = x_ref[...] + acc_scr[...]


def _ffn_weights(w_up, conv_w, conv_b, w_down):
    wa = w_up[:, :D_FF].astype(BF16).reshape(D_MODEL, FFN_NC, FFN_FC).transpose(1, 0, 2)
    wb = w_up[:, D_FF:].astype(BF16).reshape(D_MODEL, FFN_NC, FFN_FC).transpose(1, 0, 2)
    cw = jnp.pad(conv_w.astype(F32), ((0, SUBLANES - CONV_W), (0, 0)))
    cw = cw.reshape(SUBLANES, FFN_NC, FFN_FC).transpose(1, 0, 2)
    cb = conv_b.astype(F32).reshape(FFN_NC, 1, FFN_FC)
    wd = w_down.astype(BF16).reshape(FFN_NC, FFN_FC, D_MODEL)
    return wa, wb, cw, cb, wd


def _conv_ffn(x2d, conv_state, g, weights, *, batch, seq):
    wa, wb, cw, cb, wd = weights
    tm = min(seq, 512)
    tiles_per_seq = seq // tm
    hb = tm // SUBLANES
    cs = jnp.pad(conv_state.astype(F32), ((0, 0), (SUBLANES - (CONV_W - 1), 0), (0, 0)))
    cs = cs.reshape(batch, SUBLANES, FFN_NC, FFN_FC).transpose(0, 2, 1, 3)
    const3 = lambda i: (0, 0, 0)
    up_set = [pltpu.VMEM((tm, FFN_FC), F32), pltpu.VMEM((tm, FFN_FC), F32), pltpu.VMEM((SUBLANES, FFN_FC), F32)]
    gate = pltpu.VMEM((tm, FFN_FC), BF16)
    scratch = [pltpu.VMEM((tm, D_MODEL), BF16)] + up_set + up_set + [gate, gate, pltpu.VMEM((tm, D_MODEL), F32)]
    out, co = pl.pallas_call(
        functools.partial(_ffn_body, tm=tm, tiles_per_seq=tiles_per_seq),
        out_shape=(jax.ShapeDtypeStruct(x2d.shape, F32),
                   jax.ShapeDtypeStruct((batch, FFN_NC, SUBLANES, FFN_FC), F32)),
        grid=(batch * tiles_per_seq,),
        in_specs=[pl.BlockSpec((tm, D_MODEL), lambda i: (i, 0)),
                  pl.BlockSpec((SUBLANES, D_MODEL), lambda i: (jnp.maximum(i * hb - 1, 0), 0)),
                  pl.BlockSpec((1, FFN_NC, SUBLANES, FFN_FC), lambda i: (i // tiles_per_seq, 0, 0, 0)),
                  pl.BlockSpec((1, D_MODEL), lambda i: (0, 0)),
                  pl.BlockSpec((FFN_NC, D_MODEL, FFN_FC), const3),
                  pl.BlockSpec((FFN_NC, D_MODEL, FFN_FC), const3),
                  pl.BlockSpec((FFN_NC, SUBLANES, FFN_FC), const3),
                  pl.BlockSpec((FFN_NC, 1, FFN_FC), const3),
                  pl.BlockSpec((FFN_NC, FFN_FC, D_MODEL), const3)],
        out_specs=(pl.BlockSpec((tm, D_MODEL), lambda i: (i, 0)),
                   pl.BlockSpec((1, FFN_NC, SUBLANES, FFN_FC), lambda i: (i // tiles_per_seq, 0, 0, 0))),
        scratch_shapes=scratch,
        compiler_params=_cparams(("arbitrary",)),
        name="conv_ffn",
    )(x2d, x2d, cs, g.astype(F32).reshape(1, D_MODEL), wa, wb, cw, cb, wd)
    new_state = co[:, :, SUBLANES - (CONV_W - 1):, :].transpose(0, 2, 1, 3).reshape(batch, CONV_W - 1, D_FF)
    return out, new_state


def _rope_tables(pos):
    half = ROT_DIM // 2
    inv = ROPE_THETA ** (-jnp.arange(half, dtype=F32) / half)
    ang = pos.astype(F32)[:, None] * inv[None, :]
    cos = jnp.cos(ang)
    sin = jnp.sin(ang)
    lane = jnp.arange(LANES) % HEAD_DIM
    idx = lane % half
    cc = jnp.where(lane[None] < ROT_DIM, cos[:, idx], 1.0)
    s1 = jnp.where((lane[None] >= half) & (lane[None] < ROT_DIM), sin[:, idx], 0.0)
    s2 = jnp.where(lane[None] < half, -sin[:, idx], 0.0)
    return cc.astype(F32), s1.astype(F32), s2.astype(F32)


def _proj_body(x_ref, g_ref, w_ref, qn_ref, kn_ref, cc_ref, s1_ref, s2_ref, bd_ref,
               q_ref, khm_ref, vhm_ref, kf_ref, vf_ref, qi_ref, kib_ref, kif_ref, wi_ref):
    xn = _rms(x_ref[...], g_ref[...]).astype(BF16)
    proj = jnp.dot(xn, w_ref[...], preferred_element_type=F32)
    cc = cc_ref[...]
    s1 = s1_ref[...]
    s2 = s2_ref[...]
    bd = bd_ref[...]

    def rope(y):
        return y * cc + pltpu.roll(y, ROT_DIM // 2, 1) * s1 + pltpu.roll(y, LANES - ROT_DIM // 2, 1) * s2

    def head_norm(y, gain):
        sq = y * y
        hi = sq.astype(BF16)
        lo = (sq - hi.astype(F32)).astype(BF16)
        ms = (jnp.dot(hi, bd, preferred_element_type=F32) + jnp.dot(lo, bd, preferred_element_type=F32))
        return y * lax.rsqrt(ms + EPS) * gain

    qn = qn_ref[...]
    kn = kn_ref[...]
    scale = HEAD_DIM ** -0.5 * LOG2E
    one_col = jnp.where(lax.broadcasted_iota(I32, (x_ref.shape[0], HEAD_DIM), 1) == 0, 1.0, 0.0)
    for c in range(ATTN_WIDTH // LANES):
        y = proj[:, c * LANES:(c + 1) * LANES]
        q_ref[:, c * LANES:(c + 1) * LANES] = (rope(head_norm(y, qn)) * scale).astype(BF16)
    for c in range(KV_WIDTH // LANES):
        y = proj[:, OFF_K + c * LANES:OFF_K + (c + 1) * LANES]
        k = rope(head_norm(y, kn))
        kf_ref[:, c * LANES:(c + 1) * LANES] = k
        khm_ref[2 * c] = k[:, :HEAD_DIM].astype(BF16)
        khm_ref[2 * c + 1] = k[:, HEAD_DIM:].astype(BF16)
        v = proj[:, OFF_V + c * LANES:OFF_V + (c + 1) * LANES]
        vf_ref[:, c * LANES:(c + 1) * LANES] = v
        vhm_ref[2 * c] = jnp.concatenate([v[:, :HEAD_DIM], one_col], axis=1).astype(BF16)
        vhm_ref[2 * c + 1] = jnp.concatenate([v[:, HEAD_DIM:], one_col], axis=1).astype(BF16)
    iscale = IDX_DIM ** -0.5
    for c in range(IDX_HEADS * IDX_DIM // LANES):
        y = proj[:, OFF_QI + c * LANES:OFF_QI + (c + 1) * LANES]
        qi_ref[:, c * LANES:(c + 1) * LANES] = (rope(y) * iscale).astype(BF16)
    tail = proj[:, OFF_KI:OFF_KI + LANES]
    ki = rope(tail)[:, :IDX_DIM]
    kif_ref[...] = ki
    kib_ref[...] = ki.astype(BF16)
    wi_ref[...] = tail[:, IDX_DIM:IDX_DIM + IDX_HEADS] * (IDX_HEADS ** -0.5)


def _dsa_project(x2d, g, w_in, q_norm, k_norm, pos):
    rows = x2d.shape[0]
    tm = min(rows, 256)
    w = jnp.pad(w_in.astype(BF16), ((0, 0), (0, IN_COLS_PAD - IN_COLS)))
    if pos.shape[0] < tm:
        pos = jnp.tile(pos, tm // pos.shape[0])
    ntab = pos.shape[0] // tm
    cc, s1, s2 = _rope_tables(pos)
    lane = jnp.arange(LANES)
    bd = jnp.where((lane[:, None] // HEAD_DIM) == (lane[None, :] // HEAD_DIM), 1.0 / HEAD_DIM, 0.0).astype(BF16)
    qn = jnp.tile(q_norm.astype(F32), LANES // HEAD_DIM).reshape(1, LANES)
    kn = jnp.tile(k_norm.astype(F32), LANES // HEAD_DIM).reshape(1, LANES)
    row_spec = lambda width: pl.BlockSpec((tm, width), lambda i: (i, 0))
    const = lambda shape: pl.BlockSpec(shape, lambda i: (0,) * len(shape))
    hm_spec = pl.BlockSpec((N_KV_HEADS, tm, HEAD_DIM), lambda i: (0, i, 0))
    vhm_spec = pl.BlockSpec((N_KV_HEADS, tm, LANES), lambda i: (0, i, 0))
    tab_spec = pl.BlockSpec((tm, LANES), lambda i: (i % ntab, 0))
    return pl.pallas_call(
        _proj_body,
        out_shape=(jax.ShapeDtypeStruct((rows, ATTN_WIDTH), BF16),
                   jax.ShapeDtypeStruct((N_KV_HEADS, rows, HEAD_DIM), BF16),
                   jax.ShapeDtypeStruct((N_KV_HEADS, rows, LANES), BF16),
                   jax.ShapeDtypeStruct((rows, KV_WIDTH), F32),
                   jax.ShapeDtypeStruct((rows, KV_WIDTH), F32),
                   jax.ShapeDtypeStruct((rows, IDX_HEADS * IDX_DIM), BF16),
                   jax.ShapeDtypeStruct((rows, IDX_DIM), BF16),
                   jax.ShapeDtypeStruct((rows, IDX_DIM), F32),
                   jax.ShapeDtypeStruct((rows, IDX_HEADS), F32)),
        grid=(rows // tm,),
        in_specs=[row_spec(D_MODEL), const((1, D_MODEL)), const((D_MODEL, IN_COLS_PAD)),
                  const((1, LANES)), const((1, LANES)),
                  tab_spec, tab_spec, tab_spec, const((LANES, LANES))],
        out_specs=(row_spec(ATTN_WIDTH), hm_spec, vhm_spec, row_spec(KV_WIDTH), row_spec(KV_WIDTH),
                   row_spec(IDX_HEADS * IDX_DIM), row_spec(IDX_DIM), row_spec(IDX_DIM), row_spec(IDX_HEADS)),
        compiler_params=_cparams(("arbitrary",)),
        name="dsa_project",
    )(x2d, g.astype(F32).reshape(1, D_MODEL), w, qn, kn, cc, s1, s2, bd)


def _attn_body(nk_ref, q_ref, qi_ref, wi_ref, lim_ref, x_ref, k_ref, v_ref, ki_ref, wo_ref,
               o_ref, sc_scr, bias_scr, qg_scr, s_scr, p_scr, m_scr, acc_scr, *, tq, tk, topk):
    i = pl.program_id(1)
    nb = nk_ref[i]
    ncol = tk // LANES
    kf = float(topk)
    qi = qi_ref[...].astype(F32)
    wi = wi_ref[...]
    lim = lim_ref[...]
    qih = [qi[:, h * IDX_DIM:(h + 1) * IDX_DIM].astype(BF16) for h in range(IDX_HEADS)]
    wih = [wi[:, h:h + 1] for h in range(IDX_HEADS)]
    nt = (((1,), (1,)), ((), ()))
    lane_tk = lax.broadcasted_iota(I32, (tq, tk), 1)

    def fold(x, op):
        part = x[:, :LANES]
        for c in range(1, ncol):
            part = op(part, x[:, c * LANES:(c + 1) * LANES])
        return part

    def score_blk(kb, carry):
        mn, mx = carry
        off = pl.multiple_of(kb * tk, tk)
        kib = ki_ref[pl.ds(off, tk), :]
        sc = jnp.zeros((tq, tk), F32)
        for h in range(IDX_HEADS):
            lg = lax.dot_general(qih[h], kib, nt, preferred_element_type=F32)
            sc = sc + jnp.maximum(lg, 0.0) * wih[h]
        allowed = lane_tk + off < lim
        sc_scr[kb] = jnp.where(allowed, sc, NEG)
        mn = jnp.minimum(mn, fold(jnp.where(allowed, sc, -NEG), jnp.minimum))
        mx = jnp.maximum(mx, fold(jnp.where(allowed, sc, NEG), jnp.maximum))
        return mn, mx

    mn, mx = lax.fori_loop(0, nb, score_blk,
                           (jnp.full((tq, LANES), -NEG, F32), jnp.full((tq, LANES), NEG, F32)))
    lo0 = jnp.min(mn, axis=1, keepdims=True)
    hi0 = jnp.max(mx, axis=1, keepdims=True)
    hi0 = hi0 + (jnp.abs(hi0) + 1e-30) * 1e-6

    def count_ge(x):
        def body(kb, acc):
            return acc + fold(jnp.where(sc_scr[kb] >= x, 1.0, 0.0), jnp.add)

        acc = lax.fori_loop(0, nb, body, jnp.zeros((tq, LANES), F32))
        return jnp.sum(acc, axis=1, keepdims=True)

    def any_over(cnt_lo):
        return jnp.max(jnp.where(cnt_lo > kf, 1.0, 0.0)) > 0.5

    def bis_cond(c):
        return c[4]

    def bis_body(c):
        it, lo, hi, cnt_lo, _ = c
        mid = 0.5 * (lo + hi)
        cnt = count_ge(mid)
        up = cnt >= kf
        lo = jnp.where(up, mid, lo)
        hi = jnp.where(up, hi, mid)
        cnt_lo = jnp.where(up, cnt, cnt_lo)
        return it + 1, lo, hi, cnt_lo, jnp.logical_and(any_over(cnt_lo), it + 1 < BISECT_MAX_ITERS)

    cnt0 = count_ge(lo0)
    _, lo, hi, cnt_lo, _ = lax.while_loop(bis_cond, bis_body, (jnp.int32(0), lo0, hi0, cnt0, any_over(cnt0)))
    ties = any_over(cnt_lo)

    @pl.when(jnp.logical_not(ties))
    def _():
        def body(kb, carry):
            s = sc_scr[kb]
            bias_scr[kb] = jnp.where((s >= lo) & (s > 0.5 * NEG), 0.0, NEG)
            return carry

        lax.fori_loop(0, nb, body, 0)

    @pl.when(ties)
    def _():
        need = kf - count_ge(hi)
        tri = (lax.broadcasted_iota(I32, (LANES, LANES), 0)
               <= lax.broadcasted_iota(I32, (LANES, LANES), 1)).astype(BF16)

        def body(kb, seen):
            s = sc_scr[kb]
            cols = []
            for c in range(ncol):
                sc = s[:, c * LANES:(c + 1) * LANES]
                tie = (sc >= lo) & (sc < hi)
                tief = jnp.where(tie, 1.0, 0.0)
                incl = jnp.dot(tief.astype(BF16), tri, preferred_element_type=F32)
                take = tie & ((seen + incl - tief) < need)
                sel = ((sc >= hi) | take) & (sc > 0.5 * NEG)
                cols.append(jnp.where(sel, 0.0, NEG))
                seen = seen + incl[:, LANES - 1:]
            bias_scr[kb] = jnp.concatenate(cols, axis=1)
            return seen

        lax.fori_loop(0, nb, body, jnp.zeros((tq, 1), F32))

    q = q_ref[...].astype(F32)
    gq = N_HEADS // N_KV_HEADS
    rb = min(tq, ATTN_ROW_BLOCK)
    for g in range(N_KV_HEADS):
        for r in range(gq):
            h = g * gq + r
            qg_scr[g, r * tq:(r + 1) * tq, :] = q[:, h * HEAD_DIM:(h + 1) * HEAD_DIM].astype(BF16)
    m_scr[...] = jnp.full(m_scr.shape, NEG, F32)
    acc_scr[...] = jnp.zeros(acc_scr.shape, F32)

    def attn_blk(kb, carry):
        off = pl.multiple_of(kb * tk, tk)
        for g in range(N_KV_HEADS):
            kblk = k_ref[g, pl.ds(off, tk), :]
            vblk = v_ref[g, pl.ds(off, tk), :]
            s_scr[g] = lax.dot_general(qg_scr[g], kblk, nt, preferred_element_type=F32)
            for r0 in range(0, gq * tq, rb):
                rows = slice(r0, r0 + rb)
                s = s_scr[g, rows, :] + bias_scr[kb, r0 % tq:r0 % tq + rb, :]
                m_old = m_scr[g, rows, :]
                m_new = jnp.maximum(m_old, jnp.max(s, axis=1, keepdims=True))
                p_scr[g, rows, :] = jnp.exp2(s - jnp.concatenate([m_new] * ncol, axis=1)).astype(BF16)
                acc_scr[g, rows, :] = jnp.exp2(m_old - m_new) * acc_scr[g, rows, :]
                m_scr[g, rows, :] = m_new
            acc_scr[g] += jnp.dot(p_scr[g], vblk, preferred_element_type=F32)
        return carry

    lax.fori_loop(0, nb, attn_blk, 0)
    outs = []
    for g in range(N_KV_HEADS):
        for r in range(gq):
            acc = acc_scr[g, r * tq:(r + 1) * tq, :]
            outs.append(acc[:, :HEAD_DIM] / acc[:, HEAD_DIM:HEAD_DIM + 1])
    o = jnp.concatenate(outs, axis=1).astype(BF16)
    o_ref[...] = x_ref[...] + jnp.dot(o, wo_ref[...], preferred_element_type=F32)


def _dsa_attend(x2d, q, qi, wi, limit, k_hm, v_hm, ki, w_o, nkeys, *, batch, tq, tk, topk):
    rows = x2d.shape[0]
    per_batch = rows // batch
    nqt = per_batch // tq
    s_len = ki.shape[0] // batch
    nblk = s_len // tk
    gq = N_HEADS // N_KV_HEADS
    row_spec = lambda width: pl.BlockSpec((tq, width), lambda b, i, nk: (b * nqt + i, 0))
    k_spec = pl.BlockSpec((N_KV_HEADS, s_len, HEAD_DIM), lambda b, i, nk: (0, b, 0))
    v_spec = pl.BlockSpec((N_KV_HEADS, s_len, LANES), lambda b, i, nk: (0, b, 0))
    grid_spec = pltpu.PrefetchScalarGridSpec(
        num_scalar_prefetch=1,
        grid=(batch, nqt),
        in_specs=[row_spec(ATTN_WIDTH), row_spec(IDX_HEADS * IDX_DIM), row_spec(IDX_HEADS), row_spec(1),
                  row_spec(D_MODEL), k_spec, v_spec,
                  pl.BlockSpec((s_len, IDX_DIM), lambda b, i, nk: (b, 0)),
                  pl.BlockSpec((ATTN_WIDTH, D_MODEL), lambda b, i, nk: (0, 0))],
        out_specs=row_spec(D_MODEL),
        scratch_shapes=[pltpu.VMEM((nblk, tq, tk), F32), pltpu.VMEM((nblk, tq, tk), F32),
                        pltpu.VMEM((N_KV_HEADS, gq * tq, HEAD_DIM), BF16),
                        pltpu.VMEM((N_KV_HEADS, gq * tq, tk), F32), pltpu.VMEM((N_KV_HEADS, gq * tq, tk), BF16),
                        pltpu.VMEM((N_KV_HEADS, gq * tq, LANES), F32), pltpu.VMEM((N_KV_HEADS, gq * tq, LANES), F32)],
    )
    return pl.pallas_call(
        functools.partial(_attn_body, tq=tq, tk=tk, topk=topk),
        out_shape=jax.ShapeDtypeStruct(x2d.shape, F32),
        grid_spec=grid_spec,
        compiler_params=_cparams(("arbitrary", "arbitrary")),
        name="dsa_attend",
    )(nkeys, q, qi, wi, limit, x2d, k_hm, v_hm, ki, w_o.astype(BF16))


def _round_up(x, m):
    return (x + m - 1) // m * m


def kernel(x_prompt, x_sample, state_ssm_re, state_ssm_im, cache_k, cache_v, cache_kidx, cache_conv,
           norm_mix, norm_ffn, ssm_lambda_re, ssm_lambda_im, ssm_log_dt, ssm_b_re, ssm_b_im,
           ssm_c_re, ssm_c_im, ssm_d, ssm_w_glu, ssm_b_glu, attn_w_in, attn_q_norm, attn_k_norm,
           attn_w_o, ffn_w_up, ffn_conv_w, ffn_conv_b, ffn_w_down):
    b_p, t_p, _ = x_prompt.shape
    b_s, t_s, _ = x_sample.shape
    past = cache_k.shape[2]
    xp = x_prompt.astype(F32).reshape(b_p * t_p, D_MODEL)
    xs = x_sample.astype(F32).reshape(b_s * t_s, D_MODEL)

    ops = _s5_operators(ssm_lambda_re[0], ssm_lambda_im[0], ssm_log_dt[0], ssm_b_re[0], ssm_b_im[0],
                        ssm_c_re[0], ssm_c_im[0], ssm_d[0])
    zero_state = jnp.zeros((S5_OCT, b_p, 2 * S5_HALF), F32)
    zp, htp = _s5_scan(_norm(xp, norm_mix[0]), ops, zero_state, batch=b_p, seq=t_p)
    zs, hts = _s5_scan(_norm(xs, norm_mix[0]), ops, _state_to_oct(state_ssm_re[0], state_ssm_im[0]),
                       batch=b_s, seq=t_s)
    xp = _glu_residual(zp, xp, ssm_w_glu[0], ssm_b_glu[0])
    xs = _glu_residual(zs, xs, ssm_w_glu[0], ssm_b_glu[0])
    re_p, im_p = _oct_to_state(htp, b_p)
    re_s, im_s = _oct_to_state(hts, b_s)

    ffn0 = _ffn_weights(ffn_w_up[0], ffn_conv_w[0], ffn_conv_b[0], ffn_w_down[0])
    xp, conv_p0 = _conv_ffn(xp, jnp.zeros((b_p, CONV_W - 1, D_FF), F32), norm_ffn[0], ffn0, batch=b_p, seq=t_p)
    xs, conv_s0 = _conv_ffn(xs, cache_conv[0], norm_ffn[0], ffn0, batch=b_s, seq=t_s)

    pos_p = jnp.arange(t_p, dtype=I32)
    pos_s = past + jnp.arange(t_s, dtype=I32)
    topk_p = min(TOPK_MAX, t_p // 4)
    topk_s = min(TOPK_MAX, (past + t_s) // 4)

    (q_p, khm_p, vhm_p, kf_p, vf_p, qi_p, kib_p, kif_p, wi_p) = _dsa_project(
        xp, norm_mix[1], attn_w_in[0], attn_q_norm[0], attn_k_norm[0], pos_p)
    tq_p = min(t_p, 128)
    tk_p = min(t_p, 512)
    lim_p = jnp.tile((pos_p // CHUNK + 1) * CHUNK, b_p).reshape(b_p * t_p, 1)
    q_end = (jnp.arange(t_p // tq_p, dtype=I32) + 1) * tq_p
    nk_p = jnp.minimum((q_end + tk_p - 1) // tk_p, t_p // tk_p).astype(I32)
    xp = _dsa_attend(xp, q_p, qi_p, wi_p, lim_p, khm_p, vhm_p, kib_p, attn_w_o[0], nk_p,
                     batch=b_p, tq=tq_p, tk=tk_p, topk=topk_p)

    (q_s, khm_s, vhm_s, kf_s, vf_s, qi_s, kib_s, kif_s, wi_s) = _dsa_project(
        xs, norm_mix[1], attn_w_in[0], attn_q_norm[0], attn_k_norm[0], pos_s)
    tk_s = 3 * LANES
    s_all = past + t_s
    s_pad = _round_up(s_all, tk_s)

    def with_cache(cache, new_hm):
        heads = new_hm.shape[0]
        c = cache.astype(BF16).transpose(2, 0, 1, 3)
        n = new_hm.reshape(heads, b_s, t_s, new_hm.shape[-1])
        z = jnp.zeros((heads, b_s, s_pad - s_all, new_hm.shape[-1]), BF16)
        return jnp.concatenate([c, n, z], axis=2).reshape(heads, b_s * s_pad, new_hm.shape[-1])

    ones_col = jnp.zeros(cache_v[0].shape[:-1] + (LANES - HEAD_DIM,), cache_v.dtype).at[..., 0].set(1.0)
    k_all = with_cache(cache_k[0], khm_s)
    v_all = with_cache(jnp.concatenate([cache_v[0], ones_col], axis=-1), vhm_s)
    ki_all = with_cache(cache_kidx[0][:, :, None, :], kib_s[None])[0]
    lim_s = jnp.full((b_s * t_s, 1), s_all, I32)
    nk_s = jnp.full((1,), s_pad // tk_s, I32)
    xs = _dsa_attend(xs, q_s, qi_s, wi_s, lim_s, k_all, v_all, ki_all, attn_w_o[0], nk_s,
                     batch=b_s, tq=t_s, tk=tk_s, topk=topk_s)

    ffn1 = _ffn_weights(ffn_w_up[1], ffn_conv_w[1], ffn_conv_b[1], ffn_w_down[1])
    xp, conv_p1 = _conv_ffn(xp, jnp.zeros((b_p, CONV_W - 1, D_FF), F32), norm_ffn[1], ffn1, batch=b_p, seq=t_p)
    xs, conv_s1 = _conv_ffn(xs, cache_conv[1], norm_ffn[1], ffn1, batch=b_s, seq=t_s)

    dt = x_prompt.dtype
    kv_shape_p = (1, b_p, t_p, N_KV_HEADS, HEAD_DIM)
    kv_shape_s = (1, b_s, t_s, N_KV_HEADS, HEAD_DIM)
    return (xp.reshape(x_prompt.shape).astype(dt), xs.reshape(x_sample.shape).astype(x_sample.dtype),
            re_p[None].astype(dt), im_p[None].astype(dt),
            re_s[None].astype(state_ssm_re.dtype), im_s[None].astype(state_ssm_im.dtype),
            kf_p.reshape(kv_shape_p).astype(dt), vf_p.reshape(kv_shape_p).astype(dt),
            kif_p.reshape(1, b_p, t_p, IDX_DIM).astype(dt),
            kf_s.reshape(kv_shape_s).astype(dt), vf_s.reshape(kv_shape_s).astype(dt),
            kif_s.reshape(1, b_s, t_s, IDX_DIM).astype(dt),
            jnp.stack([conv_p0, conv_p1]).astype(dt), jnp.stack([conv_s0, conv_s1]).astype(dt))
```

```python
import functools
import math

import jax
import jax.numpy as jnp
from jax import lax
from jax.experimental import pallas as pl
from jax.experimental.pallas import tpu as pltpu

F32 = jnp.float32
BF16 = jnp.bfloat16
I32 = jnp.int32

D_MODEL = 1024
CHUNK = 64
SSM_GROUP = 16
SSM_GROUPS = D_MODEL // SSM_GROUP
SSM_STATE = 64
N_HEADS = 16
N_KV_HEADS = 4
HEAD_DIM = 64
ATTN_WIDTH = N_HEADS * HEAD_DIM
KV_WIDTH = N_KV_HEADS * HEAD_DIM
IDX_HEADS = 8
IDX_DIM = 64
TOPK_MAX = 256
ROT_DIM = HEAD_DIM // 4
ROPE_THETA = 500000.0
OFF_K = ATTN_WIDTH
OFF_V = OFF_K + KV_WIDTH
OFF_QI = OFF_V + KV_WIDTH
OFF_KI = OFF_QI + IDX_HEADS * IDX_DIM
OFF_WI = OFF_KI + IDX_DIM
IN_COLS = OFF_WI + IDX_HEADS
D_FF = 2816
CONV_W = 3
EPS = 1e-6
NEG = -1e30

LANES = 128
SUBLANES = 8
VMEM_LIMIT_BYTES = 56 * 1024 * 1024

S5_L = 16
S5_OCT = D_MODEL // LANES
S5_OCT_GROUPS = LANES // SSM_GROUP
S5_HALF = S5_OCT_GROUPS * SSM_STATE
S5_PW_ROWS = 24

FFN_FC = 256
FFN_NC = D_FF // FFN_FC
assert FFN_NC % 2 == 1
IN_COLS_PAD = 2176
LOG2E = math.log2(math.e)
BISECT_MAX_ITERS = 40
ATTN_ROW_BLOCK = 64
TINY = 1.1754944e-38


def _cparams(sem):
    return pltpu.CompilerParams(dimension_semantics=sem, vmem_limit_bytes=VMEM_LIMIT_BYTES)


def _rms(x, g):
    ms = jnp.mean(x * x, axis=-1, keepdims=True)
    return x * lax.rsqrt(ms + EPS) * g


def _norm_body(x_ref, g_ref, o_ref):
    o_ref[...] = _rms(x_ref[...], g_ref[...])


def _norm(x2d, g):
    rows = x2d.shape[0]
    tm = min(rows, 1024)
    return pl.pallas_call(
        _norm_body,
        out_shape=jax.ShapeDtypeStruct((rows, D_MODEL), F32),
        grid=(rows // tm,),
        in_specs=[pl.BlockSpec((tm, D_MODEL), lambda i: (i, 0)),
                  pl.BlockSpec((1, D_MODEL), lambda i: (0, 0))],
        out_specs=pl.BlockSpec((tm, D_MODEL), lambda i: (i, 0)),
        compiler_params=_cparams(("arbitrary",)),
        name="rmsnorm",
    )(x2d, g.astype(F32).reshape(1, D_MODEL))


def _s5_operators(lam_re, lam_im, log_dt, b_re, b_im, c_re, c_im, d_skip):
    L = S5_L
    lre = jnp.minimum(lam_re.astype(F32), -1e-4)
    lim = lam_im.astype(F32)
    dt = jnp.exp(log_dt.astype(F32))[:, None]
    n = jnp.arange(L + 1, dtype=F32)[:, None, None]
    mag = jnp.exp(n * (lre * dt)[None])
    ang = n * (lim * dt)[None]
    pr = mag * jnp.cos(ang)
    pi = mag * jnp.sin(ang)
    a_re, a_im = pr[1], pi[1]
    den = lre * lre + lim * lim
    n_re = a_re - 1.0
    f_re = (n_re * lre + a_im * lim) / den
    f_im = (a_im * lre - n_re * lim) / den
    br = b_re.astype(F32)
    bi = b_im.astype(F32)
    bb_re = f_re[..., None] * br - f_im[..., None] * bi
    bb_im = f_re[..., None] * bi + f_im[..., None] * br
    cr = c_re.astype(F32)
    ci = c_im.astype(F32)
    eye = jnp.eye(S5_OCT_GROUPS, dtype=F32)
    og = (S5_OCT, S5_OCT_GROUPS)

    tg = lambda m: jnp.moveaxis(m, 0, -1)
    crt, cit = tg(cr)[None, :, None], tg(ci)[None, :, None]
    prt, pit = tg(pr[:L].transpose(1, 0, 2))[:, None, None], tg(pi[:L].transpose(1, 0, 2))[:, None, None]
    brt, bit = tg(bb_re).transpose(1, 0, 2)[None, None], tg(bb_im).transpose(1, 0, 2)[None, None]
    kl = jnp.sum((crt * prt - cit * pit) * brt - (crt * pit + cit * prt) * bit, axis=3)
    kl = kl.transpose(0, 3, 1, 2)
    kbd = jnp.einsum("ljgdc,gh->ljgchd", kl.reshape((L,) + og + (SSM_GROUP, SSM_GROUP)), eye)
    kbd = kbd.reshape(L, S5_OCT, LANES, LANES)
    kpad = jnp.concatenate([jnp.zeros_like(kbd[:1]), kbd], axis=0)
    k2 = []
    for dlt in range(L // 2):
        top = jnp.concatenate([kpad[2 * dlt + 1], kpad[2 * dlt + 2]], axis=-1)
        bot = jnp.concatenate([kpad[2 * dlt], kpad[2 * dlt + 1]], axis=-1)
        k2.append(jnp.concatenate([top, bot], axis=-2))
    k2 = jnp.stack(k2, axis=1).astype(BF16)

    def bd_in(m):
        m = m.reshape(og + (SSM_STATE, SSM_GROUP))
        return jnp.einsum("jgpc,gh->jgchp", m, eye).reshape(S5_OCT, LANES, S5_HALF)

    ab_re = a_re[..., None] * bb_re - a_im[..., None] * bb_im
    ab_im = a_re[..., None] * bb_im + a_im[..., None] * bb_re
    f2 = jnp.concatenate([
        jnp.concatenate([bd_in(ab_re), bd_in(ab_im)], axis=-1),
        jnp.concatenate([bd_in(bb_re), bd_in(bb_im)], axis=-1)], axis=-2).astype(BF16)

    def bd_out(m):
        m = m.reshape(og + (SSM_GROUP, SSM_STATE))
        return jnp.einsum("jgcp,gh->jgphc", m, eye).reshape(S5_OCT, S5_HALF, LANES)

    c1r = cr * a_re[:, None, :] - ci * a_im[:, None, :]
    c1i = cr * a_im[:, None, :] + ci * a_re[:, None, :]
    c2 = jnp.concatenate([
        jnp.concatenate([bd_out(cr), bd_out(c1r)], axis=-1),
        jnp.concatenate([-bd_out(ci), -bd_out(c1i)], axis=-1)], axis=-2).astype(BF16)

    def oct_lanes(p):
        return p.reshape((L + 1,) + og + (SSM_STATE,)).transpose(1, 0, 2, 3).reshape(S5_OCT, L + 1, S5_HALF)

    pw = jnp.concatenate([oct_lanes(pr), oct_lanes(pi)], axis=-1)
    pw = jnp.pad(pw, ((0, 0), (0, S5_PW_ROWS - (L + 1)), (0, 0)))
    dsk = d_skip.astype(F32).reshape(S5_OCT, 1, LANES)
    dsk2 = jnp.concatenate([dsk, dsk], axis=-1)
    return f2, k2, c2, pw, dsk2


def _s5_body(u_ref, f2_ref, k2_ref, c2_ref, pw_ref, dsk_ref, h0_ref, z_ref, ht_ref, v_scr, hp_scr, *, bpt, nck):
    L = S5_L
    H = S5_HALF
    tr = bpt * nck

    def cmul(xr, xi, n):
        pr = pw_ref[0, n:n + 1, :H]
        pi = pw_ref[0, n:n + 1, H:]
        return xr * pr - xi * pi, xr * pi + xi * pr

    ufp = [jnp.concatenate([u_ref[pl.ds(2 * s, tr, stride=L), :], u_ref[pl.ds(2 * s + 1, tr, stride=L), :]], axis=1)
           for s in range(L // 2)]
    upair = [u.astype(BF16) for u in ufp]

    vr = vi = None
    for s2 in range(L // 2):
        p = jnp.dot(upair[s2], f2_ref[0], preferred_element_type=F32)
        qr, qi = cmul(p[:, :H], p[:, H:], L - 2 - 2 * s2)
        vr = qr if vr is None else vr + qr
        vi = qi if vi is None else vi + qi
    v_scr[:, :H] = vr
    v_scr[:, H:] = vi

    alr = pw_ref[0, L:L + 1, :H]
    ali = pw_ref[0, L:L + 1, H:]
    for b in range(bpt):
        def step(k, carry, b=b):
            hr, hi = carry
            row = b * nck + k
            hp_scr[pl.ds(row, 1), :H] = hr
            hp_scr[pl.ds(row, 1), H:] = hi
            wr = v_scr[pl.ds(row, 1), :H]
            wi = v_scr[pl.ds(row, 1), H:]
            return hr * alr - hi * ali + wr, hr * ali + hi * alr + wi

        h0r = h0_ref[0, 0, b:b + 1, :H]
        h0i = h0_ref[0, 0, b:b + 1, H:]
        hr, hi = lax.fori_loop(0, nck, step, (h0r, h0i))
        ht_ref[0, 0, b:b + 1, :H] = hr
        ht_ref[0, 0, b:b + 1, H:] = hi

    hpr = hp_scr[:, :H]
    hpi = hp_scr[:, H:]
    dsk = dsk_ref[0]
    for t2 in range(L // 2):
        gr, gi = cmul(hpr, hpi, 2 * t2 + 1)
        g = jnp.concatenate([gr, gi], axis=1).astype(BF16)
        y = jnp.dot(g, c2_ref[0], preferred_element_type=F32)
        for s2 in range(t2 + 1):
            y = y + jnp.dot(upair[s2], k2_ref[0, t2 - s2], preferred_element_type=F32)
        z = jax.nn.gelu(y + dsk * ufp[t2])
        z_ref[pl.ds(2 * t2, tr, stride=L), :] = z[:, :LANES]
        z_ref[pl.ds(2 * t2 + 1, tr, stride=L), :] = z[:, LANES:]


def _s5_scan(u2d, ops, h0, *, batch, seq):
    L = S5_L
    f2, k2, c2, pw, dsk2 = ops
    nck = seq // L
    n_rows = batch * nck
    bpt = 1 if nck % SUBLANES == 0 else batch
    tr = bpt * nck
    n_tiles = n_rows // tr
    h0 = h0.reshape(S5_OCT, n_tiles, bpt, 2 * S5_HALF)
    tok_spec = pl.BlockSpec((tr * L, LANES), lambda j, i: (i, j))
    in_specs = [
        tok_spec,
        pl.BlockSpec((1, 2 * LANES, 2 * S5_HALF), lambda j, i: (j, 0, 0)),
        pl.BlockSpec((1, L // 2, 2 * LANES, 2 * LANES), lambda j, i: (j, 0, 0, 0)),
        pl.BlockSpec((1, 2 * S5_HALF, 2 * LANES), lambda j, i: (j, 0, 0)),
        pl.BlockSpec((1, S5_PW_ROWS, 2 * S5_HALF), lambda j, i: (j, 0, 0)),
        pl.BlockSpec((1, 1, 2 * LANES), lambda j, i: (j, 0, 0)),
        pl.BlockSpec((1, 1, bpt, 2 * S5_HALF), lambda j, i: (j, i, 0, 0)),
    ]
    z, ht = pl.pallas_call(
        functools.partial(_s5_body, bpt=bpt, nck=nck),
        out_shape=(jax.ShapeDtypeStruct(u2d.shape, F32),
                   jax.ShapeDtypeStruct((S5_OCT, n_tiles, bpt, 2 * S5_HALF), F32)),
        grid=(S5_OCT, n_tiles),
        in_specs=in_specs,
        out_specs=(tok_spec, pl.BlockSpec((1, 1, bpt, 2 * S5_HALF), lambda j, i: (j, i, 0, 0))),
        scratch_shapes=[pltpu.VMEM((tr, 2 * S5_HALF), F32), pltpu.VMEM((tr, 2 * S5_HALF), F32)],
        compiler_params=_cparams(("arbitrary", "arbitrary")),
        name="s5_scan",
    )(u2d, f2, k2, c2, pw, dsk2, h0)
    return z, ht


def _state_to_oct(s_re, s_im):
    b = s_re.shape[0]
    r = s_re.astype(F32).reshape(b, S5_OCT, S5_HALF).transpose(1, 0, 2)
    i = s_im.astype(F32).reshape(b, S5_OCT, S5_HALF).transpose(1, 0, 2)
    return jnp.concatenate([r, i], axis=-1)


def _oct_to_state(ht, batch):
    ht = ht.reshape(S5_OCT, batch, 2 * S5_HALF)
    r = ht[..., :S5_HALF].transpose(1, 0, 2).reshape(batch, SSM_GROUPS, SSM_STATE)
    i = ht[..., S5_HALF:].transpose(1, 0, 2).reshape(batch, SSM_GROUPS, SSM_STATE)
    return r, i


def _glu_body(z_ref, x_ref, w_ref, b_ref, o_ref):
    g = jnp.dot(z_ref[...].astype(BF16), w_ref[...], preferred_element_type=F32) + b_ref[...]
    o_ref[...] = x_ref[...] + g[:, :D_MODEL] * jax.nn.sigmoid(g[:, D_MODEL:])


def _glu_residual(z, x2d, w_glu, b_glu):
    rows = x2d.shape[0]
    tm = min(rows, 512)
    row_spec = pl.BlockSpec((tm, D_MODEL), lambda i: (i, 0))
    return pl.pallas_call(
        _glu_body,
        out_shape=jax.ShapeDtypeStruct(x2d.shape, F32),
        grid=(rows // tm,),
        in_specs=[row_spec, row_spec,
                  pl.BlockSpec((D_MODEL, 2 * D_MODEL), lambda i: (0, 0)),
                  pl.BlockSpec((1, 2 * D_MODEL), lambda i: (0, 0))],
        out_specs=row_spec,
        compiler_params=_cparams(("arbitrary",)),
        name="glu_residual",
    )(z, x2d, w_glu.astype(BF16), b_glu.astype(F32).reshape(1, 2 * D_MODEL))


def _ffn_body(x_ref, xh_ref, cs_ref, g_ref, wa_ref, wb_ref, cw_ref, cb_ref, wd_ref,
              o_ref, co_ref, xn_scr, ha0, hb0, hh0, ha1, hb1, hh1, gt0, gt1, acc_scr, *, tm, tiles_per_seq):
    i = pl.program_id(0)
    first = (i % tiles_per_seq) == 0
    g = g_ref[...]
    xn_scr[...] = _rms(x_ref[...], g).astype(BF16)
    xnh = _rms(xh_ref[...], g).astype(BF16)
    rows = lax.broadcasted_iota(I32, (tm, FFN_FC), 0)
    sets = ((ha0, hb0, hh0, gt0), (ha1, hb1, hh1, gt1))

    def up(c, s):
        ha, hb, hh, _ = sets[s]
        xn = xn_scr[...]
        ha[...] = jnp.dot(xn, wa_ref[c], preferred_element_type=F32)
        hb[...] = jnp.dot(xn, wb_ref[c], preferred_element_type=F32)
        hh[...] = jnp.dot(xnh, wa_ref[c], preferred_element_type=F32)

    def act(c, s):
        ha_ref, hb_ref, hh_ref, gt = sets[s]
        ha = ha_ref[...]
        hh = jnp.where(first, cs_ref[0, c], hh_ref[...])
        co_ref[0, c] = ha[tm - SUBLANES:, :]
        h6 = hh[6:7, :]
        h7 = hh[7:8, :]
        prev1 = jnp.where(rows == 0, h7, pltpu.roll(ha, 1, 0))
        prev2 = jnp.where(rows == 0, h6, jnp.where(rows == 1, h7, pltpu.roll(ha, 2, 0)))
        cw = cw_ref[c]
        cv = cb_ref[c] + cw[0:1, :] * prev2 + cw[1:2, :] * prev1 + cw[2:3, :] * ha
        gt[...] = (jax.nn.gelu(cv) * hb_ref[...]).astype(BF16)

    def down(c, s):
        acc_scr[...] += jnp.dot(sets[s][3][...], wd_ref[c], preferred_element_type=F32)

    acc_scr[...] = jnp.zeros(acc_scr.shape, F32)
    gt1[...] = jnp.zeros(gt1.shape, BF16)
    up(0, 0)

    def body(k, carry):
        c = 2 * k
        up(c + 1, 1)
        act(c, 0)
        down(jnp.maximum(c - 1, 0), 1)
        up(c + 2, 0)
        act(c + 1, 1)
        down(c, 0)
        return carry

    lax.fori_loop(0, (FFN_NC - 1) // 2, body, 0)
    act(FFN_NC - 1, 0)
    down(FFN_NC - 2, 1)
    down(FFN_NC - 1, 0)
    o_ref[...] = x_ref[...] + acc_scr[...]


def _ffn_weights(w_up, conv_w, conv_b, w_down):
    wa = w_up[:, :D_FF].astype(BF16).reshape(D_MODEL, FFN_NC, FFN_FC).transpose(1, 0, 2)
    wb = w_up[:, D_FF:].astype(BF16).reshape(D_MODEL, FFN_NC, FFN_FC).transpose(1, 0, 2)
    cw = jnp.pad(conv_w.astype(F32), ((0, SUBLANES - CONV_W), (0, 0)))
    cw = cw.reshape(SUBLANES, FFN_NC, FFN_FC).transpose(1, 0, 2)
    cb = conv_b.astype(F32).reshape(FFN_NC, 1, FFN_FC)
    wd = w_down.astype(BF16).reshape(FFN_NC, FFN_FC, D_MODEL)
    return wa, wb, cw, cb, wd


def _conv_ffn(x2d, conv_state, g, weights, *, batch, seq):
    wa, wb, cw, cb, wd = weights
    tm = min(seq, 512)
    tiles_per_seq = seq // tm
    hb = tm // SUBLANES
    cs = jnp.pad(conv_state.astype(F32), ((0, 0), (SUBLANES - (CONV_W - 1), 0), (0, 0)))
    cs = cs.reshape(batch, SUBLANES, FFN_NC, FFN_FC).transpose(0, 2, 1, 3)
    const3 = lambda i: (0, 0, 0)
    up_set = [pltpu.VMEM((tm, FFN_FC), F32), pltpu.VMEM((tm, FFN_FC), F32), pltpu.VMEM((SUBLANES, FFN_FC), F32)]
    gate = pltpu.VMEM((tm, FFN_FC), BF16)
    scratch = [pltpu.VMEM((tm, D_MODEL), BF16)] + up_set + up_set + [gate, gate, pltpu.VMEM((tm, D_MODEL), F32)]
    out, co = pl.pallas_call(
        functools.partial(_ffn_body, tm=tm, tiles_per_seq=tiles_per_seq),
        out_shape=(jax.ShapeDtypeStruct(x2d.shape, F32),
                   jax.ShapeDtypeStruct((batch, FFN_NC, SUBLANES, FFN_FC), F32)),
        grid=(batch * tiles_per_seq,),
        in_specs=[pl.BlockSpec((tm, D_MODEL), lambda i: (i, 0)),
                  pl.BlockSpec((SUBLANES, D_MODEL), lambda i: (jnp.maximum(i * hb - 1, 0), 0)),
                  pl.BlockSpec((1, FFN_NC, SUBLANES, FFN_FC), lambda i: (i // tiles_per_seq, 0, 0, 0)),
                  pl.BlockSpec((1, D_MODEL), lambda i: (0, 0)),
                  pl.BlockSpec((FFN_NC, D_MODEL, FFN_FC), const3),
                  pl.BlockSpec((FFN_NC, D_MODEL, FFN_FC), const3),
                  pl.BlockSpec((FFN_NC, SUBLANES, FFN_FC), const3),
                  pl.BlockSpec((FFN_NC, 1, FFN_FC), const3),
                  pl.BlockSpec((FFN_NC, FFN_FC, D_MODEL), const3)],
        out_specs=(pl.BlockSpec((tm, D_MODEL), lambda i: (i, 0)),
                   pl.BlockSpec((1, FFN_NC, SUBLANES, FFN_FC), lambda i: (i // tiles_per_seq, 0, 0, 0))),
        scratch_shapes=scratch,
        compiler_params=_cparams(("arbitrary",)),
        name="conv_ffn",
    )(x2d, x2d, cs, g.astype(F32).reshape(1, D_MODEL), wa, wb, cw, cb, wd)
    new_state = co[:, :, SUBLANES - (CONV_W - 1):, :].transpose(0, 2, 1, 3).reshape(batch, CONV_W - 1, D_FF)
    return out, new_state


def _rope_tables(pos):
    half = ROT_DIM // 2
    inv = ROPE_THETA ** (-jnp.arange(half, dtype=F32) / half)
    ang = pos.astype(F32)[:, None] * inv[None, :]
    cos = jnp.cos(ang)
    sin = jnp.sin(ang)
    lane = jnp.arange(LANES) % HEAD_DIM
    idx = lane % half
    cc = jnp.where(lane[None] < ROT_DIM, cos[:, idx], 1.0)
    s1 = jnp.where((lane[None] >= half) & (lane[None] < ROT_DIM), sin[:, idx], 0.0)
    s2 = jnp.where(lane[None] < half, -sin[:, idx], 0.0)
    return cc.astype(F32), s1.astype(F32), s2.astype(F32)


def _proj_body(x_ref, g_ref, w_ref, qn_ref, kn_ref, cc_ref, s1_ref, s2_ref, bd_ref,
               q_ref, khm_ref, vhm_ref, kf_ref, vf_ref, qi_ref, kib_ref, kif_ref, wi_ref):
    xn = _rms(x_ref[...], g_ref[...]).astype(BF16)
    proj = jnp.dot(xn, w_ref[...], preferred_element_type=F32)
    cc = cc_ref[...]
    s1 = s1_ref[...]
    s2 = s2_ref[...]
    bd = bd_ref[...]

    def rope(y):
        return y * cc + pltpu.roll(y, ROT_DIM // 2, 1) * s1 + pltpu.roll(y, LANES - ROT_DIM // 2, 1) * s2

    def head_norm(y, gain):
        sq = y * y
        hi = sq.astype(BF16)
        lo = (sq - hi.astype(F32)).astype(BF16)
        ms = (jnp.dot(hi, bd, preferred_element_type=F32) + jnp.dot(lo, bd, preferred_element_type=F32))
        return y * lax.rsqrt(ms + EPS) * gain

    qn = qn_ref[...]
    kn = kn_ref[...]
    scale = HEAD_DIM ** -0.5 * LOG2E
    one_col = jnp.where(lax.broadcasted_iota(I32, (x_ref.shape[0], HEAD_DIM), 1) == 0, 1.0, 0.0)
    for c in range(ATTN_WIDTH // LANES):
        y = proj[:, c * LANES:(c + 1) * LANES]
        q_ref[:, c * LANES:(c + 1) * LANES] = (rope(head_norm(y, qn)) * scale).astype(BF16)
    for c in range(KV_WIDTH // LANES):
        y = proj[:, OFF_K + c * LANES:OFF_K + (c + 1) * LANES]
        k = rope(head_norm(y, kn))
        kf_ref[:, c * LANES:(c + 1) * LANES] = k
        khm_ref[2 * c] = k[:, :HEAD_DIM].astype(BF16)
        khm_ref[2 * c + 1] = k[:, HEAD_DIM:].astype(BF16)
        v = proj[:, OFF_V + c * LANES:OFF_V + (c + 1) * LANES]
        vf_ref[:, c * LANES:(c + 1) * LANES] = v
        vhm_ref[2 * c] = jnp.concatenate([v[:, :HEAD_DIM], one_col], axis=1).astype(BF16)
        vhm_ref[2 * c + 1] = jnp.concatenate([v[:, HEAD_DIM:], one_col], axis=1).astype(BF16)
    iscale = IDX_DIM ** -0.5
    for c in range(IDX_HEADS * IDX_DIM // LANES):
        y = proj[:, OFF_QI + c * LANES:OFF_QI + (c + 1) * LANES]
        qi_ref[:, c * LANES:(c + 1) * LANES] = (rope(y) * iscale).astype(BF16)
    tail = proj[:, OFF_KI:OFF_KI + LANES]
    ki = rope(tail)[:, :IDX_DIM]
    kif_ref[...] = ki
    kib_ref[...] = ki.astype(BF16)
    wi_ref[...] = tail[:, IDX_DIM:IDX_DIM + IDX_HEADS] * (IDX_HEADS ** -0.5)


def _dsa_project(x2d, g, w_in, q_norm, k_norm, pos):
    rows = x2d.shape[0]
    tm = min(rows, 256)
    w = jnp.pad(w_in.astype(BF16), ((0, 0), (0, IN_COLS_PAD - IN_COLS)))
    if pos.shape[0] < tm:
        pos = jnp.tile(pos, tm // pos.shape[0])
    ntab = pos.shape[0] // tm
    cc, s1, s2 = _rope_tables(pos)
    lane = jnp.arange(LANES)
    bd = jnp.where((lane[:, None] // HEAD_DIM) == (lane[None, :] // HEAD_DIM), 1.0 / HEAD_DIM, 0.0).astype(BF16)
    qn = jnp.tile(q_norm.astype(F32), LANES // HEAD_DIM).reshape(1, LANES)
    kn = jnp.tile(k_norm.astype(F32), LANES // HEAD_DIM).reshape(1, LANES)
    row_spec = lambda width: pl.BlockSpec((tm, width), lambda i: (i, 0))
    const = lambda shape: pl.BlockSpec(shape, lambda i: (0,) * len(shape))
    hm_spec = pl.BlockSpec((N_KV_HEADS, tm, HEAD_DIM), lambda i: (0, i, 0))
    vhm_spec = pl.BlockSpec((N_KV_HEADS, tm, LANES), lambda i: (0, i, 0))
    tab_spec = pl.BlockSpec((tm, LANES), lambda i: (i % ntab, 0))
    return pl.pallas_call(
        _proj_body,
        out_shape=(jax.ShapeDtypeStruct((rows, ATTN_WIDTH), BF16),
                   jax.ShapeDtypeStruct((N_KV_HEADS, rows, HEAD_DIM), BF16),
                   jax.ShapeDtypeStruct((N_KV_HEADS, rows, LANES), BF16),
                   jax.ShapeDtypeStruct((rows, KV_WIDTH), F32),
                   jax.ShapeDtypeStruct((rows, KV_WIDTH), F32),
                   jax.ShapeDtypeStruct((rows, IDX_HEADS * IDX_DIM), BF16),
                   jax.ShapeDtypeStruct((rows, IDX_DIM), BF16),
                   jax.ShapeDtypeStruct((rows, IDX_DIM), F32),
                   jax.ShapeDtypeStruct((rows, IDX_HEADS), F32)),
        grid=(rows // tm,),
        in_specs=[row_spec(D_MODEL), const((1, D_MODEL)), const((D_MODEL, IN_COLS_PAD)),
                  const((1, LANES)), const((1, LANES)),
                  tab_spec, tab_spec, tab_spec, const((LANES, LANES))],
        out_specs=(row_spec(ATTN_WIDTH), hm_spec, vhm_spec, row_spec(KV_WIDTH), row_spec(KV_WIDTH),
                   row_spec(IDX_HEADS * IDX_DIM), row_spec(IDX_DIM), row_spec(IDX_DIM), row_spec(IDX_HEADS)),
        compiler_params=_cparams(("arbitrary",)),
        name="dsa_project",
    )(x2d, g.astype(F32).reshape(1, D_MODEL), w, qn, kn, cc, s1, s2, bd)


def _attn_body(nk_ref, q_ref, qi_ref, wi_ref, lim_ref, x_ref, k_ref, v_ref, ki_ref, wo_ref,
               o_ref, sc_scr, bias_scr, qg_scr, s_scr, p_scr, m_scr, acc_scr, *, tq, tk, topk):
    i = pl.program_id(1)
    nb = nk_ref[i]
    ncol = tk // LANES
    kf = float(topk)
    qi = qi_ref[...].astype(F32)
    wi = wi_ref[...]
    lim = lim_ref[...]
    qih = [qi[:, h * IDX_DIM:(h + 1) * IDX_DIM].astype(BF16) for h in range(IDX_HEADS)]
    wih = [wi[:, h:h + 1] for h in range(IDX_HEADS)]
    nt = (((1,), (1,)), ((), ()))
    lane_tk = lax.broadcasted_iota(I32, (tq, tk), 1)

    def fold(x, op):
        part = x[:, :LANES]
        for c in range(1, ncol):
            part = op(part, x[:, c * LANES:(c + 1) * LANES])
        return part

    def rows_all(x, op):
        return jnp.broadcast_to(op(x, axis=1, keepdims=True), (tq, LANES))

    def tile_cols(x):
        return jnp.concatenate([x] * ncol, axis=1)

    def score_blk(kb, carry):
        mn, mx, cpos, cnn = carry
        off = pl.multiple_of(kb * tk, tk)
        kib = ki_ref[pl.ds(off, tk), :]
        sc = jnp.zeros((tq, tk), F32)
        for h in range(IDX_HEADS):
            lg = lax.dot_general(qih[h], kib, nt, preferred_element_type=F32)
            sc = sc + jnp.maximum(lg, 0.0) * wih[h]
        allowed = lane_tk + off < lim
        scm = jnp.where(allowed, sc, NEG)
        sc_scr[kb] = scm
        mn = jnp.minimum(mn, fold(jnp.where(allowed, sc, -NEG), jnp.minimum))
        mx = jnp.maximum(mx, fold(scm, jnp.maximum))
        cpos = cpos + fold(jnp.where(scm > 0.0, 1.0, 0.0), jnp.add)
        cnn = cnn + fold(jnp.where(scm >= 0.0, 1.0, 0.0), jnp.add)
        return mn, mx, cpos, cnn

    zeros = jnp.zeros((tq, LANES), F32)
    mn, mx, cpos, cnn = lax.fori_loop(
        0, nb, score_blk, (jnp.full((tq, LANES), -NEG, F32), jnp.full((tq, LANES), NEG, F32), zeros, zeros))
    smin = rows_all(mn, jnp.min)
    smax = rows_all(mx, jnp.max)
    cpos = rows_all(cpos, jnp.sum)
    cnn = rows_all(cnn, jnp.sum)

    def count_ge(x):
        xt = tile_cols(x)

        def body(kb, acc):
            return acc + fold(jnp.where(sc_scr[kb] >= xt, 1.0, 0.0), jnp.add)

        return rows_all(lax.fori_loop(0, nb, body, zeros), jnp.sum)

    above_zero = cpos >= kf
    at_zero = jnp.logical_and(jnp.logical_not(above_zero), cnn >= kf)
    n_allowed = jnp.broadcast_to(jnp.minimum(lim, nb * tk).astype(F32), (tq, LANES))
    lo0 = jnp.where(above_zero, TINY, jnp.where(at_zero, 0.0, smin))
    hi0 = jnp.where(above_zero, smax + (jnp.abs(smax) + 1e-30) * 1e-6, jnp.where(at_zero, TINY, 0.0))
    cnt0 = jnp.where(above_zero, cpos, jnp.where(at_zero, cnn, n_allowed))
    frozen = at_zero

    def any_over(cnt_lo):
        return jnp.max(jnp.where(jnp.logical_and(cnt_lo > kf, jnp.logical_not(frozen)), 1.0, 0.0)) > 0.5

    def bis_step(lo, hi, cnt_lo):
        mid = 0.5 * (lo + hi)
        cnt = count_ge(mid)
        up = jnp.logical_and(cnt >= kf, jnp.logical_not(frozen))
        dn = jnp.logical_and(cnt < kf, jnp.logical_not(frozen))
        return jnp.where(up, mid, lo), jnp.where(dn, mid, hi), jnp.where(up, cnt, cnt_lo)

    def bis_cond(c):
        return c[4]

    def bis_body(c):
        it, lo, hi, cnt_lo, _ = c
        lo, hi, cnt_lo = bis_step(*bis_step(lo, hi, cnt_lo))
        return it + 2, lo, hi, cnt_lo, jnp.logical_and(any_over(cnt_lo), it + 2 < BISECT_MAX_ITERS)

    _, lo, hi, cnt_lo, _ = lax.while_loop(bis_cond, bis_body, (jnp.int32(0), lo0, hi0, cnt0, any_over(cnt0)))
    ties = jnp.max(jnp.where(cnt_lo > kf, 1.0, 0.0)) > 0.5
    lo_t = tile_cols(lo)
    hi_t = tile_cols(hi)

    @pl.when(jnp.logical_not(ties))
    def _():
        def body(kb, carry):
            s = sc_scr[kb]
            bias_scr[kb] = jnp.where((s >= lo_t) & (s > 0.5 * NEG), 0.0, NEG)
            return carry

        lax.fori_loop(0, nb, body, 0)

    @pl.when(ties)
    def _():
        need = kf - count_ge(hi)
        tri = (lax.broadcasted_iota(I32, (LANES, LANES), 0)
               <= lax.broadcasted_iota(I32, (LANES, LANES), 1)).astype(BF16)

        def body(kb, seen):
            s = sc_scr[kb]
            cols = []
            for c in range(ncol):
                sc = s[:, c * LANES:(c + 1) * LANES]
                tie = (sc >= lo) & (sc < hi)
                tief = jnp.where(tie, 1.0, 0.0)
                incl = jnp.dot(tief.astype(BF16), tri, preferred_element_type=F32)
                take = tie & ((seen + incl - tief) < need)
                sel = ((sc >= hi) | take) & (sc > 0.5 * NEG)
                cols.append(jnp.where(sel, 0.0, NEG))
                seen = seen + incl[:, LANES - 1:]
            bias_scr[kb] = jnp.concatenate(cols, axis=1)
            return seen

        lax.fori_loop(0, nb, body, zeros)

    q = q_ref[...].astype(F32)
    gq = N_HEADS // N_KV_HEADS
    rb = min(tq, ATTN_ROW_BLOCK)
    for g in range(N_KV_HEADS):
        for r in range(gq):
            h = g * gq + r
            qg_scr[g, r * tq:(r + 1) * tq, :] = q[:, h * HEAD_DIM:(h + 1) * HEAD_DIM].astype(BF16)
    m_scr[...] = jnp.full(m_scr.shape, NEG, F32)
    acc_scr[...] = jnp.zeros(acc_scr.shape, F32)

    def attn_blk(kb, carry):
        off = pl.multiple_of(kb * tk, tk)
        for g in range(N_KV_HEADS):
            kblk = k_ref[g, pl.ds(off, tk), :]
            vblk = v_ref[g, pl.ds(off, tk), :]
            s_scr[g] = lax.dot_general(qg_scr[g], kblk, nt, preferred_element_type=F32)
            for r0 in range(0, gq * tq, rb):
                rows = slice(r0, r0 + rb)
                s = s_scr[g, rows, :] + bias_scr[kb, r0 % tq:r0 % tq + rb, :]
                m_old = m_scr[g, rows, :]
                m_new = jnp.maximum(m_old, jnp.max(s, axis=1, keepdims=True))
                p_scr[g, rows, :] = jnp.exp2(s - jnp.concatenate([m_new] * ncol, axis=1)).astype(BF16)
                acc_scr[g, rows, :] = jnp.exp2(m_old - m_new) * acc_scr[g, rows, :]
                m_scr[g, rows, :] = m_new
            acc_scr[g] += jnp.dot(p_scr[g], vblk, preferred_element_type=F32)
        return carry

    lax.fori_loop(0, nb, attn_blk, 0)
    outs = []
    for g in range(N_KV_HEADS):
        for r in range(gq):
            acc = acc_scr[g, r * tq:(r + 1) * tq, :]
            outs.append(acc[:, :HEAD_DIM] / acc[:, HEAD_DIM:HEAD_DIM + 1])
    o = jnp.concatenate(outs, axis=1).astype(BF16)
    o_ref[...] = x_ref[...] + jnp.dot(o, wo_ref[...], preferred_element_type=F32)


def _dsa_attend(x2d, q, qi, wi, limit, k_hm, v_hm, ki, w_o, nkeys, *, batch, tq, tk, topk):
    rows = x2d.shape[0]
    per_batch = rows // batch
    nqt = per_batch // tq
    s_len = ki.shape[0] // batch
    nblk = s_len // tk
    gq = N_HEADS // N_KV_HEADS
    row_spec = lambda width: pl.BlockSpec((tq, width), lambda b, i, nk: (b * nqt + i, 0))
    once = pl.Buffered(1)
    k_spec = pl.BlockSpec((N_KV_HEADS, s_len, HEAD_DIM), lambda b, i, nk: (0, b, 0), pipeline_mode=once)
    v_spec = pl.BlockSpec((N_KV_HEADS, s_len, LANES), lambda b, i, nk: (0, b, 0), pipeline_mode=once)
    grid_spec = pltpu.PrefetchScalarGridSpec(
        num_scalar_prefetch=1,
        grid=(batch, nqt),
        in_specs=[row_spec(ATTN_WIDTH), row_spec(IDX_HEADS * IDX_DIM), row_spec(IDX_HEADS), row_spec(1),
                  row_spec(D_MODEL), k_spec, v_spec,
                  pl.BlockSpec((s_len, IDX_DIM), lambda b, i, nk: (b, 0), pipeline_mode=once),
                  pl.BlockSpec((ATTN_WIDTH, D_MODEL), lambda b, i, nk: (0, 0), pipeline_mode=once)],
        out_specs=row_spec(D_MODEL),
        scratch_shapes=[pltpu.VMEM((nblk, tq, tk), F32), pltpu.VMEM((nblk, tq, tk), F32),
                        pltpu.VMEM((N_KV_HEADS, gq * tq, HEAD_DIM), BF16),
                        pltpu.VMEM((N_KV_HEADS, gq * tq, tk), F32), pltpu.VMEM((N_KV_HEADS, gq * tq, tk), BF16),
                        pltpu.VMEM((N_KV_HEADS, gq * tq, LANES), F32), pltpu.VMEM((N_KV_HEADS, gq * tq, LANES), F32)],
    )
    return pl.pallas_call(
        functools.partial(_attn_body, tq=tq, tk=tk, topk=topk),
        out_shape=jax.ShapeDtypeStruct(x2d.shape, F32),
        grid_spec=grid_spec,
        compiler_params=_cparams(("arbitrary", "arbitrary")),
        name="dsa_attend",
    )(nkeys, q, qi, wi, limit, x2d, k_hm, v_hm, ki, w_o.astype(BF16))


def _round_up(x, m):
    return (x + m - 1) // m * m


def kernel(x_prompt, x_sample, state_ssm_re, state_ssm_im, cache_k, cache_v, cache_kidx, cache_conv,
           norm_mix, norm_ffn, ssm_lambda_re, ssm_lambda_im, ssm_log_dt, ssm_b_re, ssm_b_im,
           ssm_c_re, ssm_c_im, ssm_d, ssm_w_glu, ssm_b_glu, attn_w_in, attn_q_norm, attn_k_norm,
           attn_w_o, ffn_w_up, ffn_conv_w, ffn_conv_b, ffn_w_down):
    b_p, t_p, _ = x_prompt.shape
    b_s, t_s, _ = x_sample.shape
    past = cache_k.shape[2]
    xp = x_prompt.astype(F32).reshape(b_p * t_p, D_MODEL)
    xs = x_sample.astype(F32).reshape(b_s * t_s, D_MODEL)

    ops = _s5_operators(ssm_lambda_re[0], ssm_lambda_im[0], ssm_log_dt[0], ssm_b_re[0], ssm_b_im[0],
                        ssm_c_re[0], ssm_c_im[0], ssm_d[0])
    zero_state = jnp.zeros((S5_OCT, b_p, 2 * S5_HALF), F32)
    zp, htp = _s5_scan(_norm(xp, norm_mix[0]), ops, zero_state, batch=b_p, seq=t_p)
    zs, hts = _s5_scan(_norm(xs, norm_mix[0]), ops, _state_to_oct(state_ssm_re[0], state_ssm_im[0]),
                       batch=b_s, seq=t_s)
    xp = _glu_residual(zp, xp, ssm_w_glu[0], ssm_b_glu[0])
    xs = _glu_residual(zs, xs, ssm_w_glu[0], ssm_b_glu[0])
    re_p, im_p = _oct_to_state(htp, b_p)
    re_s, im_s = _oct_to_state(hts, b_s)

    ffn0 = _ffn_weights(ffn_w_up[0], ffn_conv_w[0], ffn_conv_b[0], ffn_w_down[0])
    xp, conv_p0 = _conv_ffn(xp, jnp.zeros((b_p, CONV_W - 1, D_FF), F32), norm_ffn[0], ffn0, batch=b_p, seq=t_p)
    xs, conv_s0 = _conv_ffn(xs, cache_conv[0], norm_ffn[0], ffn0, batch=b_s, seq=t_s)

    pos_p = jnp.arange(t_p, dtype=I32)
    pos_s = past + jnp.arange(t_s, dtype=I32)
    topk_p = min(TOPK_MAX, t_p // 4)
    topk_s = min(TOPK_MAX, (past + t_s) // 4)

    (q_p, khm_p, vhm_p, kf_p, vf_p, qi_p, kib_p, kif_p, wi_p) = _dsa_project(
        xp, norm_mix[1], attn_w_in[0], attn_q_norm[0], attn_k_norm[0], pos_p)
    tq_p = min(t_p, 256)
    tk_p = min(t_p, 512)
    lim_p = jnp.tile((pos_p // CHUNK + 1) * CHUNK, b_p).reshape(b_p * t_p, 1)
    q_end = (jnp.arange(t_p // tq_p, dtype=I32) + 1) * tq_p
    nk_p = jnp.minimum((q_end + tk_p - 1) // tk_p, t_p // tk_p).astype(I32)
    xp = _dsa_attend(xp, q_p, qi_p, wi_p, lim_p, khm_p, vhm_p, kib_p, attn_w_o[0], nk_p,
                     batch=b_p, tq=tq_p, tk=tk_p, topk=topk_p)

    (q_s, khm_s, vhm_s, kf_s, vf_s, qi_s, kib_s, kif_s, wi_s) = _dsa_project(
        xs, norm_mix[1], attn_w_in[0], attn_q_norm[0], attn_k_norm[0], pos_s)
    tk_s = 3 * LANES
    s_all = past + t_s
    s_pad = _round_up(s_all, tk_s)

    def with_cache(cache, new_hm):
        heads = new_hm.shape[0]
        c = cache.astype(BF16).transpose(2, 0, 1, 3)
        n = new_hm.reshape(heads, b_s, t_s, new_hm.shape[-1])
        z = jnp.zeros((heads, b_s, s_pad - s_all, new_hm.shape[-1]), BF16)
        return jnp.concatenate([c, n, z], axis=2).reshape(heads, b_s * s_pad, new_hm.shape[-1])

    ones_col = jnp.zeros(cache_v[0].shape[:-1] + (LANES - HEAD_DIM,), cache_v.dtype).at[..., 0].set(1.0)
    k_all = with_cache(cache_k[0], khm_s)
    v_all = with_cache(jnp.concatenate([cache_v[0], ones_col], axis=-1), vhm_s)
    ki_all = with_cache(cache_kidx[0][:, :, None, :], kib_s[None])[0]
    lim_s = jnp.full((b_s * t_s, 1), s_all, I32)
    nk_s = jnp.full((1,), s_pad // tk_s, I32)
    xs = _dsa_attend(xs, q_s, qi_s, wi_s, lim_s, k_all, v_all, ki_all, attn_w_o[0], nk_s,
                     batch=b_s, tq=t_s, tk=tk_s, topk=topk_s)

    ffn1 = _ffn_weights(ffn_w_up[1], ffn_conv_w[1], ffn_conv_b[1], ffn_w_down[1])
    xp, conv_p1 = _conv_ffn(xp, jnp.zeros((b_p, CONV_W - 1, D_FF), F32), norm_ffn[1], ffn1, batch=b_p, seq=t_p)
    xs, conv_s1 = _conv_ffn(xs, cache_conv[1], norm_ffn[1], ffn1, batch=b_s, seq=t_s)

    dt = x_prompt.dtype
    kv_shape_p = (1, b_p, t_p, N_KV_HEADS, HEAD_DIM)
    kv_shape_s = (1, b_s, t_s, N_KV_HEADS, HEAD_DIM)
    return (xp.reshape(x_prompt.shape).astype(dt), xs.reshape(x_sample.shape).astype(x_sample.dtype),
            re_p[None].astype(dt), im_p[None].astype(dt),
            re_s[None].astype(state_ssm_re.dtype), im_s[None].astype(state_ssm_im.dtype),
            kf_p.reshape(kv_shape_p).astype(dt), vf_p.reshape(kv_shape_p).astype(dt),
            kif_p.reshape(1, b_p, t_p, IDX_DIM).astype(dt),
            kf_s.reshape(kv_shape_s).astype(dt), vf_s.reshape(kv_shape_s).astype(dt),
            kif_s.reshape(1, b_s, t_s, IDX_DIM).astype(dt),
            jnp.stack([conv_p0, conv_p1]).astype(dt), jnp.stack([conv_s0, conv_s1]).astype(dt))
```

```python
import functools
import math

import jax
import jax.numpy as jnp
from jax import lax
from jax.experimental import pallas as pl
from jax.experimental.pallas import tpu as pltpu

F32 = jnp.float32
BF16 = jnp.bfloat16
I32 = jnp.int32

D_MODEL = 1024
CHUNK = 64
SSM_GROUP = 16
SSM_GROUPS = D_MODEL // SSM_GROUP
SSM_STATE = 64
N_HEADS = 16
N_KV_HEADS = 4
HEAD_DIM = 64
ATTN_WIDTH = N_HEADS * HEAD_DIM
KV_WIDTH = N_KV_HEADS * HEAD_DIM
IDX_HEADS = 8
IDX_DIM = 64
TOPK_MAX = 256
ROT_DIM = HEAD_DIM // 4
ROPE_THETA = 500000.0
OFF_K = ATTN_WIDTH
OFF_V = OFF_K + KV_WIDTH
OFF_QI = OFF_V + KV_WIDTH
OFF_KI = OFF_QI + IDX_HEADS * IDX_DIM
OFF_WI = OFF_KI + IDX_DIM
IN_COLS = OFF_WI + IDX_HEADS
D_FF = 2816
CONV_W = 3
EPS = 1e-6
NEG = -1e30

LANES = 128
SUBLANES = 8
VMEM_LIMIT_BYTES = 56 * 1024 * 1024

S5_L = 16
S5_OCT = D_MODEL // LANES
S5_OCT_GROUPS = LANES // SSM_GROUP
S5_HALF = S5_OCT_GROUPS * SSM_STATE

FFN_FC = 256
FFN_NC = D_FF // FFN_FC
assert FFN_NC % 2 == 1
IN_COLS_PAD = 2176
LOG2E = math.log2(math.e)
BISECT_MAX_ITERS = 40
ATTN_ROW_BLOCK = 64
TINY = 1.1754944e-38
COUNT_ROWS = 128


def _cparams(sem):
    return pltpu.CompilerParams(dimension_semantics=sem, vmem_limit_bytes=VMEM_LIMIT_BYTES)


def _rms(x, g):
    ms = jnp.mean(x * x, axis=-1, keepdims=True)
    return x * lax.rsqrt(ms + EPS) * g


def _norm_body(x_ref, g_ref, o_ref):
    o_ref[...] = _rms(x_ref[...], g_ref[...])


def _norm(x2d, g):
    rows = x2d.shape[0]
    tm = min(rows, 1024)
    return pl.pallas_call(
        _norm_body,
        out_shape=jax.ShapeDtypeStruct((rows, D_MODEL), F32),
        grid=(rows // tm,),
        in_specs=[pl.BlockSpec((tm, D_MODEL), lambda i: (i, 0)),
                  pl.BlockSpec((1, D_MODEL), lambda i: (0, 0))],
        out_specs=pl.BlockSpec((tm, D_MODEL), lambda i: (i, 0)),
        compiler_params=_cparams(("arbitrary",)),
        name="rmsnorm",
    )(x2d, g.astype(F32).reshape(1, D_MODEL))


def _s5_operators(lam_re, lam_im, log_dt, b_re, b_im, c_re, c_im, d_skip):
    L = S5_L
    lre = jnp.minimum(lam_re.astype(F32), -1e-4)
    lim = lam_im.astype(F32)
    dt = jnp.exp(log_dt.astype(F32))[:, None]
    n = jnp.arange(L + 1, dtype=F32)[:, None, None]
    mag = jnp.exp(n * (lre * dt)[None])
    ang = n * (lim * dt)[None]
    pr = mag * jnp.cos(ang)
    pi = mag * jnp.sin(ang)
    a_re, a_im = pr[1], pi[1]
    den = lre * lre + lim * lim
    n_re = a_re - 1.0
    f_re = (n_re * lre + a_im * lim) / den
    f_im = (a_im * lre - n_re * lim) / den
    br = b_re.astype(F32)
    bi = b_im.astype(F32)
    bb_re = f_re[..., None] * br - f_im[..., None] * bi
    bb_im = f_re[..., None] * bi + f_im[..., None] * br
    cr = c_re.astype(F32)
    ci = c_im.astype(F32)
    eye = jnp.eye(S5_OCT_GROUPS, dtype=F32)
    og = (S5_OCT, S5_OCT_GROUPS)

    tg = lambda m: jnp.moveaxis(m, 0, -1)
    crt, cit = tg(cr)[None, :, None], tg(ci)[None, :, None]
    prt, pit = tg(pr[:L].transpose(1, 0, 2))[:, None, None], tg(pi[:L].transpose(1, 0, 2))[:, None, None]
    brt, bit = tg(bb_re).transpose(1, 0, 2)[None, None], tg(bb_im).transpose(1, 0, 2)[None, None]
    kl = jnp.sum((crt * prt - cit * pit) * brt - (crt * pit + cit * prt) * bit, axis=3)
    kl = kl.transpose(0, 3, 1, 2)
    kbd = jnp.einsum("ljgdc,gh->ljgchd", kl.reshape((L,) + og + (SSM_GROUP, SSM_GROUP)), eye)
    kbd = kbd.reshape(L, S5_OCT, LANES, LANES)
    kpad = jnp.concatenate([jnp.zeros_like(kbd[:1]), kbd], axis=0)
    k2 = []
    for dlt in range(L // 2):
        top = jnp.concatenate([kpad[2 * dlt + 1], kpad[2 * dlt + 2]], axis=-1)
        bot = jnp.concatenate([kpad[2 * dlt], kpad[2 * dlt + 1]], axis=-1)
        k2.append(jnp.concatenate([top, bot], axis=-2))
    k2 = jnp.stack(k2, axis=1).astype(BF16)

    def oct_lanes(p):
        return p.reshape((L + 1,) + og + (SSM_STATE,)).transpose(1, 0, 2, 3).reshape(S5_OCT, L + 1, S5_HALF)

    pwr, pwi = oct_lanes(pr), oct_lanes(pi)

    def bd_in(m):
        m = m.reshape(og + (SSM_STATE, SSM_GROUP))
        return jnp.einsum("jgpc,gh->jgchp", m, eye).reshape(S5_OCT, LANES, S5_HALF)

    b_r, b_i = bd_in(bb_re)[:, None], bd_in(bb_im)[:, None]
    f_r, f_i = pwr[:, L - 1::-1][:, :, None, :], pwi[:, L - 1::-1][:, :, None, :]
    fre = (b_r * f_r - b_i * f_i).astype(BF16).reshape(S5_OCT, L * LANES, S5_HALF)
    fim = (b_r * f_i + b_i * f_r).astype(BF16).reshape(S5_OCT, L * LANES, S5_HALF)

    def bd_out(m):
        m = m.reshape(og + (SSM_GROUP, SSM_STATE))
        return jnp.einsum("jgcp,gh->jgphc", m, eye).reshape(S5_OCT, S5_HALF, LANES)

    c_r, c_i = bd_out(cr)[:, :, None, :], bd_out(ci)[:, :, None, :]
    e_r = pwr[:, 1:].transpose(0, 2, 1)[..., None]
    e_i = pwi[:, 1:].transpose(0, 2, 1)[..., None]
    ere = (c_r * e_r - c_i * e_i).astype(BF16).reshape(S5_OCT, S5_HALF, L * LANES)
    eim = (-(c_r * e_i + c_i * e_r)).astype(BF16).reshape(S5_OCT, S5_HALF, L * LANES)

    al = jnp.concatenate([pwr[:, L:], pwi[:, L:]], axis=-1)
    dsk = d_skip.astype(F32).reshape(S5_OCT, 1, LANES)
    dsk2 = jnp.concatenate([dsk, dsk], axis=-1)
    return fre, fim, ere, eim, k2, al, dsk2


def _s5_body(u_ref, fre_ref, fim_ref, ere_ref, eim_ref, k2_ref, al_ref, dsk_ref, h0_ref, z_ref, ht_ref,
             ucat_scr, v_scr, hp_scr, ys_scr, *, bpt, nck):
    L = S5_L
    H = S5_HALF
    tr = bpt * nck

    def token(s):
        return u_ref[pl.ds(s, tr, stride=L), :]

    for s in range(L):
        ucat_scr[:, s * LANES:(s + 1) * LANES] = token(s).astype(BF16)

    ucat = ucat_scr[...]
    v_scr[:, :H] = jnp.dot(ucat, fre_ref[0], preferred_element_type=F32)
    v_scr[:, H:] = jnp.dot(ucat, fim_ref[0], preferred_element_type=F32)

    alr = al_ref[0, :, :H]
    ali = al_ref[0, :, H:]
    for b in range(bpt):
        def step(k, carry, b=b):
            hr, hi = carry
            row = b * nck + k
            hp_scr[pl.ds(row, 1), :H] = hr
            hp_scr[pl.ds(row, 1), H:] = hi
            wr = v_scr[pl.ds(row, 1), :H]
            wi = v_scr[pl.ds(row, 1), H:]
            return hr * alr - hi * ali + wr, hr * ali + hi * alr + wi

        h0r = h0_ref[0, 0, b:b + 1, :H]
        h0i = h0_ref[0, 0, b:b + 1, H:]
        hr, hi = lax.fori_loop(0, nck, step, (h0r, h0i))
        ht_ref[0, 0, b:b + 1, :H] = hr
        ht_ref[0, 0, b:b + 1, H:] = hi

    ys_scr[...] = (jnp.dot(hp_scr[:, :H].astype(BF16), ere_ref[0], preferred_element_type=F32)
                   + jnp.dot(hp_scr[:, H:].astype(BF16), eim_ref[0], preferred_element_type=F32))
    dsk = dsk_ref[0]
    pw = 2 * LANES
    for t2 in range(L // 2):
        y = ys_scr[:, t2 * pw:(t2 + 1) * pw]
        for s2 in range(t2 + 1):
            y = y + jnp.dot(ucat_scr[:, s2 * pw:(s2 + 1) * pw], k2_ref[0, t2 - s2], preferred_element_type=F32)
        z = jax.nn.gelu(y + dsk * jnp.concatenate([token(2 * t2), token(2 * t2 + 1)], axis=1))
        z_ref[pl.ds(2 * t2, tr, stride=L), :] = z[:, :LANES]
        z_ref[pl.ds(2 * t2 + 1, tr, stride=L), :] = z[:, LANES:]


def _s5_scan(u2d, ops, h0, *, batch, seq):
    L = S5_L
    nck = seq // L
    n_rows = batch * nck
    bpt = 1 if nck % SUBLANES == 0 else batch
    tr = bpt * nck
    n_tiles = n_rows // tr
    h0 = h0.reshape(S5_OCT, n_tiles, bpt, 2 * S5_HALF)
    tok_spec = pl.BlockSpec((tr * L, LANES), lambda j, i: (i, j))
    per_oct = lambda a: pl.BlockSpec((1,) + a.shape[1:], lambda j, i: (j,) + (0,) * (a.ndim - 1))
    state_spec = pl.BlockSpec((1, 1, bpt, 2 * S5_HALF), lambda j, i: (j, i, 0, 0))
    z, ht = pl.pallas_call(
        functools.partial(_s5_body, bpt=bpt, nck=nck),
        out_shape=(jax.ShapeDtypeStruct(u2d.shape, F32),
                   jax.ShapeDtypeStruct((S5_OCT, n_tiles, bpt, 2 * S5_HALF), F32)),
        grid=(S5_OCT, n_tiles),
        in_specs=[tok_spec] + [per_oct(a) for a in ops] + [state_spec],
        out_specs=(tok_spec, state_spec),
        scratch_shapes=[pltpu.VMEM((tr, L * LANES), BF16), pltpu.VMEM((tr, 2 * S5_HALF), F32),
                        pltpu.VMEM((tr, 2 * S5_HALF), F32), pltpu.VMEM((tr, L * LANES), F32)],
        compiler_params=_cparams(("arbitrary", "arbitrary")),
        name="s5_scan",
    )(u2d, *ops, h0)
    return z, ht


def _state_to_oct(s_re, s_im):
    b = s_re.shape[0]
    r = s_re.astype(F32).reshape(b, S5_OCT, S5_HALF).transpose(1, 0, 2)
    i = s_im.astype(F32).reshape(b, S5_OCT, S5_HALF).transpose(1, 0, 2)
    return jnp.concatenate([r, i], axis=-1)


def _oct_to_state(ht, batch):
    ht = ht.reshape(S5_OCT, batch, 2 * S5_HALF)
    r = ht[..., :S5_HALF].transpose(1, 0, 2).reshape(batch, SSM_GROUPS, SSM_STATE)
    i = ht[..., S5_HALF:].transpose(1, 0, 2).reshape(batch, SSM_GROUPS, SSM_STATE)
    return r, i


def _glu_body(z_ref, x_ref, w_ref, b_ref, o_ref):
    g = jnp.dot(z_ref[...].astype(BF16), w_ref[...], preferred_element_type=F32) + b_ref[...]
    o_ref[...] = x_ref[...] + g[:, :D_MODEL] * jax.nn.sigmoid(g[:, D_MODEL:])


def _glu_residual(z, x2d, w_glu, b_glu):
    rows = x2d.shape[0]
    tm = min(rows, 512)
    row_spec = pl.BlockSpec((tm, D_MODEL), lambda i: (i, 0))
    return pl.pallas_call(
        _glu_body,
        out_shape=jax.ShapeDtypeStruct(x2d.shape, F32),
        grid=(rows // tm,),
        in_specs=[row_spec, row_spec,
                  pl.BlockSpec((D_MODEL, 2 * D_MODEL), lambda i: (0, 0)),
                  pl.BlockSpec((1, 2 * D_MODEL), lambda i: (0, 0))],
        out_specs=row_spec,
        compiler_params=_cparams(("arbitrary",)),
        name="glu_residual",
    )(z, x2d, w_glu.astype(BF16), b_glu.astype(F32).reshape(1, 2 * D_MODEL))


def _ffn_body(x_ref, xh_ref, cs_ref, g_ref, wa_ref, wb_ref, cw_ref, cb_ref, wd_ref,
              o_ref, co_ref, xn_scr, ha0, hb0, hh0, ha1, hb1, hh1, gt0, gt1, acc_scr, *, tm, tiles_per_seq):
    i = pl.program_id(0)
    first = (i % tiles_per_seq) == 0
    g = g_ref[...]
    xn_scr[...] = _rms(x_ref[...], g).astype(BF16)
    xnh = _rms(xh_ref[...], g).astype(BF16)
    rows = lax.broadcasted_iota(I32, (tm, FFN_FC), 0)
    sets = ((ha0, hb0, hh0, gt0), (ha1, hb1, hh1, gt1))

    def up(c, s):
        ha, hb, hh, _ = sets[s]
        xn = xn_scr[...]
        ha[...] = jnp.dot(xn, wa_ref[c], preferred_element_type=F32)
        hb[...] = jnp.dot(xn, wb_ref[c], preferred_element_type=F32)
        hh[...] = jnp.dot(xnh, wa_ref[c], preferred_element_type=F32)

    def act(c, s):
        ha_ref, hb_ref, hh_ref, gt = sets[s]
        ha = ha_ref[...]
        hh = jnp.where(first, cs_ref[0, c], hh_ref[...])
        co_ref[0, c] = ha[tm - SUBLANES:, :]
        h6 = hh[6:7, :]
        h7 = hh[7:8, :]
        prev1 = jnp.where(rows == 0, h7, pltpu.roll(ha, 1, 0))
        prev2 = jnp.where(rows == 0, h6, jnp.where(rows == 1, h7, pltpu.roll(ha, 2, 0)))
        cw = cw_ref[c]
        cv = cb_ref[c] + cw[0:1, :] * prev2 + cw[1:2, :] * prev1 + cw[2:3, :] * ha
        gt[...] = (jax.nn.gelu(cv) * hb_ref[...]).astype(BF16)

    def down(c, s):
        acc_scr[...] += jnp.dot(sets[s][3][...], wd_ref[c], preferred_element_type=F32)

    acc_scr[...] = jnp.zeros(acc_scr.shape, F32)
    gt1[...] = jnp.zeros(gt1.shape, BF16)
    up(0, 0)

    def body(k, carry):
        c = 2 * k
        up(c + 1, 1)
        act(c, 0)
        down(jnp.maximum(c - 1, 0), 1)
        up(c + 2, 0)
        act(c + 1, 1)
        down(c, 0)
        return carry

    lax.fori_loop(0, (FFN_NC - 1) // 2, body, 0)
    act(FFN_NC - 1, 0)
    down(FFN_NC - 2, 1)
    down(FFN_NC - 1, 0)
    o_ref[...] = x_ref[...] + acc_scr[...]


def _ffn_weights(w_up, conv_w, conv_b, w_down):
    wa = w_up[:, :D_FF].astype(BF16).reshape(D_MODEL, FFN_NC, FFN_FC).transpose(1, 0, 2)
    wb = w_up[:, D_FF:].astype(BF16).reshape(D_MODEL, FFN_NC, FFN_FC).transpose(1, 0, 2)
    cw = jnp.pad(conv_w.astype(F32), ((0, SUBLANES - CONV_W), (0, 0)))
    cw = cw.reshape(SUBLANES, FFN_NC, FFN_FC).transpose(1, 0, 2)
    cb = conv_b.astype(F32).reshape(FFN_NC, 1, FFN_FC)
    wd = w_down.astype(BF16).reshape(FFN_NC, FFN_FC, D_MODEL)
    return wa, wb, cw, cb, wd


def _conv_ffn(x2d, conv_state, g, weights, *, batch, seq):
    wa, wb, cw, cb, wd = weights
    tm = min(seq, 512)
    tiles_per_seq = seq // tm
    hb = tm // SUBLANES
    cs = jnp.pad(conv_state.astype(F32), ((0, 0), (SUBLANES - (CONV_W - 1), 0), (0, 0)))
    cs = cs.reshape(batch, SUBLANES, FFN_NC, FFN_FC).transpose(0, 2, 1, 3)
    const3 = lambda i: (0, 0, 0)
    up_set = [pltpu.VMEM((tm, FFN_FC), F32), pltpu.VMEM((tm, FFN_FC), F32), pltpu.VMEM((SUBLANES, FFN_FC), F32)]
    gate = pltpu.VMEM((tm, FFN_FC), BF16)
    scratch = [pltpu.VMEM((tm, D_MODEL), BF16)] + up_set + up_set + [gate, gate, pltpu.VMEM((tm, D_MODEL), F32)]
    out, co = pl.pallas_call(
        functools.partial(_ffn_body, tm=tm, tiles_per_seq=tiles_per_seq),
        out_shape=(jax.ShapeDtypeStruct(x2d.shape, F32),
                   jax.ShapeDtypeStruct((batch, FFN_NC, SUBLANES, FFN_FC), F32)),
        grid=(batch * tiles_per_seq,),
        in_specs=[pl.BlockSpec((tm, D_MODEL), lambda i: (i, 0)),
                  pl.BlockSpec((SUBLANES, D_MODEL), lambda i: (jnp.maximum(i * hb - 1, 0), 0)),
                  pl.BlockSpec((1, FFN_NC, SUBLANES, FFN_FC), lambda i: (i // tiles_per_seq, 0, 0, 0)),
                  pl.BlockSpec((1, D_MODEL), lambda i: (0, 0)),
                  pl.BlockSpec((FFN_NC, D_MODEL, FFN_FC), const3),
                  pl.BlockSpec((FFN_NC, D_MODEL, FFN_FC), const3),
                  pl.BlockSpec((FFN_NC, SUBLANES, FFN_FC), const3),
                  pl.BlockSpec((FFN_NC, 1, FFN_FC), const3),
                  pl.BlockSpec((FFN_NC, FFN_FC, D_MODEL), const3)],
        out_specs=(pl.BlockSpec((tm, D_MODEL), lambda i: (i, 0)),
                   pl.BlockSpec((1, FFN_NC, SUBLANES, FFN_FC), lambda i: (i // tiles_per_seq, 0, 0, 0))),
        scratch_shapes=scratch,
        compiler_params=_cparams(("arbitrary",)),
        name="conv_ffn",
    )(x2d, x2d, cs, g.astype(F32).reshape(1, D_MODEL), wa, wb, cw, cb, wd)
    new_state = co[:, :, SUBLANES - (CONV_W - 1):, :].transpose(0, 2, 1, 3).reshape(batch, CONV_W - 1, D_FF)
    return out, new_state


def _rope_tables(pos):
    half = ROT_DIM // 2
    inv = ROPE_THETA ** (-jnp.arange(half, dtype=F32) / half)
    ang = pos.astype(F32)[:, None] * inv[None, :]
    cos = jnp.cos(ang)
    sin = jnp.sin(ang)
    lane = jnp.arange(LANES) % HEAD_DIM
    idx = lane % half
    cc = jnp.where(lane[None] < ROT_DIM, cos[:, idx], 1.0)
    s1 = jnp.where((lane[None] >= half) & (lane[None] < ROT_DIM), sin[:, idx], 0.0)
    s2 = jnp.where(lane[None] < half, -sin[:, idx], 0.0)
    return cc.astype(F32), s1.astype(F32), s2.astype(F32)


def _proj_body(x_ref, g_ref, w_ref, qn_ref, kn_ref, cc_ref, s1_ref, s2_ref, bd_ref,
               q_ref, khm_ref, vhm_ref, kf_ref, vf_ref, qi_ref, kib_ref, kif_ref, wi_ref):
    xn = _rms(x_ref[...], g_ref[...]).astype(BF16)
    proj = jnp.dot(xn, w_ref[...], preferred_element_type=F32)
    cc = cc_ref[...]
    s1 = s1_ref[...]
    s2 = s2_ref[...]
    bd = bd_ref[...]

    def rope(y):
        return y * cc + pltpu.roll(y, ROT_DIM // 2, 1) * s1 + pltpu.roll(y, LANES - ROT_DIM // 2, 1) * s2

    def head_norm(y, gain):
        ms = jnp.dot((y * y).astype(BF16), bd, preferred_element_type=F32)
        return y * lax.rsqrt(ms + EPS) * gain

    qn = qn_ref[...]
    kn = kn_ref[...]
    scale = HEAD_DIM ** -0.5 * LOG2E
    one_col = jnp.where(lax.broadcasted_iota(I32, (x_ref.shape[0], HEAD_DIM), 1) == 0, 1.0, 0.0)
    for c in range(ATTN_WIDTH // LANES):
        y = proj[:, c * LANES:(c + 1) * LANES]
        q_ref[:, c * LANES:(c + 1) * LANES] = (rope(head_norm(y, qn)) * scale).astype(BF16)
    for c in range(KV_WIDTH // LANES):
        y = proj[:, OFF_K + c * LANES:OFF_K + (c + 1) * LANES]
        k = rope(head_norm(y, kn))
        kf_ref[:, c * LANES:(c + 1) * LANES] = k
        khm_ref[2 * c] = k[:, :HEAD_DIM].astype(BF16)
        khm_ref[2 * c + 1] = k[:, HEAD_DIM:].astype(BF16)
        v = proj[:, OFF_V + c * LANES:OFF_V + (c + 1) * LANES]
        vf_ref[:, c * LANES:(c + 1) * LANES] = v
        vhm_ref[2 * c] = jnp.concatenate([v[:, :HEAD_DIM], one_col], axis=1).astype(BF16)
        vhm_ref[2 * c + 1] = jnp.concatenate([v[:, HEAD_DIM:], one_col], axis=1).astype(BF16)
    iscale = IDX_DIM ** -0.5
    for c in range(IDX_HEADS * IDX_DIM // LANES):
        y = proj[:, OFF_QI + c * LANES:OFF_QI + (c + 1) * LANES]
        qi_ref[:, c * LANES:(c + 1) * LANES] = (rope(y) * iscale).astype(BF16)
    tail = proj[:, OFF_KI:OFF_KI + LANES]
    ki = rope(tail)[:, :IDX_DIM]
    kif_ref[...] = ki
    kib_ref[...] = ki.astype(BF16)
    wi_ref[...] = tail[:, IDX_DIM:IDX_DIM + IDX_HEADS] * (IDX_HEADS ** -0.5)


def _dsa_project(x2d, g, w_in, q_norm, k_norm, pos):
    rows = x2d.shape[0]
    tm = min(rows, 256)
    w = jnp.pad(w_in.astype(BF16), ((0, 0), (0, IN_COLS_PAD - IN_COLS)))
    if pos.shape[0] < tm:
        pos = jnp.tile(pos, tm // pos.shape[0])
    ntab = pos.shape[0] // tm
    cc, s1, s2 = _rope_tables(pos)
    lane = jnp.arange(LANES)
    bd = jnp.where((lane[:, None] // HEAD_DIM) == (lane[None, :] // HEAD_DIM), 1.0 / HEAD_DIM, 0.0).astype(BF16)
    qn = jnp.tile(q_norm.astype(F32), LANES // HEAD_DIM).reshape(1, LANES)
    kn = jnp.tile(k_norm.astype(F32), LANES // HEAD_DIM).reshape(1, LANES)
    row_spec = lambda width: pl.BlockSpec((tm, width), lambda i: (i, 0))
    const = lambda shape: pl.BlockSpec(shape, lambda i: (0,) * len(shape))
    hm_spec = pl.BlockSpec((N_KV_HEADS, tm, HEAD_DIM), lambda i: (0, i, 0))
    vhm_spec = pl.BlockSpec((N_KV_HEADS, tm, LANES), lambda i: (0, i, 0))
    tab_spec = pl.BlockSpec((tm, LANES), lambda i: (i % ntab, 0))
    return pl.pallas_call(
        _proj_body,
        out_shape=(jax.ShapeDtypeStruct((rows, ATTN_WIDTH), BF16),
                   jax.ShapeDtypeStruct((N_KV_HEADS, rows, HEAD_DIM), BF16),
                   jax.ShapeDtypeStruct((N_KV_HEADS, rows, LANES), BF16),
                   jax.ShapeDtypeStruct((rows, KV_WIDTH), F32),
                   jax.ShapeDtypeStruct((rows, KV_WIDTH), F32),
                   jax.ShapeDtypeStruct((rows, IDX_HEADS * IDX_DIM), BF16),
                   jax.ShapeDtypeStruct((rows, IDX_DIM), BF16),
                   jax.ShapeDtypeStruct((rows, IDX_DIM), F32),
                   jax.ShapeDtypeStruct((rows, IDX_HEADS), F32)),
        grid=(rows // tm,),
        in_specs=[row_spec(D_MODEL), const((1, D_MODEL)), const((D_MODEL, IN_COLS_PAD)),
                  const((1, LANES)), const((1, LANES)),
                  tab_spec, tab_spec, tab_spec, const((LANES, LANES))],
        out_specs=(row_spec(ATTN_WIDTH), hm_spec, vhm_spec, row_spec(KV_WIDTH), row_spec(KV_WIDTH),
                   row_spec(IDX_HEADS * IDX_DIM), row_spec(IDX_DIM), row_spec(IDX_DIM), row_spec(IDX_HEADS)),
        compiler_params=_cparams(("arbitrary",)),
        name="dsa_project",
    )(x2d, g.astype(F32).reshape(1, D_MODEL), w, qn, kn, cc, s1, s2, bd)


def _attn_body(nk_ref, q_ref, qi_ref, wi_ref, lim_ref, x_ref, k_ref, v_ref, ki_ref, wo_ref,
               o_ref, sc_scr, bias_scr, qg_scr, s_scr, p_scr, m_scr, acc_scr, *, tq, tk, topk):
    i = pl.program_id(1)
    nb = nk_ref[i]
    ncol = tk // LANES
    kf = float(topk)
    qi = qi_ref[...].astype(F32)
    wi = wi_ref[...]
    lim = lim_ref[...]
    qih = [qi[:, h * IDX_DIM:(h + 1) * IDX_DIM].astype(BF16) for h in range(IDX_HEADS)]
    wih = [wi[:, h:h + 1] for h in range(IDX_HEADS)]
    nt = (((1,), (1,)), ((), ()))
    rb = min(tq, ATTN_ROW_BLOCK)
    hr = min(tq, COUNT_ROWS)
    lane_tk = lax.broadcasted_iota(I32, (tq, tk), 1)

    def fold(x, op):
        part = x[:, :LANES]
        for c in range(1, ncol):
            part = op(part, x[:, c * LANES:(c + 1) * LANES])
        return part

    def rows_all(x, op):
        return jnp.broadcast_to(op(x, axis=1, keepdims=True), x.shape)

    def tile_cols(x):
        return jnp.concatenate([x] * ncol, axis=1)

    def score_blk(kb, carry):
        mn, mx, cpos, cnn = carry
        off = pl.multiple_of(kb * tk, tk)
        kib = ki_ref[pl.ds(off, tk), :]
        sc = jnp.zeros((tq, tk), F32)
        for h in range(IDX_HEADS):
            lg = lax.dot_general(qih[h], kib, nt, preferred_element_type=F32)
            sc = sc + jnp.maximum(lg, 0.0) * wih[h]
        allowed = lane_tk + off < lim
        scm = jnp.where(allowed, sc, NEG)
        sc_scr[kb] = scm
        mn = jnp.minimum(mn, fold(jnp.where(allowed, sc, -NEG), jnp.minimum))
        mx = jnp.maximum(mx, fold(scm, jnp.maximum))
        cpos = cpos + fold(jnp.where(scm > 0.0, 1.0, 0.0), jnp.add)
        cnn = cnn + fold(jnp.where(scm >= 0.0, 1.0, 0.0), jnp.add)
        return mn, mx, cpos, cnn

    zeros = jnp.zeros((tq, LANES), F32)
    mn, mx, cpos, cnn = lax.fori_loop(
        0, nb, score_blk, (jnp.full((tq, LANES), -NEG, F32), jnp.full((tq, LANES), NEG, F32), zeros, zeros))
    smin = rows_all(mn, jnp.min)
    smax = rows_all(mx, jnp.max)
    cpos = rows_all(cpos, jnp.sum)
    cnn = rows_all(cnn, jnp.sum)

    def count_ge(x):
        parts = []
        for r0 in range(0, tq, hr):
            xt = tile_cols(x[r0:r0 + hr])

            def body(kb, acc, r0=r0, xt=xt):
                return acc + fold(jnp.where(sc_scr[kb, r0:r0 + hr, :] >= xt, 1.0, 0.0), jnp.add)

            parts.append(lax.fori_loop(0, nb, body, jnp.zeros((hr, LANES), F32)))
        return rows_all(jnp.concatenate(parts, axis=0), jnp.sum)

    above_zero = cpos >= kf
    at_zero = jnp.logical_and(jnp.logical_not(above_zero), cnn >= kf)
    n_allowed = jnp.broadcast_to(jnp.minimum(lim, nb * tk).astype(F32), (tq, LANES))
    lo0 = jnp.where(above_zero, TINY, jnp.where(at_zero, 0.0, smin))
    hi0 = jnp.where(above_zero, smax + (jnp.abs(smax) + 1e-30) * 1e-6, jnp.where(at_zero, TINY, 0.0))
    cnt0 = jnp.where(above_zero, cpos, jnp.where(at_zero, cnn, n_allowed))
    frozen = at_zero

    def any_over(cnt_lo):
        return jnp.max(jnp.where(jnp.logical_and(cnt_lo > kf, jnp.logical_not(frozen)), 1.0, 0.0)) > 0.5

    def bis_step(lo, hi, cnt_lo):
        mid = 0.5 * (lo + hi)
        cnt = count_ge(mid)
        up = jnp.logical_and(cnt >= kf, jnp.logical_not(frozen))
        dn = jnp.logical_and(cnt < kf, jnp.logical_not(frozen))
        return jnp.where(up, mid, lo), jnp.where(dn, mid, hi), jnp.where(up, cnt, cnt_lo)

    def bis_cond(c):
        return c[4]

    def bis_body(c):
        it, lo, hi, cnt_lo, _ = c
        lo, hi, cnt_lo = bis_step(*bis_step(lo, hi, cnt_lo))
        return it + 2, lo, hi, cnt_lo, jnp.logical_and(any_over(cnt_lo), it + 2 < BISECT_MAX_ITERS)

    _, lo, hi, cnt_lo, _ = lax.while_loop(bis_cond, bis_body, (jnp.int32(0), lo0, hi0, cnt0, any_over(cnt0)))
    ties = jnp.max(jnp.where(cnt_lo > kf, 1.0, 0.0)) > 0.5
    lo_t = tile_cols(lo)
    hi_t = tile_cols(hi)

    @pl.when(jnp.logical_not(ties))
    def _():
        def body(kb, carry):
            s = sc_scr[kb]
            bias_scr[kb] = jnp.where((s >= lo_t) & (s > 0.5 * NEG), 0.0, NEG)
            return carry

        lax.fori_loop(0, nb, body, 0)

    @pl.when(ties)
    def _():
        need = kf - count_ge(hi)
        tri = (lax.broadcasted_iota(I32, (LANES, LANES), 0)
               <= lax.broadcasted_iota(I32, (LANES, LANES), 1)).astype(BF16)

        def body(kb, seen):
            s = sc_scr[kb]
            cols = []
            for c in range(ncol):
                sc = s[:, c * LANES:(c + 1) * LANES]
                tie = (sc >= lo) & (sc < hi)
                tief = jnp.where(tie, 1.0, 0.0)
                incl = jnp.dot(tief.astype(BF16), tri, preferred_element_type=F32)
                take = tie & ((seen + incl - tief) < need)
                sel = ((sc >= hi) | take) & (sc > 0.5 * NEG)
                cols.append(jnp.where(sel, 0.0, NEG))
                seen = seen + incl[:, LANES - 1:]
            bias_scr[kb] = jnp.concatenate(cols, axis=1)
            return seen

        lax.fori_loop(0, nb, body, zeros)

    q = q_ref[...].astype(F32)
    gq = N_HEADS // N_KV_HEADS
    for g in range(N_KV_HEADS):
        for r in range(gq):
            h = g * gq + r
            qg_scr[g, r * tq:(r + 1) * tq, :] = q[:, h * HEAD_DIM:(h + 1) * HEAD_DIM].astype(BF16)
    m_scr[...] = jnp.full(m_scr.shape, NEG, F32)
    acc_scr[...] = jnp.zeros(acc_scr.shape, F32)

    def attn_blk(kb, carry):
        off = pl.multiple_of(kb * tk, tk)
        for g in range(N_KV_HEADS):
            kblk = k_ref[g, pl.ds(off, tk), :]
            vblk = v_ref[g, pl.ds(off, tk), :]
            s_scr[g] = lax.dot_general(qg_scr[g], kblk, nt, preferred_element_type=F32)
            for r0 in range(0, gq * tq, rb):
                rows = slice(r0, r0 + rb)
                s = s_scr[g, rows, :] + bias_scr[kb, r0 % tq:r0 % tq + rb, :]
                m_old = m_scr[g, rows, :]
                m_new = jnp.maximum(m_old, jnp.max(s, axis=1, keepdims=True))
                p_scr[g, rows, :] = jnp.exp2(s - jnp.concatenate([m_new] * ncol, axis=1)).astype(BF16)
                acc_scr[g, rows, :] = jnp.exp2(m_old - m_new) * acc_scr[g, rows, :]
                m_scr[g, rows, :] = m_new
            acc_scr[g] += jnp.dot(p_scr[g], vblk, preferred_element_type=F32)
        return carry

    lax.fori_loop(0, nb, attn_blk, 0)
    outs = []
    for g in range(N_KV_HEADS):
        for r in range(gq):
            acc = acc_scr[g, r * tq:(r + 1) * tq, :]
            outs.append(acc[:, :HEAD_DIM] / acc[:, HEAD_DIM:HEAD_DIM + 1])
    o = jnp.concatenate(outs, axis=1).astype(BF16)
    o_ref[...] = x_ref[...] + jnp.dot(o, wo_ref[...], preferred_element_type=F32)


def _dsa_attend(x2d, q, qi, wi, limit, k_hm, v_hm, ki, w_o, nkeys, *, batch, tq, tk, topk):
    rows = x2d.shape[0]
    per_batch = rows // batch
    nqt = per_batch // tq
    s_len = ki.shape[0] // batch
    nblk = s_len // tk
    gq = N_HEADS // N_KV_HEADS
    row_spec = lambda width: pl.BlockSpec((tq, width), lambda b, i, nk: (b * nqt + i, 0))
    once = pl.Buffered(1)
    k_spec = pl.BlockSpec((N_KV_HEADS, s_len, HEAD_DIM), lambda b, i, nk: (0, b, 0), pipeline_mode=once)
    v_spec = pl.BlockSpec((N_KV_HEADS, s_len, LANES), lambda b, i, nk: (0, b, 0), pipeline_mode=once)
    grid_spec = pltpu.PrefetchScalarGridSpec(
        num_scalar_prefetch=1,
        grid=(batch, nqt),
        in_specs=[row_spec(ATTN_WIDTH), row_spec(IDX_HEADS * IDX_DIM), row_spec(IDX_HEADS), row_spec(1),
                  row_spec(D_MODEL), k_spec, v_spec,
                  pl.BlockSpec((s_len, IDX_DIM), lambda b, i, nk: (b, 0), pipeline_mode=once),
                  pl.BlockSpec((ATTN_WIDTH, D_MODEL), lambda b, i, nk: (0, 0), pipeline_mode=once)],
        out_specs=row_spec(D_MODEL),
        scratch_shapes=[pltpu.VMEM((nblk, tq, tk), F32), pltpu.VMEM((nblk, tq, tk), F32),
                        pltpu.VMEM((N_KV_HEADS, gq * tq, HEAD_DIM), BF16),
                        pltpu.VMEM((N_KV_HEADS, gq * tq, tk), F32), pltpu.VMEM((N_KV_HEADS, gq * tq, tk), BF16),
                        pltpu.VMEM((N_KV_HEADS, gq * tq, LANES), F32), pltpu.VMEM((N_KV_HEADS, gq * tq, LANES), F32)],
    )
    return pl.pallas_call(
        functools.partial(_attn_body, tq=tq, tk=tk, topk=topk),
        out_shape=jax.ShapeDtypeStruct(x2d.shape, F32),
        grid_spec=grid_spec,
        compiler_params=_cparams(("arbitrary", "arbitrary")),
        name="dsa_attend",
    )(nkeys, q, qi, wi, limit, x2d, k_hm, v_hm, ki, w_o.astype(BF16))


def _round_up(x, m):
    return (x + m - 1) // m * m


def kernel(x_prompt, x_sample, state_ssm_re, state_ssm_im, cache_k, cache_v, cache_kidx, cache_conv,
           norm_mix, norm_ffn, ssm_lambda_re, ssm_lambda_im, ssm_log_dt, ssm_b_re, ssm_b_im,
           ssm_c_re, ssm_c_im, ssm_d, ssm_w_glu, ssm_b_glu, attn_w_in, attn_q_norm, attn_k_norm,
           attn_w_o, ffn_w_up, ffn_conv_w, ffn_conv_b, ffn_w_down):
    b_p, t_p, _ = x_prompt.shape
    b_s, t_s, _ = x_sample.shape
    past = cache_k.shape[2]
    xp = x_prompt.astype(F32).reshape(b_p * t_p, D_MODEL)
    xs = x_sample.astype(F32).reshape(b_s * t_s, D_MODEL)

    ops = _s5_operators(ssm_lambda_re[0], ssm_lambda_im[0], ssm_log_dt[0], ssm_b_re[0], ssm_b_im[0],
                        ssm_c_re[0], ssm_c_im[0], ssm_d[0])
    zero_state = jnp.zeros((S5_OCT, b_p, 2 * S5_HALF), F32)
    zp, htp = _s5_scan(_norm(xp, norm_mix[0]), ops, zero_state, batch=b_p, seq=t_p)
    zs, hts = _s5_scan(_norm(xs, norm_mix[0]), ops, _state_to_oct(state_ssm_re[0], state_ssm_im[0]),
                       batch=b_s, seq=t_s)
    xp = _glu_residual(zp, xp, ssm_w_glu[0], ssm_b_glu[0])
    xs = _glu_residual(zs, xs, ssm_w_glu[0], ssm_b_glu[0])
    re_p, im_p = _oct_to_state(htp, b_p)
    re_s, im_s = _oct_to_state(hts, b_s)

    ffn0 = _ffn_weights(ffn_w_up[0], ffn_conv_w[0], ffn_conv_b[0], ffn_w_down[0])
    xp, conv_p0 = _conv_ffn(xp, jnp.zeros((b_p, CONV_W - 1, D_FF), F32), norm_ffn[0], ffn0, batch=b_p, seq=t_p)
    xs, conv_s0 = _conv_ffn(xs, cache_conv[0], norm_ffn[0], ffn0, batch=b_s, seq=t_s)

    pos_p = jnp.arange(t_p, dtype=I32)
    pos_s = past + jnp.arange(t_s, dtype=I32)
    topk_p = min(TOPK_MAX, t_p // 4)
    topk_s = min(TOPK_MAX, (past + t_s) // 4)

    (q_p, khm_p, vhm_p, kf_p, vf_p, qi_p, kib_p, kif_p, wi_p) = _dsa_project(
        xp, norm_mix[1], attn_w_in[0], attn_q_norm[0], attn_k_norm[0], pos_p)
    tq_p = min(t_p, 256)
    tk_p = min(t_p, 512)
    lim_p = jnp.tile((pos_p // CHUNK + 1) * CHUNK, b_p).reshape(b_p * t_p, 1)
    q_end = (jnp.arange(t_p // tq_p, dtype=I32) + 1) * tq_p
    nk_p = jnp.minimum((q_end + tk_p - 1) // tk_p, t_p // tk_p).astype(I32)
    xp = _dsa_attend(xp, q_p, qi_p, wi_p, lim_p, khm_p, vhm_p, kib_p, attn_w_o[0], nk_p,
                     batch=b_p, tq=tq_p, tk=tk_p, topk=topk_p)

    (q_s, khm_s, vhm_s, kf_s, vf_s, qi_s, kib_s, kif_s, wi_s) = _dsa_project(
        xs, norm_mix[1], attn_w_in[0], attn_q_norm[0], attn_k_norm[0], pos_s)
    tk_s = 3 * LANES
    s_all = past + t_s
    s_pad = _round_up(s_all, tk_s)

    def with_cache(cache, new_hm, extra=None):
        heads, w = new_hm.shape[0], new_hm.shape[-1]
        c = jnp.pad(cache.astype(BF16).transpose(2, 0, 1, 3),
                    ((0, 0), (0, 0), (0, s_pad - past), (0, w - cache.shape[-1])))
        if extra is not None:
            c = c + extra
        c = lax.dynamic_update_slice(c, new_hm.reshape(heads, b_s, t_s, w), (0, 0, past, 0))
        return c.reshape(heads, b_s * s_pad, w)

    one_lane = (jnp.arange(LANES) == HEAD_DIM).astype(BF16)
    k_all = with_cache(cache_k[0], khm_s)
    v_all = with_cache(cache_v[0], vhm_s, extra=one_lane)
    ki_all = with_cache(cache_kidx[0][:, :, None, :], kib_s[None])[0]
    lim_s = jnp.full((b_s * t_s, 1), s_all, I32)
    nk_s = jnp.full((1,), s_pad // tk_s, I32)
    xs = _dsa_attend(xs, q_s, qi_s, wi_s, lim_s, k_all, v_all, ki_all, attn_w_o[0], nk_s,
                     batch=b_s, tq=t_s, tk=tk_s, topk=topk_s)

    ffn1 = _ffn_weights(ffn_w_up[1], ffn_conv_w[1], ffn_conv_b[1], ffn_w_down[1])
    xp, conv_p1 = _conv_ffn(xp, jnp.zeros((b_p, CONV_W - 1, D_FF), F32), norm_ffn[1], ffn1, batch=b_p, seq=t_p)
    xs, conv_s1 = _conv_ffn(xs, cache_conv[1], norm_ffn[1], ffn1, batch=b_s, seq=t_s)

    dt = x_prompt.dtype
    kv_shape_p = (1, b_p, t_p, N_KV_HEADS, HEAD_DIM)
    kv_shape_s = (1, b_s, t_s, N_KV_HEADS, HEAD_DIM)
    return (xp.reshape(x_prompt.shape).astype(dt), xs.reshape(x_sample.shape).astype(x_sample.dtype),
            re_p[None].astype(dt), im_p[None].astype(dt),
            re_s[None].astype(state_ssm_re.dtype), im_s[None].astype(state_ssm_im.dtype),
            kf_p.reshape(kv_shape_p).astype(dt), vf_p.reshape(kv_shape_p).astype(dt),
            kif_p.reshape(1, b_p, t_p, IDX_DIM).astype(dt),
            kf_s.reshape(kv_shape_s).astype(dt), vf_s.reshape(kv_shape_s).astype(dt),
            kif_s.reshape(1, b_s, t_s, IDX_DIM).astype(dt),
            jnp.stack([conv_p0, conv_p1]).astype(dt), jnp.stack([conv_s0, conv_s1]).astype(dt))
```

```python
import functools
import math

import jax
import jax.numpy as jnp
from jax import lax
from jax.experimental import pallas as pl
from jax.experimental.pallas import tpu as pltpu

F32 = jnp.float32
BF16 = jnp.bfloat16
I32 = jnp.int32

D_MODEL = 1024
CHUNK = 64
SSM_GROUP = 16
SSM_GROUPS = D_MODEL // SSM_GROUP
SSM_STATE = 64
N_HEADS = 16
N_KV_HEADS = 4
HEAD_DIM = 64
ATTN_WIDTH = N_HEADS * HEAD_DIM
KV_WIDTH = N_KV_HEADS * HEAD_DIM
IDX_HEADS = 8
IDX_DIM = 64
TOPK_MAX = 256
ROT_DIM = HEAD_DIM // 4
ROPE_THETA = 500000.0
OFF_K = ATTN_WIDTH
OFF_V = OFF_K + KV_WIDTH
OFF_QI = OFF_V + KV_WIDTH
OFF_KI = OFF_QI + IDX_HEADS * IDX_DIM
OFF_WI = OFF_KI + IDX_DIM
IN_COLS = OFF_WI + IDX_HEADS
D_FF = 2816
CONV_W = 3
EPS = 1e-6
NEG = -1e30

LANES = 128
SUBLANES = 8
VMEM_LIMIT_BYTES = 56 * 1024 * 1024

S5_L = 16
S5_OCT = D_MODEL // LANES
S5_OCT_GROUPS = LANES // SSM_GROUP
S5_HALF = S5_OCT_GROUPS * SSM_STATE
S5_PW_ROWS = 24

FFN_FC = 256
FFN_NC = D_FF // FFN_FC
assert FFN_NC % 2 == 1
IN_COLS_PAD = 2176
LOG2E = math.log2(math.e)
BISECT_MAX_ITERS = 40
ATTN_BLOCK_VREGS = 32
TINY = 1.1754944e-38
COUNT_ROWS = 128


def _cparams(sem):
    return pltpu.CompilerParams(dimension_semantics=sem, vmem_limit_bytes=VMEM_LIMIT_BYTES)


def _rms(x, g):
    ms = jnp.mean(x * x, axis=-1, keepdims=True)
    return x * lax.rsqrt(ms + EPS) * g


def _norm_body(x_ref, g_ref, o_ref):
    o_ref[...] = _rms(x_ref[...], g_ref[...])


def _norm(x2d, g):
    rows = x2d.shape[0]
    tm = min(rows, 1024)
    return pl.pallas_call(
        _norm_body,
        out_shape=jax.ShapeDtypeStruct((rows, D_MODEL), F32),
        grid=(rows // tm,),
        in_specs=[pl.BlockSpec((tm, D_MODEL), lambda i: (i, 0)),
                  pl.BlockSpec((1, D_MODEL), lambda i: (0, 0))],
        out_specs=pl.BlockSpec((tm, D_MODEL), lambda i: (i, 0)),
        compiler_params=_cparams(("arbitrary",)),
        name="rmsnorm",
    )(x2d, g.astype(F32).reshape(1, D_MODEL))


def _s5_operators(lam_re, lam_im, log_dt, b_re, b_im, c_re, c_im, d_skip):
    L = S5_L
    lre = jnp.minimum(lam_re.astype(F32), -1e-4)
    lim = lam_im.astype(F32)
    dt = jnp.exp(log_dt.astype(F32))[:, None]
    n = jnp.arange(L + 1, dtype=F32)[:, None, None]
    mag = jnp.exp(n * (lre * dt)[None])
    ang = n * (lim * dt)[None]
    pr = mag * jnp.cos(ang)
    pi = mag * jnp.sin(ang)
    a_re, a_im = pr[1], pi[1]
    den = lre * lre + lim * lim
    n_re = a_re - 1.0
    f_re = (n_re * lre + a_im * lim) / den
    f_im = (a_im * lre - n_re * lim) / den
    br = b_re.astype(F32)
    bi = b_im.astype(F32)
    bb_re = f_re[..., None] * br - f_im[..., None] * bi
    bb_im = f_re[..., None] * bi + f_im[..., None] * br
    cr = c_re.astype(F32)
    ci = c_im.astype(F32)
    eye = jnp.eye(S5_OCT_GROUPS, dtype=F32)
    og = (S5_OCT, S5_OCT_GROUPS)

    tg = lambda m: jnp.moveaxis(m, 0, -1)
    crt, cit = tg(cr)[None, :, None], tg(ci)[None, :, None]
    prt, pit = tg(pr[:L].transpose(1, 0, 2))[:, None, None], tg(pi[:L].transpose(1, 0, 2))[:, None, None]
    brt, bit = tg(bb_re).transpose(1, 0, 2)[None, None], tg(bb_im).transpose(1, 0, 2)[None, None]
    kl = jnp.sum((crt * prt - cit * pit) * brt - (crt * pit + cit * prt) * bit, axis=3)
    krow = kl.transpose(0, 3, 2, 1).reshape(L, S5_OCT, LANES, SSM_GROUP)
    lane_group = jnp.arange(LANES) // SSM_GROUP
    same_group = (lane_group[:, None] == lane_group[None, :]).astype(F32)
    kbd = jnp.tile(krow, (1, 1, 1, S5_OCT_GROUPS)) * same_group
    kpad = jnp.concatenate([jnp.zeros_like(kbd[:1]), kbd], axis=0)
    k2 = []
    for dlt in range(L // 2):
        top = jnp.concatenate([kpad[2 * dlt + 1], kpad[2 * dlt + 2]], axis=-1)
        bot = jnp.concatenate([kpad[2 * dlt], kpad[2 * dlt + 1]], axis=-1)
        k2.append(jnp.concatenate([top, bot], axis=-2))
    k2 = jnp.stack(k2, axis=1).astype(BF16)

    def bd_in(m):
        m = m.reshape(og + (SSM_STATE, SSM_GROUP))
        return jnp.einsum("jgpc,gh->jgchp", m, eye).reshape(S5_OCT, LANES, S5_HALF)

    ab_re = a_re[..., None] * bb_re - a_im[..., None] * bb_im
    ab_im = a_re[..., None] * bb_im + a_im[..., None] * bb_re
    f2 = jnp.concatenate([
        jnp.concatenate([bd_in(ab_re), bd_in(ab_im)], axis=-1),
        jnp.concatenate([bd_in(bb_re), bd_in(bb_im)], axis=-1)], axis=-2).astype(BF16)

    def bd_out(m):
        m = m.reshape(og + (SSM_GROUP, SSM_STATE))
        return jnp.einsum("jgcp,gh->jgphc", m, eye).reshape(S5_OCT, S5_HALF, LANES)

    c1r = cr * a_re[:, None, :] - ci * a_im[:, None, :]
    c1i = cr * a_im[:, None, :] + ci * a_re[:, None, :]
    c2 = jnp.concatenate([
        jnp.concatenate([bd_out(cr), bd_out(c1r)], axis=-1),
        jnp.concatenate([-bd_out(ci), -bd_out(c1i)], axis=-1)], axis=-2).astype(BF16)

    def oct_lanes(p):
        return p.reshape((L + 1,) + og + (SSM_STATE,)).transpose(1, 0, 2, 3).reshape(S5_OCT, L + 1, S5_HALF)

    pw = jnp.concatenate([oct_lanes(pr), oct_lanes(pi)], axis=-1)
    pw = jnp.pad(pw, ((0, 0), (0, S5_PW_ROWS - (L + 1)), (0, 0)))
    dsk = d_skip.astype(F32).reshape(S5_OCT, 1, LANES)
    dsk2 = jnp.concatenate([dsk, dsk], axis=-1)
    return f2, k2, c2, pw, dsk2


def _s5_body(u_ref, f2_ref, k2_ref, c2_ref, pw_ref, dsk_ref, h0_ref, z_ref, ht_ref, v_scr, hp_scr, *, bpt, nck):
    L = S5_L
    H = S5_HALF
    tr = bpt * nck

    def cmul(xr, xi, n):
        pr = pw_ref[0, n:n + 1, :H]
        pi = pw_ref[0, n:n + 1, H:]
        return xr * pr - xi * pi, xr * pi + xi * pr

    ufp = [jnp.concatenate([u_ref[pl.ds(2 * s, tr, stride=L), :], u_ref[pl.ds(2 * s + 1, tr, stride=L), :]], axis=1)
           for s in range(L // 2)]
    upair = [u.astype(BF16) for u in ufp]

    vr = vi = None
    for s2 in range(L // 2):
        p = jnp.dot(upair[s2], f2_ref[0], preferred_element_type=F32)
        qr, qi = cmul(p[:, :H], p[:, H:], L - 2 - 2 * s2)
        vr = qr if vr is None else vr + qr
        vi = qi if vi is None else vi + qi
    v_scr[:, :H] = vr
    v_scr[:, H:] = vi

    alr = pw_ref[0, L:L + 1, :H]
    ali = pw_ref[0, L:L + 1, H:]
    for b in range(bpt):
        def step(k, carry, b=b):
            hr, hi = carry
            row = b * nck + k
            hp_scr[pl.ds(row, 1), :H] = hr
            hp_scr[pl.ds(row, 1), H:] = hi
            wr = v_scr[pl.ds(row, 1), :H]
            wi = v_scr[pl.ds(row, 1), H:]
            return hr * alr - hi * ali + wr, hr * ali + hi * alr + wi

        h0r = h0_ref[0, 0, b:b + 1, :H]
        h0i = h0_ref[0, 0, b:b + 1, H:]
        hr, hi = lax.fori_loop(0, nck, step, (h0r, h0i))
        ht_ref[0, 0, b:b + 1, :H] = hr
        ht_ref[0, 0, b:b + 1, H:] = hi

    hpr = hp_scr[:, :H]
    hpi = hp_scr[:, H:]
    dsk = dsk_ref[0]
    for t2 in range(L // 2):
        gr, gi = cmul(hpr, hpi, 2 * t2 + 1)
        g = jnp.concatenate([gr, gi], axis=1).astype(BF16)
        y = jnp.dot(g, c2_ref[0], preferred_element_type=F32)
        for s2 in range(t2 + 1):
            y = y + jnp.dot(upair[s2], k2_ref[0, t2 - s2], preferred_element_type=F32)
        z = jax.nn.gelu(y + dsk * ufp[t2])
        z_ref[pl.ds(2 * t2, tr, stride=L), :] = z[:, :LANES]
        z_ref[pl.ds(2 * t2 + 1, tr, stride=L), :] = z[:, LANES:]


def _s5_scan(u2d, ops, h0, *, batch, seq):
    L = S5_L
    nck = seq // L
    n_rows = batch * nck
    bpt = 1 if nck % SUBLANES == 0 else batch
    tr = bpt * nck
    n_tiles = n_rows // tr
    h0 = h0.reshape(S5_OCT, n_tiles, bpt, 2 * S5_HALF)
    tok_spec = pl.BlockSpec((tr * L, LANES), lambda j, i: (i, j))
    per_oct = lambda a: pl.BlockSpec((1,) + a.shape[1:], lambda j, i: (j,) + (0,) * (a.ndim - 1))
    state_spec = pl.BlockSpec((1, 1, bpt, 2 * S5_HALF), lambda j, i: (j, i, 0, 0))
    z, ht = pl.pallas_call(
        functools.partial(_s5_body, bpt=bpt, nck=nck),
        out_shape=(jax.ShapeDtypeStruct(u2d.shape, F32),
                   jax.ShapeDtypeStruct((S5_OCT, n_tiles, bpt, 2 * S5_HALF), F32)),
        grid=(S5_OCT, n_tiles),
        in_specs=[tok_spec] + [per_oct(a) for a in ops] + [state_spec],
        out_specs=(tok_spec, state_spec),
        scratch_shapes=[pltpu.VMEM((tr, 2 * S5_HALF), F32), pltpu.VMEM((tr, 2 * S5_HALF), F32)],
        compiler_params=_cparams(("arbitrary", "arbitrary")),
        name="s5_scan",
    )(u2d, *ops, h0)
    return z, ht


def _state_to_oct(s_re, s_im):
    b = s_re.shape[0]
    r = s_re.astype(F32).reshape(b, S5_OCT, S5_HALF).transpose(1, 0, 2)
    i = s_im.astype(F32).reshape(b, S5_OCT, S5_HALF).transpose(1, 0, 2)
    return jnp.concatenate([r, i], axis=-1)


def _oct_to_state(ht, batch):
    ht = ht.reshape(S5_OCT, batch, 2 * S5_HALF)
    r = ht[..., :S5_HALF].transpose(1, 0, 2).reshape(batch, SSM_GROUPS, SSM_STATE)
    i = ht[..., S5_HALF:].transpose(1, 0, 2).reshape(batch, SSM_GROUPS, SSM_STATE)
    return r, i


def _glu_body(z_ref, x_ref, w_ref, b_ref, o_ref):
    g = jnp.dot(z_ref[...].astype(BF16), w_ref[...], preferred_element_type=F32) + b_ref[...]
    o_ref[...] = x_ref[...] + g[:, :D_MODEL] * jax.nn.sigmoid(g[:, D_MODEL:])


def _glu_residual(z, x2d, w_glu, b_glu):
    rows = x2d.shape[0]
    tm = min(rows, 512)
    row_spec = pl.BlockSpec((tm, D_MODEL), lambda i: (i, 0))
    return pl.pallas_call(
        _glu_body,
        out_shape=jax.ShapeDtypeStruct(x2d.shape, F32),
        grid=(rows // tm,),
        in_specs=[row_spec, row_spec,
                  pl.BlockSpec((D_MODEL, 2 * D_MODEL), lambda i: (0, 0)),
                  pl.BlockSpec((1, 2 * D_MODEL), lambda i: (0, 0))],
        out_specs=row_spec,
        compiler_params=_cparams(("arbitrary",)),
        name="glu_residual",
    )(z, x2d, w_glu.astype(BF16), b_glu.astype(F32).reshape(1, 2 * D_MODEL))


def _ffn_body(x_ref, xh_ref, cs_ref, g_ref, wa_ref, wb_ref, cw_ref, cb_ref, wd_ref,
              o_ref, co_ref, xn_scr, ha0, hb0, hh0, ha1, hb1, hh1, gt0, gt1, acc_scr, *, tm, tiles_per_seq):
    i = pl.program_id(0)
    first = (i % tiles_per_seq) == 0
    g = g_ref[...]
    xn_scr[...] = _rms(x_ref[...], g).astype(BF16)
    xnh = _rms(xh_ref[...], g).astype(BF16)
    rows = lax.broadcasted_iota(I32, (tm, FFN_FC), 0)
    sets = ((ha0, hb0, hh0, gt0), (ha1, hb1, hh1, gt1))

    def up(c, s):
        ha, hb, hh, _ = sets[s]
        xn = xn_scr[...]
        ha[...] = jnp.dot(xn, wa_ref[c], preferred_element_type=F32)
        hb[...] = jnp.dot(xn, wb_ref[c], preferred_element_type=F32)
        hh[...] = jnp.dot(xnh, wa_ref[c], preferred_element_type=F32)

    def act(c, s):
        ha_ref, hb_ref, hh_ref, gt = sets[s]
        ha = ha_ref[...]
        hh = jnp.where(first, cs_ref[0, c], hh_ref[...])
        co_ref[0, c] = ha[tm - SUBLANES:, :]
        h6 = hh[6:7, :]
        h7 = hh[7:8, :]
        prev1 = jnp.where(rows == 0, h7, pltpu.roll(ha, 1, 0))
        prev2 = jnp.where(rows == 0, h6, jnp.where(rows == 1, h7, pltpu.roll(ha, 2, 0)))
        cw = cw_ref[c]
        cv = cb_ref[c] + cw[0:1, :] * prev2 + cw[1:2, :] * prev1 + cw[2:3, :] * ha
        gt[...] = (jax.nn.gelu(cv) * hb_ref[...]).astype(BF16)

    def down(c, s):
        acc_scr[...] += jnp.dot(sets[s][3][...], wd_ref[c], preferred_element_type=F32)

    acc_scr[...] = jnp.zeros(acc_scr.shape, F32)
    gt1[...] = jnp.zeros(gt1.shape, BF16)
    up(0, 0)

    def body(k, carry):
        c = 2 * k
        up(c + 1, 1)
        act(c, 0)
        down(jnp.maximum(c - 1, 0), 1)
        up(c + 2, 0)
        act(c + 1, 1)
        down(c, 0)
        return carry

    lax.fori_loop(0, (FFN_NC - 1) // 2, body, 0)
    act(FFN_NC - 1, 0)
    down(FFN_NC - 2, 1)
    down(FFN_NC - 1, 0)
    o_ref[...] = x_ref[...] + acc_scr[...]


def _ffn_weights(w_up, conv_w, conv_b, w_down):
    wab = w_up.astype(BF16).reshape(D_MODEL, 2, FFN_NC, FFN_FC).transpose(1, 2, 0, 3)
    wa, wb = wab[0], wab[1]
    cw = jnp.pad(conv_w.astype(F32), ((0, SUBLANES - CONV_W), (0, 0)))
    cw = cw.reshape(SUBLANES, FFN_NC, FFN_FC).transpose(1, 0, 2)
    cb = conv_b.astype(F32).reshape(FFN_NC, 1, FFN_FC)
    wd = w_down.astype(BF16).reshape(FFN_NC, FFN_FC, D_MODEL)
    return wa, wb, cw, cb, wd


def _conv_ffn(x2d, conv_state, g, weights, *, batch, seq):
    wa, wb, cw, cb, wd = weights
    tm = min(seq, 512)
    tiles_per_seq = seq // tm
    hb = tm // SUBLANES
    cs = jnp.pad(conv_state.astype(F32), ((0, 0), (SUBLANES - (CONV_W - 1), 0), (0, 0)))
    cs = cs.reshape(batch, SUBLANES, FFN_NC, FFN_FC).transpose(0, 2, 1, 3)
    const3 = lambda i: (0, 0, 0)
    up_set = [pltpu.VMEM((tm, FFN_FC), F32), pltpu.VMEM((tm, FFN_FC), F32), pltpu.VMEM((SUBLANES, FFN_FC), F32)]
    gate = pltpu.VMEM((tm, FFN_FC), BF16)
    scratch = [pltpu.VMEM((tm, D_MODEL), BF16)] + up_set + up_set + [gate, gate, pltpu.VMEM((tm, D_MODEL), F32)]
    out, co = pl.pallas_call(
        functools.partial(_ffn_body, tm=tm, tiles_per_seq=tiles_per_seq),
        out_shape=(jax.ShapeDtypeStruct(x2d.shape, F32),
                   jax.ShapeDtypeStruct((batch, FFN_NC, SUBLANES, FFN_FC), F32)),
        grid=(batch * tiles_per_seq,),
        in_specs=[pl.BlockSpec((tm, D_MODEL), lambda i: (i, 0)),
                  pl.BlockSpec((SUBLANES, D_MODEL), lambda i: (jnp.maximum(i * hb - 1, 0), 0)),
                  pl.BlockSpec((1, FFN_NC, SUBLANES, FFN_FC), lambda i: (i // tiles_per_seq, 0, 0, 0)),
                  pl.BlockSpec((1, D_MODEL), lambda i: (0, 0)),
                  pl.BlockSpec((FFN_NC, D_MODEL, FFN_FC), const3),
                  pl.BlockSpec((FFN_NC, D_MODEL, FFN_FC), const3),
                  pl.BlockSpec((FFN_NC, SUBLANES, FFN_FC), const3),
                  pl.BlockSpec((FFN_NC, 1, FFN_FC), const3),
                  pl.BlockSpec((FFN_NC, FFN_FC, D_MODEL), const3)],
        out_specs=(pl.BlockSpec((tm, D_MODEL), lambda i: (i, 0)),
                   pl.BlockSpec((1, FFN_NC, SUBLANES, FFN_FC), lambda i: (i // tiles_per_seq, 0, 0, 0))),
        scratch_shapes=scratch,
        compiler_params=_cparams(("arbitrary",)),
        name="conv_ffn",
    )(x2d, x2d, cs, g.astype(F32).reshape(1, D_MODEL), wa, wb, cw, cb, wd)
    new_state = co[:, :, SUBLANES - (CONV_W - 1):, :].transpose(0, 2, 1, 3).reshape(batch, CONV_W - 1, D_FF)
    return out, new_state


def _rope_tables(pos):
    half = ROT_DIM // 2
    inv = ROPE_THETA ** (-jnp.arange(half, dtype=F32) / half)
    ang = pos.astype(F32)[:, None] * inv[None, :]
    cos = jnp.cos(ang)
    sin = jnp.sin(ang)
    lane = jnp.arange(LANES) % HEAD_DIM
    idx = lane % half
    cc = jnp.where(lane[None] < ROT_DIM, cos[:, idx], 1.0)
    s1 = jnp.where((lane[None] >= half) & (lane[None] < ROT_DIM), sin[:, idx], 0.0)
    s2 = jnp.where(lane[None] < half, -sin[:, idx], 0.0)
    return cc.astype(F32), s1.astype(F32), s2.astype(F32)


def _proj_body(x_ref, g_ref, w_ref, qn_ref, kn_ref, cc_ref, s1_ref, s2_ref, bd_ref,
               q_ref, khm_ref, vhm_ref, kf_ref, vf_ref, qi_ref, kib_ref, kif_ref, wi_ref):
    xn = _rms(x_ref[...], g_ref[...]).astype(BF16)
    proj = jnp.dot(xn, w_ref[...], preferred_element_type=F32)
    cc = cc_ref[...]
    s1 = s1_ref[...]
    s2 = s2_ref[...]
    bd = bd_ref[...]

    def rope(y):
        return y * cc + pltpu.roll(y, ROT_DIM // 2, 1) * s1 + pltpu.roll(y, LANES - ROT_DIM // 2, 1) * s2

    def head_norm(y, gain):
        ms = jnp.dot((y * y).astype(BF16), bd, preferred_element_type=F32)
        return y * lax.rsqrt(ms + EPS) * gain

    qn = qn_ref[...]
    kn = kn_ref[...]
    scale = HEAD_DIM ** -0.5 * LOG2E
    one_col = jnp.where(lax.broadcasted_iota(I32, (x_ref.shape[0], HEAD_DIM), 1) == 0, 1.0, 0.0)
    for c in range(ATTN_WIDTH // LANES):
        y = proj[:, c * LANES:(c + 1) * LANES]
        q_ref[:, c * LANES:(c + 1) * LANES] = (rope(head_norm(y, qn)) * scale).astype(BF16)
    for c in range(KV_WIDTH // LANES):
        y = proj[:, OFF_K + c * LANES:OFF_K + (c + 1) * LANES]
        k = rope(head_norm(y, kn))
        kf_ref[:, c * LANES:(c + 1) * LANES] = k
        khm_ref[2 * c] = k[:, :HEAD_DIM].astype(BF16)
        khm_ref[2 * c + 1] = k[:, HEAD_DIM:].astype(BF16)
        v = proj[:, OFF_V + c * LANES:OFF_V + (c + 1) * LANES]
        vf_ref[:, c * LANES:(c + 1) * LANES] = v
        vhm_ref[2 * c] = jnp.concatenate([v[:, :HEAD_DIM], one_col], axis=1).astype(BF16)
        vhm_ref[2 * c + 1] = jnp.concatenate([v[:, HEAD_DIM:], one_col], axis=1).astype(BF16)
    iscale = IDX_DIM ** -0.5
    for c in range(IDX_HEADS * IDX_DIM // LANES):
        y = proj[:, OFF_QI + c * LANES:OFF_QI + (c + 1) * LANES]
        qi_ref[:, c * LANES:(c + 1) * LANES] = (rope(y) * iscale).astype(BF16)
    tail = proj[:, OFF_KI:OFF_KI + LANES]
    ki = rope(tail)[:, :IDX_DIM]
    kif_ref[...] = ki
    kib_ref[...] = ki.astype(BF16)
    wi_ref[...] = tail[:, IDX_DIM:IDX_DIM + IDX_HEADS] * (IDX_HEADS ** -0.5)


def _dsa_project(x2d, g, w_in, q_norm, k_norm, pos):
    rows = x2d.shape[0]
    tm = min(rows, 256)
    w = jnp.pad(w_in.astype(BF16), ((0, 0), (0, IN_COLS_PAD - IN_COLS)))
    if pos.shape[0] < tm:
        pos = jnp.tile(pos, tm // pos.shape[0])
    ntab = pos.shape[0] // tm
    cc, s1, s2 = _rope_tables(pos)
    lane = jnp.arange(LANES)
    bd = jnp.where((lane[:, None] // HEAD_DIM) == (lane[None, :] // HEAD_DIM), 1.0 / HEAD_DIM, 0.0).astype(BF16)
    qn = jnp.tile(q_norm.astype(F32), LANES // HEAD_DIM).reshape(1, LANES)
    kn = jnp.tile(k_norm.astype(F32), LANES // HEAD_DIM).reshape(1, LANES)
    row_spec = lambda width: pl.BlockSpec((tm, width), lambda i: (i, 0))
    const = lambda shape: pl.BlockSpec(shape, lambda i: (0,) * len(shape))
    hm_spec = pl.BlockSpec((N_KV_HEADS, tm, HEAD_DIM), lambda i: (0, i, 0))
    vhm_spec = pl.BlockSpec((N_KV_HEADS, tm, LANES), lambda i: (0, i, 0))
    tab_spec = pl.BlockSpec((tm, LANES), lambda i: (i % ntab, 0))
    return pl.pallas_call(
        _proj_body,
        out_shape=(jax.ShapeDtypeStruct((rows, ATTN_WIDTH), BF16),
                   jax.ShapeDtypeStruct((N_KV_HEADS, rows, HEAD_DIM), BF16),
                   jax.ShapeDtypeStruct((N_KV_HEADS, rows, LANES), BF16),
                   jax.ShapeDtypeStruct((rows, KV_WIDTH), F32),
                   jax.ShapeDtypeStruct((rows, KV_WIDTH), F32),
                   jax.ShapeDtypeStruct((rows, IDX_HEADS * IDX_DIM), BF16),
                   jax.ShapeDtypeStruct((rows, IDX_DIM), BF16),
                   jax.ShapeDtypeStruct((rows, IDX_DIM), F32),
                   jax.ShapeDtypeStruct((rows, IDX_HEADS), F32)),
        grid=(rows // tm,),
        in_specs=[row_spec(D_MODEL), const((1, D_MODEL)), const((D_MODEL, IN_COLS_PAD)),
                  const((1, LANES)), const((1, LANES)),
                  tab_spec, tab_spec, tab_spec, const((LANES, LANES))],
        out_specs=(row_spec(ATTN_WIDTH), hm_spec, vhm_spec, row_spec(KV_WIDTH), row_spec(KV_WIDTH),
                   row_spec(IDX_HEADS * IDX_DIM), row_spec(IDX_DIM), row_spec(IDX_DIM), row_spec(IDX_HEADS)),
        compiler_params=_cparams(("arbitrary",)),
        name="dsa_project",
    )(x2d, g.astype(F32).reshape(1, D_MODEL), w, qn, kn, cc, s1, s2, bd)


def _attn_body(nk_ref, q_ref, qi_ref, wi_ref, lim_ref, x_ref, k_ref, v_ref, ki_ref, wo_ref,
               o_ref, sc_scr, bias_scr, qg_scr, s_scr, p_scr, m_scr, acc_scr, *, tq, tk, topk):
    i = pl.program_id(1)
    nb = nk_ref[i]
    ncol = tk // LANES
    kf = float(topk)
    qi = qi_ref[...].astype(F32)
    wi = wi_ref[...]
    lim = lim_ref[...]
    qih = [qi[:, h * IDX_DIM:(h + 1) * IDX_DIM].astype(BF16) for h in range(IDX_HEADS)]
    wih = [wi[:, h:h + 1] for h in range(IDX_HEADS)]
    nt = (((1,), (1,)), ((), ()))
    rb = min(tq, max(SUBLANES, ATTN_BLOCK_VREGS // ncol * SUBLANES))
    hr = min(tq, COUNT_ROWS)
    lane_tk = lax.broadcasted_iota(I32, (tq, tk), 1)

    def fold(x, op):
        part = x[:, :LANES]
        for c in range(1, ncol):
            part = op(part, x[:, c * LANES:(c + 1) * LANES])
        return part

    def rows_all(x, op):
        return jnp.broadcast_to(op(x, axis=1, keepdims=True), x.shape)

    def tile_cols(x):
        return jnp.concatenate([x] * ncol, axis=1)

    def score_blk(kb, carry):
        mn, mx, cpos, cnn = carry
        off = pl.multiple_of(kb * tk, tk)
        kib = ki_ref[pl.ds(off, tk), :]
        sc = jnp.zeros((tq, tk), F32)
        for h in range(IDX_HEADS):
            lg = lax.dot_general(qih[h], kib, nt, preferred_element_type=F32)
            sc = sc + jnp.maximum(lg, 0.0) * wih[h]
        allowed = lane_tk + off < lim
        scm = jnp.where(allowed, sc, NEG)
        sc_scr[kb] = scm
        mn = jnp.minimum(mn, fold(jnp.where(allowed, sc, -NEG), jnp.minimum))
        mx = jnp.maximum(mx, fold(scm, jnp.maximum))
        cpos = cpos + fold(jnp.where(scm > 0.0, 1.0, 0.0), jnp.add)
        cnn = cnn + fold(jnp.where(scm >= 0.0, 1.0, 0.0), jnp.add)
        return mn, mx, cpos, cnn

    zeros = jnp.zeros((tq, LANES), F32)
    mn, mx, cpos, cnn = lax.fori_loop(
        0, nb, score_blk, (jnp.full((tq, LANES), -NEG, F32), jnp.full((tq, LANES), NEG, F32), zeros, zeros))
    smin = rows_all(mn, jnp.min)
    smax = rows_all(mx, jnp.max)
    cpos = rows_all(cpos, jnp.sum)
    cnn = rows_all(cnn, jnp.sum)

    def count_ge(x):
        parts = []
        for r0 in range(0, tq, hr):
            xt = tile_cols(x[r0:r0 + hr])

            def body(kb, acc, r0=r0, xt=xt):
                return acc + fold(jnp.where(sc_scr[kb, r0:r0 + hr, :] >= xt, 1.0, 0.0), jnp.add)

            parts.append(lax.fori_loop(0, nb, body, jnp.zeros((hr, LANES), F32)))
        return rows_all(jnp.concatenate(parts, axis=0), jnp.sum)

    above_zero = cpos >= kf
    at_zero = jnp.logical_and(jnp.logical_not(above_zero), cnn >= kf)
    n_allowed = jnp.broadcast_to(jnp.minimum(lim, nb * tk).astype(F32), (tq, LANES))
    lo0 = jnp.where(above_zero, TINY, jnp.where(at_zero, 0.0, smin))
    hi0 = jnp.where(above_zero, smax + (jnp.abs(smax) + 1e-30) * 1e-6, jnp.where(at_zero, TINY, 0.0))
    cnt0 = jnp.where(above_zero, cpos, jnp.where(at_zero, cnn, n_allowed))
    frozen = at_zero

    def any_over(cnt_lo):
        return jnp.max(jnp.where(jnp.logical_and(cnt_lo > kf, jnp.logical_not(frozen)), 1.0, 0.0)) > 0.5

    def bis_step(lo, hi, cnt_lo):
        mid = 0.5 * (lo + hi)
        cnt = count_ge(mid)
        up = jnp.logical_and(cnt >= kf, jnp.logical_not(frozen))
        dn = jnp.logical_and(cnt < kf, jnp.logical_not(frozen))
        return jnp.where(up, mid, lo), jnp.where(dn, mid, hi), jnp.where(up, cnt, cnt_lo)

    def bis_cond(c):
        return c[4]

    def bis_body(c):
        it, lo, hi, cnt_lo, _ = c
        lo, hi, cnt_lo = bis_step(*bis_step(lo, hi, cnt_lo))
        return it + 2, lo, hi, cnt_lo, jnp.logical_and(any_over(cnt_lo), it + 2 < BISECT_MAX_ITERS)

    _, lo, hi, cnt_lo, _ = lax.while_loop(bis_cond, bis_body, (jnp.int32(0), lo0, hi0, cnt0, any_over(cnt0)))
    ties = jnp.max(jnp.where(cnt_lo > kf, 1.0, 0.0)) > 0.5
    lo_t = tile_cols(lo)
    hi_t = tile_cols(hi)

    @pl.when(jnp.logical_not(ties))
    def _():
        def body(kb, carry):
            s = sc_scr[kb]
            bias_scr[kb] = jnp.where((s >= lo_t) & (s > 0.5 * NEG), 0.0, NEG)
            return carry

        lax.fori_loop(0, nb, body, 0)

    @pl.when(ties)
    def _():
        need = kf - count_ge(hi)
        tri = (lax.broadcasted_iota(I32, (LANES, LANES), 0)
               <= lax.broadcasted_iota(I32, (LANES, LANES), 1)).astype(BF16)

        def body(kb, seen):
            s = sc_scr[kb]
            cols = []
            for c in range(ncol):
                sc = s[:, c * LANES:(c + 1) * LANES]
                tie = (sc >= lo) & (sc < hi)
                tief = jnp.where(tie, 1.0, 0.0)
                incl = jnp.dot(tief.astype(BF16), tri, preferred_element_type=F32)
                take = tie & ((seen + incl - tief) < need)
                sel = ((sc >= hi) | take) & (sc > 0.5 * NEG)
                cols.append(jnp.where(sel, 0.0, NEG))
                seen = seen + incl[:, LANES - 1:]
            bias_scr[kb] = jnp.concatenate(cols, axis=1)
            return seen

        lax.fori_loop(0, nb, body, zeros)

    q = q_ref[...].astype(F32)
    gq = N_HEADS // N_KV_HEADS
    for g in range(N_KV_HEADS):
        for r in range(gq):
            h = g * gq + r
            qg_scr[g, r * tq:(r + 1) * tq, :] = q[:, h * HEAD_DIM:(h + 1) * HEAD_DIM].astype(BF16)
    m_scr[...] = jnp.full(m_scr.shape, NEG, F32)
    acc_scr[...] = jnp.zeros(acc_scr.shape, F32)

    def attn_blk(kb, carry):
        off = pl.multiple_of(kb * tk, tk)
        for g in range(N_KV_HEADS):
            kblk = k_ref[g, pl.ds(off, tk), :]
            vblk = v_ref[g, pl.ds(off, tk), :]
            s_scr[g] = lax.dot_general(qg_scr[g], kblk, nt, preferred_element_type=F32)
            for r0 in range(0, gq * tq, rb):
                rows = slice(r0, r0 + rb)
                s = s_scr[g, rows, :] + bias_scr[kb, r0 % tq:r0 % tq + rb, :]
                m_old = m_scr[g, rows, :]
                m_new = jnp.maximum(m_old, jnp.max(s, axis=1, keepdims=True))
                p_scr[g, rows, :] = jnp.exp2(s - jnp.concatenate([m_new] * ncol, axis=1)).astype(BF16)
                acc_scr[g, rows, :] = jnp.exp2(m_old - m_new) * acc_scr[g, rows, :]
                m_scr[g, rows, :] = m_new
            acc_scr[g] += jnp.dot(p_scr[g], vblk, preferred_element_type=F32)
        return carry

    lax.fori_loop(0, nb, attn_blk, 0)
    outs = []
    for g in range(N_KV_HEADS):
        for r in range(gq):
            acc = acc_scr[g, r * tq:(r + 1) * tq, :]
            outs.append(acc[:, :HEAD_DIM] / acc[:, HEAD_DIM:HEAD_DIM + 1])
    o = jnp.concatenate(outs, axis=1).astype(BF16)
    o_ref[...] = x_ref[...] + jnp.dot(o, wo_ref[...], preferred_element_type=F32)


def _dsa_attend(x2d, q, qi, wi, limit, k_hm, v_hm, ki, w_o, nkeys, *, batch, tq, tk, topk):
    rows = x2d.shape[0]
    per_batch = rows // batch
    nqt = per_batch // tq
    s_len = ki.shape[0] // batch
    nblk = s_len // tk
    gq = N_HEADS // N_KV_HEADS
    row_spec = lambda width: pl.BlockSpec((tq, width), lambda b, i, nk: (b * nqt + i, 0))
    once = pl.Buffered(1)
    k_spec = pl.BlockSpec((N_KV_HEADS, s_len, HEAD_DIM), lambda b, i, nk: (0, b, 0), pipeline_mode=once)
    v_spec = pl.BlockSpec((N_KV_HEADS, s_len, LANES), lambda b, i, nk: (0, b, 0), pipeline_mode=once)
    grid_spec = pltpu.PrefetchScalarGridSpec(
        num_scalar_prefetch=1,
        grid=(batch, nqt),
        in_specs=[row_spec(ATTN_WIDTH), row_spec(IDX_HEADS * IDX_DIM), row_spec(IDX_HEADS), row_spec(1),
                  row_spec(D_MODEL), k_spec, v_spec,
                  pl.BlockSpec((s_len, IDX_DIM), lambda b, i, nk: (b, 0), pipeline_mode=once),
                  pl.BlockSpec((ATTN_WIDTH, D_MODEL), lambda b, i, nk: (0, 0), pipeline_mode=once)],
        out_specs=row_spec(D_MODEL),
        scratch_shapes=[pltpu.VMEM((nblk, tq, tk), F32), pltpu.VMEM((nblk, tq, tk), F32),
                        pltpu.VMEM((N_KV_HEADS, gq * tq, HEAD_DIM), BF16),
                        pltpu.VMEM((N_KV_HEADS, gq * tq, tk), F32), pltpu.VMEM((N_KV_HEADS, gq * tq, tk), BF16),
                        pltpu.VMEM((N_KV_HEADS, gq * tq, LANES), F32), pltpu.VMEM((N_KV_HEADS, gq * tq, LANES), F32)],
    )
    return pl.pallas_call(
        functools.partial(_attn_body, tq=tq, tk=tk, topk=topk),
        out_shape=jax.ShapeDtypeStruct(x2d.shape, F32),
        grid_spec=grid_spec,
        compiler_params=_cparams(("arbitrary", "arbitrary")),
        name="dsa_attend",
    )(nkeys, q, qi, wi, limit, x2d, k_hm, v_hm, ki, w_o.astype(BF16))


def _round_up(x, m):
    return (x + m - 1) // m * m


def kernel(x_prompt, x_sample, state_ssm_re, state_ssm_im, cache_k, cache_v, cache_kidx, cache_conv,
           norm_mix, norm_ffn, ssm_lambda_re, ssm_lambda_im, ssm_log_dt, ssm_b_re, ssm_b_im,
           ssm_c_re, ssm_c_im, ssm_d, ssm_w_glu, ssm_b_glu, attn_w_in, attn_q_norm, attn_k_norm,
           attn_w_o, ffn_w_up, ffn_conv_w, ffn_conv_b, ffn_w_down):
    b_p, t_p, _ = x_prompt.shape
    b_s, t_s, _ = x_sample.shape
    past = cache_k.shape[2]
    xp = x_prompt.astype(F32).reshape(b_p * t_p, D_MODEL)
    xs = x_sample.astype(F32).reshape(b_s * t_s, D_MODEL)

    ops = _s5_operators(ssm_lambda_re[0], ssm_lambda_im[0], ssm_log_dt[0], ssm_b_re[0], ssm_b_im[0],
                        ssm_c_re[0], ssm_c_im[0], ssm_d[0])
    zero_state = jnp.zeros((S5_OCT, b_p, 2 * S5_HALF), F32)
    zp, htp = _s5_scan(_norm(xp, norm_mix[0]), ops, zero_state, batch=b_p, seq=t_p)
    zs, hts = _s5_scan(_norm(xs, norm_mix[0]), ops, _state_to_oct(state_ssm_re[0], state_ssm_im[0]),
                       batch=b_s, seq=t_s)
    xp = _glu_residual(zp, xp, ssm_w_glu[0], ssm_b_glu[0])
    xs = _glu_residual(zs, xs, ssm_w_glu[0], ssm_b_glu[0])
    re_p, im_p = _oct_to_state(htp, b_p)
    re_s, im_s = _oct_to_state(hts, b_s)

    ffn0 = _ffn_weights(ffn_w_up[0], ffn_conv_w[0], ffn_conv_b[0], ffn_w_down[0])
    xp, conv_p0 = _conv_ffn(xp, jnp.zeros((b_p, CONV_W - 1, D_FF), F32), norm_ffn[0], ffn0, batch=b_p, seq=t_p)
    xs, conv_s0 = _conv_ffn(xs, cache_conv[0], norm_ffn[0], ffn0, batch=b_s, seq=t_s)

    pos_p = jnp.arange(t_p, dtype=I32)
    pos_s = past + jnp.arange(t_s, dtype=I32)
    topk_p = min(TOPK_MAX, t_p // 4)
    topk_s = min(TOPK_MAX, (past + t_s) // 4)

    (q_p, khm_p, vhm_p, kf_p, vf_p, qi_p, kib_p, kif_p, wi_p) = _dsa_project(
        xp, norm_mix[1], attn_w_in[0], attn_q_norm[0], attn_k_norm[0], pos_p)
    tq_p = min(t_p, 256)
    tk_p = min(t_p, 512)
    lim_p = jnp.tile((pos_p // CHUNK + 1) * CHUNK, b_p).reshape(b_p * t_p, 1)
    q_end = (jnp.arange(t_p // tq_p, dtype=I32) + 1) * tq_p
    nk_p = jnp.minimum((q_end + tk_p - 1) // tk_p, t_p // tk_p).astype(I32)
    xp = _dsa_attend(xp, q_p, qi_p, wi_p, lim_p, khm_p, vhm_p, kib_p, attn_w_o[0], nk_p,
                     batch=b_p, tq=tq_p, tk=tk_p, topk=topk_p)

    (q_s, khm_s, vhm_s, kf_s, vf_s, qi_s, kib_s, kif_s, wi_s) = _dsa_project(
        xs, norm_mix[1], attn_w_in[0], attn_q_norm[0], attn_k_norm[0], pos_s)
    tk_s = 11 * LANES
    s_all = past + t_s
    s_pad = _round_up(s_all, tk_s)

    def with_cache(cache, new_hm, extra=None):
        heads, w = new_hm.shape[0], new_hm.shape[-1]
        c = jnp.pad(cache.astype(BF16).transpose(2, 0, 1, 3),
                    ((0, 0), (0, 0), (0, s_pad - past), (0, w - cache.shape[-1])))
        if extra is not None:
            c = c + extra
        c = lax.dynamic_update_slice(c, new_hm.reshape(heads, b_s, t_s, w), (0, 0, past, 0))
        return c.reshape(heads, b_s * s_pad, w)

    one_lane = (jnp.arange(LANES) == HEAD_DIM).astype(BF16)
    k_all = with_cache(cache_k[0], khm_s)
    v_all = with_cache(cache_v[0], vhm_s, extra=one_lane)
    ki_all = with_cache(cache_kidx[0][:, :, None, :], kib_s[None])[0]
    lim_s = jnp.full((b_s * t_s, 1), s_all, I32)
    nk_s = jnp.full((1,), s_pad // tk_s, I32)
    xs = _dsa_attend(xs, q_s, qi_s, wi_s, lim_s, k_all, v_all, ki_all, attn_w_o[0], nk_s,
                     batch=b_s, tq=t_s, tk=tk_s, topk=topk_s)

    ffn1 = _ffn_weights(ffn_w_up[1], ffn_conv_w[1], ffn_conv_b[1], ffn_w_down[1])
    xp, conv_p1 = _conv_ffn(xp, jnp.zeros((b_p, CONV_W - 1, D_FF), F32), norm_ffn[1], ffn1, batch=b_p, seq=t_p)
    xs, conv_s1 = _conv_ffn(xs, cache_conv[1], norm_ffn[1], ffn1, batch=b_s, seq=t_s)

    dt = x_prompt.dtype
    kv_shape_p = (1, b_p, t_p, N_KV_HEADS, HEAD_DIM)
    kv_shape_s = (1, b_s, t_s, N_KV_HEADS, HEAD_DIM)
    return (xp.reshape(x_prompt.shape).astype(dt), xs.reshape(x_sample.shape).astype(x_sample.dtype),
            re_p[None].astype(dt), im_p[None].astype(dt),
            re_s[None].astype(state_ssm_re.dtype), im_s[None].astype(state_ssm_im.dtype),
            kf_p.reshape(kv_shape_p).astype(dt), vf_p.reshape(kv_shape_p).astype(dt),
            kif_p.reshape(1, b_p, t_p, IDX_DIM).astype(dt),
            kf_s.reshape(kv_shape_s).astype(dt), vf_s.reshape(kv_shape_s).astype(dt),
            kif_s.reshape(1, b_s, t_s, IDX_DIM).astype(dt),
            jnp.stack([conv_p0, conv_p1]).astype(dt), jnp.stack([conv_s0, conv_s1]).astype(dt))
```

```python
import functools
import math

import jax
import jax.numpy as jnp
from jax import lax
from jax.experimental import pallas as pl
from jax.experimental.pallas import tpu as pltpu

F32 = jnp.float32
BF16 = jnp.bfloat16
I32 = jnp.int32

D_MODEL = 1024
CHUNK = 64
SSM_GROUP = 16
SSM_GROUPS = D_MODEL // SSM_GROUP
SSM_STATE = 64
N_HEADS = 16
N_KV_HEADS = 4
HEAD_DIM = 64
ATTN_WIDTH = N_HEADS * HEAD_DIM
KV_WIDTH = N_KV_HEADS * HEAD_DIM
IDX_HEADS = 8
IDX_DIM = 64
TOPK_MAX = 256
ROT_DIM = HEAD_DIM // 4
ROPE_THETA = 500000.0
OFF_K = ATTN_WIDTH
OFF_V = OFF_K + KV_WIDTH
OFF_QI = OFF_V + KV_WIDTH
OFF_KI = OFF_QI + IDX_HEADS * IDX_DIM
OFF_WI = OFF_KI + IDX_DIM
IN_COLS = OFF_WI + IDX_HEADS
D_FF = 2816
CONV_W = 3
EPS = 1e-6
NEG = -1e30

LANES = 128
SUBLANES = 8
VMEM_LIMIT_BYTES = 56 * 1024 * 1024

S5_L = 16
S5_OCT = D_MODEL // LANES
S5_OCT_GROUPS = LANES // SSM_GROUP
S5_HALF = S5_OCT_GROUPS * SSM_STATE
S5_PW_ROWS = 24
FFN_TM = 512
FFN_FC = 256
FFN_NC = D_FF // FFN_FC
assert FFN_NC % 2 == 1
IN_COLS_PAD = 2176
LOG2E = math.log2(math.e)
BISECT_MAX_ITERS = 40
ATTN_BLOCK_VREGS = 32
TINY = 1.1754944e-38
COUNT_ROWS = 128


def _cparams(sem):
    return pltpu.CompilerParams(dimension_semantics=sem, vmem_limit_bytes=VMEM_LIMIT_BYTES)


def _rms(x, g):
    ms = jnp.mean(x * x, axis=-1, keepdims=True)
    return x * lax.rsqrt(ms + EPS) * g


def _norm_body(x_ref, g_ref, o_ref):
    o_ref[...] = _rms(x_ref[...], g_ref[...])


def _norm(x2d, g):
    rows = x2d.shape[0]
    tm = min(rows, 1024)
    return pl.pallas_call(
        _norm_body,
        out_shape=jax.ShapeDtypeStruct((rows, D_MODEL), F32),
        grid=(rows // tm,),
        in_specs=[pl.BlockSpec((tm, D_MODEL), lambda i: (i, 0)),
                  pl.BlockSpec((1, D_MODEL), lambda i: (0, 0))],
        out_specs=pl.BlockSpec((tm, D_MODEL), lambda i: (i, 0)),
        compiler_params=_cparams(("arbitrary",)),
        name="rmsnorm",
    )(x2d, g.astype(F32).reshape(1, D_MODEL))


def _s5_operators(lam_re, lam_im, log_dt, b_re, b_im, c_re, c_im, d_skip):
    L = S5_L
    lre = jnp.minimum(lam_re.astype(F32), -1e-4)
    lim = lam_im.astype(F32)
    dt = jnp.exp(log_dt.astype(F32))[:, None]
    n = jnp.arange(L + 1, dtype=F32)[:, None, None]
    mag = jnp.exp(n * (lre * dt)[None])
    ang = n * (lim * dt)[None]
    pr = mag * jnp.cos(ang)
    pi = mag * jnp.sin(ang)
    a_re, a_im = pr[1], pi[1]
    den = lre * lre + lim * lim
    n_re = a_re - 1.0
    f_re = (n_re * lre + a_im * lim) / den
    f_im = (a_im * lre - n_re * lim) / den
    br = b_re.astype(F32)
    bi = b_im.astype(F32)
    bb_re = f_re[..., None] * br - f_im[..., None] * bi
    bb_im = f_re[..., None] * bi + f_im[..., None] * br
    cr = c_re.astype(F32)
    ci = c_im.astype(F32)
    eye = jnp.eye(S5_OCT_GROUPS, dtype=F32)
    og = (S5_OCT, S5_OCT_GROUPS)

    tg = lambda m: jnp.moveaxis(m, 0, -1)
    crt, cit = tg(cr)[None, :, None], tg(ci)[None, :, None]
    prt, pit = tg(pr[:L].transpose(1, 0, 2))[:, None, None], tg(pi[:L].transpose(1, 0, 2))[:, None, None]
    brt, bit = tg(bb_re).transpose(1, 0, 2)[None, None], tg(bb_im).transpose(1, 0, 2)[None, None]
    kl = jnp.sum((crt * prt - cit * pit) * brt - (crt * pit + cit * prt) * bit, axis=3)
    krow = kl.transpose(0, 3, 2, 1).reshape(L, S5_OCT, LANES, SSM_GROUP)
    lane_group = jnp.arange(LANES) // SSM_GROUP
    same_group = (lane_group[:, None] == lane_group[None, :]).astype(F32)
    kbd = jnp.tile(krow, (1, 1, 1, S5_OCT_GROUPS)) * same_group
    kpad = jnp.concatenate([jnp.zeros_like(kbd[:1]), kbd], axis=0)
    k2 = []
    for dlt in range(L // 2):
        top = jnp.concatenate([kpad[2 * dlt + 1], kpad[2 * dlt + 2]], axis=-1)
        bot = jnp.concatenate([kpad[2 * dlt], kpad[2 * dlt + 1]], axis=-1)
        k2.append(jnp.concatenate([top, bot], axis=-2))
    k2 = jnp.stack(k2, axis=1).astype(BF16)

    def bd_in(m):
        m = m.reshape(og + (SSM_STATE, SSM_GROUP))
        return jnp.einsum("jgpc,gh->jgchp", m, eye).reshape(S5_OCT, LANES, S5_HALF)

    ab_re = a_re[..., None] * bb_re - a_im[..., None] * bb_im
    ab_im = a_re[..., None] * bb_im + a_im[..., None] * bb_re
    f2 = jnp.concatenate([
        jnp.concatenate([bd_in(ab_re), bd_in(ab_im)], axis=-1),
        jnp.concatenate([bd_in(bb_re), bd_in(bb_im)], axis=-1)], axis=-2).astype(BF16)

    def bd_out(m):
        m = m.reshape(og + (SSM_GROUP, SSM_STATE))
        return jnp.einsum("jgcp,gh->jgphc", m, eye).reshape(S5_OCT, S5_HALF, LANES)

    c1r = cr * a_re[:, None, :] - ci * a_im[:, None, :]
    c1i = cr * a_im[:, None, :] + ci * a_re[:, None, :]
    c2 = jnp.concatenate([
        jnp.concatenate([bd_out(cr), bd_out(c1r)], axis=-1),
        jnp.concatenate([-bd_out(ci), -bd_out(c1i)], axis=-1)], axis=-2).astype(BF16)

    def oct_lanes(p):
        return p.reshape((L + 1,) + og + (SSM_STATE,)).transpose(1, 0, 2, 3).reshape(S5_OCT, L + 1, S5_HALF)

    pw = jnp.concatenate([oct_lanes(pr), oct_lanes(pi)], axis=-1)
    pw = jnp.pad(pw, ((0, 0), (0, S5_PW_ROWS - (L + 1)), (0, 0)))
    dsk = d_skip.astype(F32).reshape(S5_OCT, 1, LANES)
    dsk2 = jnp.concatenate([dsk, dsk], axis=-1)
    return f2, k2, c2, pw, dsk2


def _s5_body(u_ref, f2_ref, k2_ref, c2_ref, pw_ref, dsk_ref, h0_ref, z_ref, ht_ref, v_scr, hp_scr, *, bpt, nck):
    L = S5_L
    H = S5_HALF
    tr = bpt * nck

    def cmul(xr, xi, n):
        pr = pw_ref[0, n:n + 1, :H]
        pi = pw_ref[0, n:n + 1, H:]
        return xr * pr - xi * pi, xr * pi + xi * pr

    ufp = [jnp.concatenate([u_ref[pl.ds(2 * s, tr, stride=L), :], u_ref[pl.ds(2 * s + 1, tr, stride=L), :]], axis=1)
           for s in range(L // 2)]
    upair = [u.astype(BF16) for u in ufp]

    vr = vi = None
    for s2 in range(L // 2):
        p = jnp.dot(upair[s2], f2_ref[0], preferred_element_type=F32)
        qr, qi = cmul(p[:, :H], p[:, H:], L - 2 - 2 * s2)
        vr = qr if vr is None else vr + qr
        vi = qi if vi is None else vi + qi
    v_scr[:, :H] = vr
    v_scr[:, H:] = vi

    alr = pw_ref[0, L:L + 1, :H]
    ali = pw_ref[0, L:L + 1, H:]
    for b in range(bpt):
        def step(k, carry, b=b):
            hr, hi = carry
            row = b * nck + k
            hp_scr[pl.ds(row, 1), :H] = hr
            hp_scr[pl.ds(row, 1), H:] = hi
            wr = v_scr[pl.ds(row, 1), :H]
            wi = v_scr[pl.ds(row, 1), H:]
            return hr * alr - hi * ali + wr, hr * ali + hi * alr + wi

        h0r = h0_ref[0, 0, b:b + 1, :H]
        h0i = h0_ref[0, 0, b:b + 1, H:]
        hr, hi = lax.fori_loop(0, nck, step, (h0r, h0i))
        ht_ref[0, 0, b:b + 1, :H] = hr
        ht_ref[0, 0, b:b + 1, H:] = hi

    hpr = hp_scr[:, :H]
    hpi = hp_scr[:, H:]
    dsk = dsk_ref[0]
    for t2 in range(L // 2):
        gr, gi = cmul(hpr, hpi, 2 * t2 + 1)
        g = jnp.concatenate([gr, gi], axis=1).astype(BF16)
        y = jnp.dot(g, c2_ref[0], preferred_element_type=F32)
        for s2 in range(t2 + 1):
            y = y + jnp.dot(upair[s2], k2_ref[0, t2 - s2], preferred_element_type=F32)
        z = jax.nn.gelu(y + dsk * ufp[t2])
        z_ref[pl.ds(2 * t2, tr, stride=L), :] = z[:, :LANES]
        z_ref[pl.ds(2 * t2 + 1, tr, stride=L), :] = z[:, LANES:]


def _s5_scan(u2d, ops, h0, *, batch, seq):
    L = S5_L
    nck = seq // L
    n_rows = batch * nck
    bpt = 1 if nck % SUBLANES == 0 else batch
    tr = bpt * nck
    n_tiles = n_rows // tr
    h0 = h0.reshape(S5_OCT, n_tiles, bpt, 2 * S5_HALF)
    tok_spec = pl.BlockSpec((tr * L, LANES), lambda j, i: (i, j))
    per_oct = lambda a: pl.BlockSpec((1,) + a.shape[1:], lambda j, i: (j,) + (0,) * (a.ndim - 1))
    state_spec = pl.BlockSpec((1, 1, bpt, 2 * S5_HALF), lambda j, i: (j, i, 0, 0))
    z, ht = pl.pallas_call(
        functools.partial(_s5_body, bpt=bpt, nck=nck),
        out_shape=(jax.ShapeDtypeStruct(u2d.shape, F32),
                   jax.ShapeDtypeStruct((S5_OCT, n_tiles, bpt, 2 * S5_HALF), F32)),
        grid=(S5_OCT, n_tiles),
        in_specs=[tok_spec] + [per_oct(a) for a in ops] + [state_spec],
        out_specs=(tok_spec, state_spec),
        scratch_shapes=[pltpu.VMEM((tr, 2 * S5_HALF), F32), pltpu.VMEM((tr, 2 * S5_HALF), F32)],
        compiler_params=_cparams(("arbitrary", "arbitrary")),
        name="s5_scan",
    )(u2d, *ops, h0)
    return z, ht


def _state_to_oct(s_re, s_im):
    b = s_re.shape[0]
    r = s_re.astype(F32).reshape(b, S5_OCT, S5_HALF).transpose(1, 0, 2)
    i = s_im.astype(F32).reshape(b, S5_OCT, S5_HALF).transpose(1, 0, 2)
    return jnp.concatenate([r, i], axis=-1)


def _oct_to_state(ht, batch):
    ht = ht.reshape(S5_OCT, batch, 2 * S5_HALF)
    r = ht[..., :S5_HALF].transpose(1, 0, 2).reshape(batch, SSM_GROUPS, SSM_STATE)
    i = ht[..., S5_HALF:].transpose(1, 0, 2).reshape(batch, SSM_GROUPS, SSM_STATE)
    return r, i


def _glu_body(z_ref, x_ref, w_ref, b_ref, o_ref):
    g = jnp.dot(z_ref[...].astype(BF16), w_ref[...], preferred_element_type=F32) + b_ref[...]
    o_ref[...] = x_ref[...] + g[:, :D_MODEL] * jax.nn.sigmoid(g[:, D_MODEL:])


def _glu_residual(z, x2d, w_glu, b_glu):
    rows = x2d.shape[0]
    tm = min(rows, 512)
    row_spec = pl.BlockSpec((tm, D_MODEL), lambda i: (i, 0))
    return pl.pallas_call(
        _glu_body,
        out_shape=jax.ShapeDtypeStruct(x2d.shape, F32),
        grid=(rows // tm,),
        in_specs=[row_spec, row_spec,
                  pl.BlockSpec((D_MODEL, 2 * D_MODEL), lambda i: (0, 0)),
                  pl.BlockSpec((1, 2 * D_MODEL), lambda i: (0, 0))],
        out_specs=row_spec,
        compiler_params=_cparams(("arbitrary",)),
        name="glu_residual",
    )(z, x2d, w_glu.astype(BF16), b_glu.astype(F32).reshape(1, 2 * D_MODEL))


def _ffn_body(x_ref, xh_ref, cs_ref, g_ref, wa_ref, wb_ref, cw_ref, cb_ref, wd_ref,
              o_ref, co_ref, xn_scr, ha0, hb0, hh0, ha1, hb1, hh1, gt0, gt1, acc_scr, *, tm, tiles_per_seq):
    i = pl.program_id(0)
    first = (i % tiles_per_seq) == 0
    g = g_ref[...]
    xn_scr[...] = _rms(x_ref[...], g).astype(BF16)
    xnh = _rms(xh_ref[...], g).astype(BF16)
    rows = lax.broadcasted_iota(I32, (tm, FFN_FC), 0)
    sets = ((ha0, hb0, hh0, gt0), (ha1, hb1, hh1, gt1))

    def up(c, s):
        ha, hb, hh, _ = sets[s]
        xn = xn_scr[...]
        ha[...] = jnp.dot(xn, wa_ref[0, c], preferred_element_type=F32)
        hb[...] = jnp.dot(xn, wb_ref[0, c], preferred_element_type=F32)
        hh[...] = jnp.dot(xnh, wa_ref[0, c], preferred_element_type=F32)

    def act(c, s):
        ha_ref, hb_ref, hh_ref, gt = sets[s]
        ha = ha_ref[...]
        hh = jnp.where(first, cs_ref[0, c], hh_ref[...])
        co_ref[0, c] = ha[tm - SUBLANES:, :]
        h6 = hh[6:7, :]
        h7 = hh[7:8, :]
        prev1 = jnp.where(rows == 0, h7, pltpu.roll(ha, 1, 0))
        prev2 = jnp.where(rows == 0, h6, jnp.where(rows == 1, h7, pltpu.roll(ha, 2, 0)))
        cw = cw_ref[c]
        cv = cb_ref[c] + cw[0:1, :] * prev2 + cw[1:2, :] * prev1 + cw[2:3, :] * ha
        gt[...] = (jax.nn.gelu(cv) * hb_ref[...]).astype(BF16)

    def down(c, s):
        acc_scr[...] += jnp.dot(sets[s][3][...], wd_ref[c], preferred_element_type=F32)

    acc_scr[...] = jnp.zeros(acc_scr.shape, F32)
    gt1[...] = jnp.zeros(gt1.shape, BF16)
    up(0, 0)

    def body(k, carry):
        c = 2 * k
        up(c + 1, 1)
        act(c, 0)
        down(jnp.maximum(c - 1, 0), 1)
        up(c + 2, 0)
        act(c + 1, 1)
        down(c, 0)
        return carry

    lax.fori_loop(0, (FFN_NC - 1) // 2, body, 0)
    act(FFN_NC - 1, 0)
    down(FFN_NC - 2, 1)
    down(FFN_NC - 1, 0)
    o_ref[...] = x_ref[...] + acc_scr[...]


def _ffn_weights(w_up, conv_w, conv_b, w_down):
    wab = w_up.astype(BF16).reshape(D_MODEL, 2, FFN_NC, FFN_FC).transpose(1, 2, 0, 3)
    cw = jnp.pad(conv_w.astype(F32), ((0, SUBLANES - CONV_W), (0, 0)))
    cw = cw.reshape(SUBLANES, FFN_NC, FFN_FC).transpose(1, 0, 2)
    cb = conv_b.astype(F32).reshape(FFN_NC, 1, FFN_FC)
    wd = w_down.astype(BF16).reshape(FFN_NC, FFN_FC, D_MODEL)
    return wab, cw, cb, wd


def _conv_ffn(x2d, conv_state, g, weights, *, batch, seq):
    wab, cw, cb, wd = weights
    tm = min(seq, FFN_TM)
    tiles_per_seq = seq // tm
    cs = jnp.pad(conv_state.astype(F32), ((0, 0), (SUBLANES - (CONV_W - 1), 0), (0, 0)))
    cs = cs.reshape(batch, SUBLANES, FFN_NC, FFN_FC).transpose(0, 2, 1, 3)
    cs_spec = pl.BlockSpec((1, FFN_NC, SUBLANES, FFN_FC), lambda i: (i // tiles_per_seq, 0, 0, 0))
    hb = tm // SUBLANES
    const3 = lambda i: (0, 0, 0)
    up_set = [pltpu.VMEM((tm, FFN_FC), F32), pltpu.VMEM((tm, FFN_FC), F32), pltpu.VMEM((SUBLANES, FFN_FC), F32)]
    gate = pltpu.VMEM((tm, FFN_FC), BF16)
    scratch = [pltpu.VMEM((tm, D_MODEL), BF16)] + up_set + up_set + [gate, gate, pltpu.VMEM((tm, D_MODEL), F32)]
    out, co = pl.pallas_call(
        functools.partial(_ffn_body, tm=tm, tiles_per_seq=tiles_per_seq),
        out_shape=(jax.ShapeDtypeStruct(x2d.shape, F32),
                   jax.ShapeDtypeStruct((batch, FFN_NC, SUBLANES, FFN_FC), F32)),
        grid=(batch * tiles_per_seq,),
        in_specs=[pl.BlockSpec((tm, D_MODEL), lambda i: (i, 0)),
                  pl.BlockSpec((SUBLANES, D_MODEL), lambda i: (jnp.maximum(i * hb - 1, 0), 0)),
                  cs_spec,
                  pl.BlockSpec((1, D_MODEL), lambda i: (0, 0)),
                  pl.BlockSpec((1, FFN_NC, D_MODEL, FFN_FC), lambda i: (0, 0, 0, 0)),
                  pl.BlockSpec((1, FFN_NC, D_MODEL, FFN_FC), lambda i: (1, 0, 0, 0)),
                  pl.BlockSpec((FFN_NC, SUBLANES, FFN_FC), const3),
                  pl.BlockSpec((FFN_NC, 1, FFN_FC), const3),
                  pl.BlockSpec((FFN_NC, FFN_FC, D_MODEL), const3)],
        out_specs=(pl.BlockSpec((tm, D_MODEL), lambda i: (i, 0)),
                   cs_spec),
        scratch_shapes=scratch,
        compiler_params=_cparams(("arbitrary",)),
        name="conv_ffn",
    )(x2d, x2d, cs, g.astype(F32).reshape(1, D_MODEL), wab, wab, cw, cb, wd)
    new_state = co[:, :, SUBLANES - (CONV_W - 1):, :].transpose(0, 2, 1, 3).reshape(batch, CONV_W - 1, D_FF)
    return out, new_state


def _rope_tables(pos):
    half = ROT_DIM // 2
    inv = ROPE_THETA ** (-jnp.arange(half, dtype=F32) / half)
    ang = pos.astype(F32)[:, None] * inv[None, :]
    cos = jnp.cos(ang)
    sin = jnp.sin(ang)
    lane = jnp.arange(LANES) % HEAD_DIM
    idx = lane % half
    cc = jnp.where(lane[None] < ROT_DIM, cos[:, idx], 1.0)
    s1 = jnp.where((lane[None] >= half) & (lane[None] < ROT_DIM), sin[:, idx], 0.0)
    s2 = jnp.where(lane[None] < half, -sin[:, idx], 0.0)
    return cc.astype(F32), s1.astype(F32), s2.astype(F32)


def _proj_body(x_ref, g_ref, w_ref, qn_ref, kn_ref, cc_ref, s1_ref, s2_ref, bd_ref,
               q_ref, khm_ref, vhm_ref, kf_ref, vf_ref, qi_ref, kib_ref, kif_ref, wi_ref):
    xn = _rms(x_ref[...], g_ref[...]).astype(BF16)
    proj = jnp.dot(xn, w_ref[...], preferred_element_type=F32)
    cc = cc_ref[...]
    s1 = s1_ref[...]
    s2 = s2_ref[...]
    bd = bd_ref[...]

    def rope(y):
        return y * cc + pltpu.roll(y, ROT_DIM // 2, 1) * s1 + pltpu.roll(y, LANES - ROT_DIM // 2, 1) * s2

    def head_norm(y, gain):
        ms = jnp.dot((y * y).astype(BF16), bd, preferred_element_type=F32)
        return y * lax.rsqrt(ms + EPS) * gain

    qn = qn_ref[...]
    kn = kn_ref[...]
    scale = HEAD_DIM ** -0.5 * LOG2E
    one_col = jnp.where(lax.broadcasted_iota(I32, (x_ref.shape[0], HEAD_DIM), 1) == 0, 1.0, 0.0)
    for c in range(ATTN_WIDTH // LANES):
        y = proj[:, c * LANES:(c + 1) * LANES]
        q_ref[:, c * LANES:(c + 1) * LANES] = (rope(head_norm(y, qn)) * scale).astype(BF16)
    for c in range(KV_WIDTH // LANES):
        y = proj[:, OFF_K + c * LANES:OFF_K + (c + 1) * LANES]
        k = rope(head_norm(y, kn))
        kf_ref[:, c * LANES:(c + 1) * LANES] = k
        khm_ref[2 * c] = k[:, :HEAD_DIM].astype(BF16)
        khm_ref[2 * c + 1] = k[:, HEAD_DIM:].astype(BF16)
        v = proj[:, OFF_V + c * LANES:OFF_V + (c + 1) * LANES]
        vf_ref[:, c * LANES:(c + 1) * LANES] = v
        vhm_ref[2 * c] = jnp.concatenate([v[:, :HEAD_DIM], one_col], axis=1).astype(BF16)
        vhm_ref[2 * c + 1] = jnp.concatenate([v[:, HEAD_DIM:], one_col], axis=1).astype(BF16)
    iscale = IDX_DIM ** -0.5
    for c in range(IDX_HEADS * IDX_DIM // LANES):
        y = proj[:, OFF_QI + c * LANES:OFF_QI + (c + 1) * LANES]
        qi_ref[:, c * LANES:(c + 1) * LANES] = (rope(y) * iscale).astype(BF16)
    tail = proj[:, OFF_KI:OFF_KI + LANES]
    ki = rope(tail)[:, :IDX_DIM]
    kif_ref[...] = ki
    kib_ref[...] = ki.astype(BF16)
    wi_ref[...] = tail[:, IDX_DIM:IDX_DIM + IDX_HEADS] * (IDX_HEADS ** -0.5)


def _dsa_project(x2d, g, w_in, q_norm, k_norm, pos):
    rows = x2d.shape[0]
    tm = min(rows, 256)
    w = jnp.pad(w_in.astype(BF16), ((0, 0), (0, IN_COLS_PAD - IN_COLS)))
    if pos.shape[0] < tm:
        pos = jnp.tile(pos, tm // pos.shape[0])
    ntab = pos.shape[0] // tm
    cc, s1, s2 = _rope_tables(pos)
    lane = jnp.arange(LANES)
    bd = jnp.where((lane[:, None] // HEAD_DIM) == (lane[None, :] // HEAD_DIM), 1.0 / HEAD_DIM, 0.0).astype(BF16)
    qn = jnp.tile(q_norm.astype(F32), LANES // HEAD_DIM).reshape(1, LANES)
    kn = jnp.tile(k_norm.astype(F32), LANES // HEAD_DIM).reshape(1, LANES)
    row_spec = lambda width: pl.BlockSpec((tm, width), lambda i: (i, 0))
    const = lambda shape: pl.BlockSpec(shape, lambda i: (0,) * len(shape))
    hm_spec = pl.BlockSpec((N_KV_HEADS, tm, HEAD_DIM), lambda i: (0, i, 0))
    vhm_spec = pl.BlockSpec((N_KV_HEADS, tm, LANES), lambda i: (0, i, 0))
    tab_spec = pl.BlockSpec((tm, LANES), lambda i: (i % ntab, 0))
    return pl.pallas_call(
        _proj_body,
        out_shape=(jax.ShapeDtypeStruct((rows, ATTN_WIDTH), BF16),
                   jax.ShapeDtypeStruct((N_KV_HEADS, rows, HEAD_DIM), BF16),
                   jax.ShapeDtypeStruct((N_KV_HEADS, rows, LANES), BF16),
                   jax.ShapeDtypeStruct((rows, KV_WIDTH), F32),
                   jax.ShapeDtypeStruct((rows, KV_WIDTH), F32),
                   jax.ShapeDtypeStruct((rows, IDX_HEADS * IDX_DIM), BF16),
                   jax.ShapeDtypeStruct((rows, IDX_DIM), BF16),
                   jax.ShapeDtypeStruct((rows, IDX_DIM), F32),
                   jax.ShapeDtypeStruct((rows, IDX_HEADS), F32)),
        grid=(rows // tm,),
        in_specs=[row_spec(D_MODEL), const((1, D_MODEL)), const((D_MODEL, IN_COLS_PAD)),
                  const((1, LANES)), const((1, LANES)),
                  tab_spec, tab_spec, tab_spec, const((LANES, LANES))],
        out_specs=(row_spec(ATTN_WIDTH), hm_spec, vhm_spec, row_spec(KV_WIDTH), row_spec(KV_WIDTH),
                   row_spec(IDX_HEADS * IDX_DIM), row_spec(IDX_DIM), row_spec(IDX_DIM), row_spec(IDX_HEADS)),
        compiler_params=_cparams(("arbitrary",)),
        name="dsa_project",
    )(x2d, g.astype(F32).reshape(1, D_MODEL), w, qn, kn, cc, s1, s2, bd)


def _attn_body(nk_ref, q_ref, qi_ref, wi_ref, lim_ref, x_ref, k_ref, v_ref, ki_ref, wo_ref,
               o_ref, sc_scr, bias_scr, qg_scr, s_scr, p_scr, m_scr, acc_scr, *, tq, tk, topk):
    i = pl.program_id(1)
    nb = nk_ref[i]
    ncol = tk // LANES
    kf = float(topk)
    qi = qi_ref[...].astype(F32)
    wi = wi_ref[...]
    lim = lim_ref[...]
    qih = [qi[:, h * IDX_DIM:(h + 1) * IDX_DIM].astype(BF16) for h in range(IDX_HEADS)]
    wih = [wi[:, h:h + 1] for h in range(IDX_HEADS)]
    nt = (((1,), (1,)), ((), ()))
    rb = min(tq, max(SUBLANES, ATTN_BLOCK_VREGS // ncol * SUBLANES))
    hr = min(tq, COUNT_ROWS)
    lane_tk = lax.broadcasted_iota(I32, (tq, tk), 1)

    def fold(x, op):
        part = x[:, :LANES]
        for c in range(1, ncol):
            part = op(part, x[:, c * LANES:(c + 1) * LANES])
        return part

    def rows_all(x, op):
        return jnp.broadcast_to(op(x, axis=1, keepdims=True), x.shape)

    def tile_cols(x):
        return jnp.concatenate([x] * ncol, axis=1)

    def score_blk(kb, carry, masked):
        mn, mx, cpos, cnn = carry
        off = pl.multiple_of(kb * tk, tk)
        kib = ki_ref[pl.ds(off, tk), :]
        sc = jnp.zeros((tq, tk), F32)
        for h in range(IDX_HEADS):
            lg = lax.dot_general(qih[h], kib, nt, preferred_element_type=F32)
            sc = sc + jnp.maximum(lg, 0.0) * wih[h]
        if masked:
            allowed = lane_tk + off < lim
            scm = jnp.where(allowed, sc, NEG)
            mn = jnp.minimum(mn, fold(jnp.where(allowed, sc, -NEG), jnp.minimum))
        else:
            scm = sc
            mn = jnp.minimum(mn, fold(sc, jnp.minimum))
        sc_scr[kb] = scm
        mx = jnp.maximum(mx, fold(scm, jnp.maximum))
        cpos = cpos + fold(jnp.where(scm > 0.0, 1.0, 0.0), jnp.add)
        cnn = cnn + fold(jnp.where(scm >= 0.0, 1.0, 0.0), jnp.add)
        return mn, mx, cpos, cnn

    zeros = jnp.zeros((tq, LANES), F32)
    stats = lax.fori_loop(0, nb - 1, functools.partial(score_blk, masked=False),
                          (jnp.full((tq, LANES), -NEG, F32), jnp.full((tq, LANES), NEG, F32), zeros, zeros))
    mn, mx, cpos, cnn = score_blk(nb - 1, stats, masked=True)
    smin = rows_all(mn, jnp.min)
    smax = rows_all(mx, jnp.max)
    cpos = rows_all(cpos, jnp.sum)
    cnn = rows_all(cnn, jnp.sum)

    def count_ge(x):
        parts = []
        for r0 in range(0, tq, hr):
            xt = tile_cols(x[r0:r0 + hr])

            def body(kb, acc, r0=r0, xt=xt):
                return acc + fold(jnp.where(sc_scr[kb, r0:r0 + hr, :] >= xt, 1.0, 0.0), jnp.add)

            parts.append(lax.fori_loop(0, nb, body, jnp.zeros((hr, LANES), F32)))
        return rows_all(jnp.concatenate(parts, axis=0), jnp.sum)

    above_zero = cpos >= kf
    at_zero = jnp.logical_and(jnp.logical_not(above_zero), cnn >= kf)
    n_allowed = jnp.broadcast_to(jnp.minimum(lim, nb * tk).astype(F32), (tq, LANES))
    lo0 = jnp.where(above_zero, TINY, jnp.where(at_zero, 0.0, smin))
    hi0 = jnp.where(above_zero, smax + (jnp.abs(smax) + 1e-30) * 1e-6, jnp.where(at_zero, TINY, 0.0))
    cnt0 = jnp.where(above_zero, cpos, jnp.where(at_zero, cnn, n_allowed))
    frozen = at_zero

    def any_over(cnt_lo):
        return jnp.max(jnp.where(jnp.logical_and(cnt_lo > kf, jnp.logical_not(frozen)), 1.0, 0.0)) > 0.5

    def bis_step(lo, hi, cnt_lo):
        mid = 0.5 * (lo + hi)
        cnt = count_ge(mid)
        up = jnp.logical_and(cnt >= kf, jnp.logical_not(frozen))
        dn = jnp.logical_and(cnt < kf, jnp.logical_not(frozen))
        return jnp.where(up, mid, lo), jnp.where(dn, mid, hi), jnp.where(up, cnt, cnt_lo)

    def bis_cond(c):
        return c[4]

    def bis_body(c):
        it, lo, hi, cnt_lo, _ = c
        lo, hi, cnt_lo = bis_step(*bis_step(lo, hi, cnt_lo))
        return it + 2, lo, hi, cnt_lo, jnp.logical_and(any_over(cnt_lo), it + 2 < BISECT_MAX_ITERS)

    _, lo, hi, cnt_lo, _ = lax.while_loop(bis_cond, bis_body, (jnp.int32(0), lo0, hi0, cnt0, any_over(cnt0)))
    ties = jnp.max(jnp.where(cnt_lo > kf, 1.0, 0.0)) > 0.5
    lo_t = tile_cols(lo)
    hi_t = tile_cols(hi)

    @pl.when(jnp.logical_not(ties))
    def _():
        def body(kb, carry):
            s = sc_scr[kb]
            bias_scr[kb] = jnp.where((s >= lo_t) & (s > 0.5 * NEG), 0.0, NEG)
            return carry

        lax.fori_loop(0, nb, body, 0)

    @pl.when(ties)
    def _():
        need = kf - count_ge(hi)
        tri = (lax.broadcasted_iota(I32, (LANES, LANES), 0)
               <= lax.broadcasted_iota(I32, (LANES, LANES), 1)).astype(BF16)

        def body(kb, seen):
            s = sc_scr[kb]
            cols = []
            for c in range(ncol):
                sc = s[:, c * LANES:(c + 1) * LANES]
                tie = (sc >= lo) & (sc < hi)
                tief = jnp.where(tie, 1.0, 0.0)
                incl = jnp.dot(tief.astype(BF16), tri, preferred_element_type=F32)
                take = tie & ((seen + incl - tief) < need)
                sel = ((sc >= hi) | take) & (sc > 0.5 * NEG)
                cols.append(jnp.where(sel, 0.0, NEG))
                seen = seen + incl[:, LANES - 1:]
            bias_scr[kb] = jnp.concatenate(cols, axis=1)
            return seen

        lax.fori_loop(0, nb, body, zeros)

    q = q_ref[...].astype(F32)
    gq = N_HEADS // N_KV_HEADS
    for g in range(N_KV_HEADS):
        for r in range(gq):
            h = g * gq + r
            qg_scr[g, r * tq:(r + 1) * tq, :] = q[:, h * HEAD_DIM:(h + 1) * HEAD_DIM].astype(BF16)
    m_scr[...] = jnp.full(m_scr.shape, NEG, F32)
    acc_scr[...] = jnp.zeros(acc_scr.shape, F32)

    def attn_blk(kb, carry):
        off = pl.multiple_of(kb * tk, tk)
        for g in range(N_KV_HEADS):
            kblk = k_ref[g, pl.ds(off, tk), :]
            vblk = v_ref[g, pl.ds(off, tk), :]
            s_scr[g] = lax.dot_general(qg_scr[g], kblk, nt, preferred_element_type=F32)
            for r0 in range(0, gq * tq, rb):
                rows = slice(r0, r0 + rb)
                s = s_scr[g, rows, :] + bias_scr[kb, r0 % tq:r0 % tq + rb, :]
                m_old = m_scr[g, rows, :]
                m_new = jnp.maximum(m_old, jnp.max(s, axis=1, keepdims=True))
                p_scr[g, rows, :] = jnp.exp2(s - jnp.concatenate([m_new] * ncol, axis=1)).astype(BF16)
                acc_scr[g, rows, :] = jnp.exp2(m_old - m_new) * acc_scr[g, rows, :]
                m_scr[g, rows, :] = m_new
            acc_scr[g] += jnp.dot(p_scr[g], vblk, preferred_element_type=F32)
        return carry

    lax.fori_loop(0, nb, attn_blk, 0)
    outs = []
    for g in range(N_KV_HEADS):
        for r in range(gq):
            acc = acc_scr[g, r * tq:(r + 1) * tq, :]
            outs.append(acc[:, :HEAD_DIM] / acc[:, HEAD_DIM:HEAD_DIM + 1])
    o = jnp.concatenate(outs, axis=1).astype(BF16)
    o_ref[...] = x_ref[...] + jnp.dot(o, wo_ref[...], preferred_element_type=F32)


def _dsa_attend(x2d, q, qi, wi, limit, k_hm, v_hm, ki, w_o, nkeys, *, batch, tq, tk, topk):
    rows = x2d.shape[0]
    per_batch = rows // batch
    nqt = per_batch // tq
    s_len = ki.shape[0] // batch
    nblk = s_len // tk
    gq = N_HEADS // N_KV_HEADS
    row_spec = lambda width: pl.BlockSpec((tq, width), lambda b, i, nk: (b * nqt + i, 0))
    once = pl.Buffered(1)
    k_spec = pl.BlockSpec((N_KV_HEADS, s_len, HEAD_DIM), lambda b, i, nk: (0, b, 0), pipeline_mode=once)
    v_spec = pl.BlockSpec((N_KV_HEADS, s_len, LANES), lambda b, i, nk: (0, b, 0), pipeline_mode=once)
    grid_spec = pltpu.PrefetchScalarGridSpec(
        num_scalar_prefetch=1,
        grid=(batch, nqt),
        in_specs=[row_spec(ATTN_WIDTH), row_spec(IDX_HEADS * IDX_DIM), row_spec(IDX_HEADS), row_spec(1),
                  row_spec(D_MODEL), k_spec, v_spec,
                  pl.BlockSpec((s_len, IDX_DIM), lambda b, i, nk: (b, 0), pipeline_mode=once),
                  pl.BlockSpec((ATTN_WIDTH, D_MODEL), lambda b, i, nk: (0, 0), pipeline_mode=once)],
        out_specs=row_spec(D_MODEL),
        scratch_shapes=[pltpu.VMEM((nblk, tq, tk), F32), pltpu.VMEM((nblk, tq, tk), F32),
                        pltpu.VMEM((N_KV_HEADS, gq * tq, HEAD_DIM), BF16),
                        pltpu.VMEM((N_KV_HEADS, gq * tq, tk), F32), pltpu.VMEM((N_KV_HEADS, gq * tq, tk), BF16),
                        pltpu.VMEM((N_KV_HEADS, gq * tq, LANES), F32), pltpu.VMEM((N_KV_HEADS, gq * tq, LANES), F32)],
    )
    return pl.pallas_call(
        functools.partial(_attn_body, tq=tq, tk=tk, topk=topk),
        out_shape=jax.ShapeDtypeStruct(x2d.shape, F32),
        grid_spec=grid_spec,
        compiler_params=_cparams(("arbitrary", "arbitrary")),
        name="dsa_attend",
    )(nkeys, q, qi, wi, limit, x2d, k_hm, v_hm, ki, w_o.astype(BF16))


def _round_up(x, m):
    return (x + m - 1) // m * m


def kernel(x_prompt, x_sample, state_ssm_re, state_ssm_im, cache_k, cache_v, cache_kidx, cache_conv,
           norm_mix, norm_ffn, ssm_lambda_re, ssm_lambda_im, ssm_log_dt, ssm_b_re, ssm_b_im,
           ssm_c_re, ssm_c_im, ssm_d, ssm_w_glu, ssm_b_glu, attn_w_in, attn_q_norm, attn_k_norm,
           attn_w_o, ffn_w_up, ffn_conv_w, ffn_conv_b, ffn_w_down):
    b_p, t_p, _ = x_prompt.shape
    b_s, t_s, _ = x_sample.shape
    past = cache_k.shape[2]
    xp = x_prompt.astype(F32).reshape(b_p * t_p, D_MODEL)
    xs = x_sample.astype(F32).reshape(b_s * t_s, D_MODEL)

    ops = _s5_operators(ssm_lambda_re[0], ssm_lambda_im[0], ssm_log_dt[0], ssm_b_re[0], ssm_b_im[0],
                        ssm_c_re[0], ssm_c_im[0], ssm_d[0])
    zero_state = jnp.zeros((S5_OCT, b_p, 2 * S5_HALF), F32)
    zp, htp = _s5_scan(_norm(xp, norm_mix[0]), ops, zero_state, batch=b_p, seq=t_p)
    zs, hts = _s5_scan(_norm(xs, norm_mix[0]), ops, _state_to_oct(state_ssm_re[0], state_ssm_im[0]),
                       batch=b_s, seq=t_s)
    xp = _glu_residual(zp, xp, ssm_w_glu[0], ssm_b_glu[0])
    xs = _glu_residual(zs, xs, ssm_w_glu[0], ssm_b_glu[0])
    re_p, im_p = _oct_to_state(htp, b_p)
    re_s, im_s = _oct_to_state(hts, b_s)

    ffn0 = _ffn_weights(ffn_w_up[0], ffn_conv_w[0], ffn_conv_b[0], ffn_w_down[0])
    xp, conv_p0 = _conv_ffn(xp, jnp.zeros((b_p, CONV_W - 1, D_FF), F32), norm_ffn[0], ffn0, batch=b_p, seq=t_p)
    xs, conv_s0 = _conv_ffn(xs, cache_conv[0], norm_ffn[0], ffn0, batch=b_s, seq=t_s)

    pos_p = jnp.arange(t_p, dtype=I32)
    pos_s = past + jnp.arange(t_s, dtype=I32)
    topk_p = min(TOPK_MAX, t_p // 4)
    topk_s = min(TOPK_MAX, (past + t_s) // 4)

    (q_p, khm_p, vhm_p, kf_p, vf_p, qi_p, kib_p, kif_p, wi_p) = _dsa_project(
        xp, norm_mix[1], attn_w_in[0], attn_q_norm[0], attn_k_norm[0], pos_p)
    tq_p = min(t_p, 256)
    tk_p = min(t_p, 512)
    lim_p = jnp.tile((pos_p // CHUNK + 1) * CHUNK, b_p).reshape(b_p * t_p, 1)
    q_end = (jnp.arange(t_p // tq_p, dtype=I32) + 1) * tq_p
    nk_p = jnp.minimum((q_end + tk_p - 1) // tk_p, t_p // tk_p).astype(I32)
    xp = _dsa_attend(xp, q_p, qi_p, wi_p, lim_p, khm_p, vhm_p, kib_p, attn_w_o[0], nk_p,
                     batch=b_p, tq=tq_p, tk=tk_p, topk=topk_p)

    (q_s, khm_s, vhm_s, kf_s, vf_s, qi_s, kib_s, kif_s, wi_s) = _dsa_project(
        xs, norm_mix[1], attn_w_in[0], attn_q_norm[0], attn_k_norm[0], pos_s)
    tk_s = 11 * LANES
    s_all = past + t_s
    s_pad = _round_up(s_all, tk_s)

    def with_cache(cache, new_hm, extra=None):
        heads, w = new_hm.shape[0], new_hm.shape[-1]
        c = jnp.pad(cache.astype(BF16).transpose(2, 0, 1, 3),
                    ((0, 0), (0, 0), (0, s_pad - past), (0, w - cache.shape[-1])))
        if extra is not None:
            c = c + extra
        c = lax.dynamic_update_slice(c, new_hm.reshape(heads, b_s, t_s, w), (0, 0, past, 0))
        return c.reshape(heads, b_s * s_pad, w)

    one_lane = (jnp.arange(LANES) == HEAD_DIM).astype(BF16)
    k_all = with_cache(cache_k[0], khm_s)
    v_all = with_cache(cache_v[0], vhm_s, extra=one_lane)
    ki_all = with_cache(cache_kidx[0][:, :, None, :], kib_s[None])[0]
    lim_s = jnp.full((b_s * t_s, 1), s_all, I32)
    nk_s = jnp.full((1,), s_pad // tk_s, I32)
    xs = _dsa_attend(xs, q_s, qi_s, wi_s, lim_s, k_all, v_all, ki_all, attn_w_o[0], nk_s,
                     batch=b_s, tq=t_s, tk=tk_s, topk=topk_s)

    ffn1 = _ffn_weights(ffn_w_up[1], ffn_conv_w[1], ffn_conv_b[1], ffn_w_down[1])
    xp, conv_p1 = _conv_ffn(xp, jnp.zeros((b_p, CONV_W - 1, D_FF), F32), norm_ffn[1], ffn1, batch=b_p, seq=t_p)
    xs, conv_s1 = _conv_ffn(xs, cache_conv[1], norm_ffn[1], ffn1, batch=b_s, seq=t_s)

    dt = x_prompt.dtype
    kv_shape_p = (1, b_p, t_p, N_KV_HEADS, HEAD_DIM)
    kv_shape_s = (1, b_s, t_s, N_KV_HEADS, HEAD_DIM)
    return (xp.reshape(x_prompt.shape).astype(dt), xs.reshape(x_sample.shape).astype(x_sample.dtype),
            re_p[None].astype(dt), im_p[None].astype(dt),
            re_s[None].astype(state_ssm_re.dtype), im_s[None].astype(state_ssm_im.dtype),
            kf_p.reshape(kv_shape_p).astype(dt), vf_p.reshape(kv_shape_p).astype(dt),
            kif_p.reshape(1, b_p, t_p, IDX_DIM).astype(dt),
            kf_s.reshape(kv_shape_s).astype(dt), vf_s.reshape(kv_shape_s).astype(dt),
            kif_s.reshape(1, b_s, t_s, IDX_DIM).astype(dt),
            jnp.stack([conv_p0, conv_p1]).astype(dt), jnp.stack([conv_s0, conv_s1]).astype(dt))
```

```python
import functools
import math

import jax
import jax.numpy as jnp
from jax import lax
from jax.experimental import pallas as pl
from jax.experimental.pallas import tpu as pltpu

F32 = jnp.float32
BF16 = jnp.bfloat16
I32 = jnp.int32

D_MODEL = 1024
CHUNK = 64
SSM_GROUP = 16
SSM_GROUPS = D_MODEL // SSM_GROUP
SSM_STATE = 64
N_HEADS = 16
N_KV_HEADS = 4
HEAD_DIM = 64
ATTN_WIDTH = N_HEADS * HEAD_DIM
KV_WIDTH = N_KV_HEADS * HEAD_DIM
IDX_HEADS = 8
IDX_DIM = 64
TOPK_MAX = 256
ROT_DIM = HEAD_DIM // 4
ROPE_THETA = 500000.0
OFF_K = ATTN_WIDTH
OFF_V = OFF_K + KV_WIDTH
OFF_QI = OFF_V + KV_WIDTH
OFF_KI = OFF_QI + IDX_HEADS * IDX_DIM
OFF_WI = OFF_KI + IDX_DIM
IN_COLS = OFF_WI + IDX_HEADS
D_FF = 2816
CONV_W = 3
EPS = 1e-6
NEG = -1e30

LANES = 128
SUBLANES = 8
VMEM_LIMIT_BYTES = 56 * 1024 * 1024

S5_L = 16
S5_OCT = D_MODEL // LANES
S5_OCT_GROUPS = LANES // SSM_GROUP
S5_HALF = S5_OCT_GROUPS * SSM_STATE
S5_PW_ROWS = 24
FFN_TM = 512
FFN_FC = 256
FFN_NC = D_FF // FFN_FC
assert FFN_NC % 2 == 1
IN_COLS_PAD = 2176
LOG2E = math.log2(math.e)
BISECT_MAX_ITERS = 40
ATTN_BLOCK_VREGS = 32
TINY = 1.1754944e-38
COUNT_ROWS = 128


def _cparams(sem):
    return pltpu.CompilerParams(dimension_semantics=sem, vmem_limit_bytes=VMEM_LIMIT_BYTES)


def _rms(x, g):
    ms = jnp.mean(x * x, axis=-1, keepdims=True)
    return x * lax.rsqrt(ms + EPS) * g


def _norm_body(x_ref, g_ref, o_ref):
    o_ref[...] = _rms(x_ref[...], g_ref[...])


def _norm(x2d, g):
    rows = x2d.shape[0]
    tm = min(rows, 1024)
    return pl.pallas_call(
        _norm_body,
        out_shape=jax.ShapeDtypeStruct((rows, D_MODEL), F32),
        grid=(rows // tm,),
        in_specs=[pl.BlockSpec((tm, D_MODEL), lambda i: (i, 0)),
                  pl.BlockSpec((1, D_MODEL), lambda i: (0, 0))],
        out_specs=pl.BlockSpec((tm, D_MODEL), lambda i: (i, 0)),
        compiler_params=_cparams(("arbitrary",)),
        name="rmsnorm",
    )(x2d, g.astype(F32).reshape(1, D_MODEL))


def _s5_operators(lam_re, lam_im, log_dt, b_re, b_im, c_re, c_im, d_skip):
    L = S5_L
    lre = jnp.minimum(lam_re.astype(F32), -1e-4)
    lim = lam_im.astype(F32)
    dt = jnp.exp(log_dt.astype(F32))[:, None]
    n = jnp.arange(L + 1, dtype=F32)[:, None, None]
    mag = jnp.exp(n * (lre * dt)[None])
    ang = n * (lim * dt)[None]
    pr = mag * jnp.cos(ang)
    pi = mag * jnp.sin(ang)
    a_re, a_im = pr[1], pi[1]
    den = lre * lre + lim * lim
    n_re = a_re - 1.0
    f_re = (n_re * lre + a_im * lim) / den
    f_im = (a_im * lre - n_re * lim) / den
    br = b_re.astype(F32)
    bi = b_im.astype(F32)
    bb_re = f_re[..., None] * br - f_im[..., None] * bi
    bb_im = f_re[..., None] * bi + f_im[..., None] * br
    cr = c_re.astype(F32)
    ci = c_im.astype(F32)
    eye = jnp.eye(S5_OCT_GROUPS, dtype=F32)
    og = (S5_OCT, S5_OCT_GROUPS)

    tg = lambda m: jnp.moveaxis(m, 0, -1)
    crt, cit = tg(cr)[None, :, :, None], tg(ci)[None, :, :, None]
    prt = tg(pr[:L].transpose(1, 0, 2))[:, None, :, None]
    pit = tg(pi[:L].transpose(1, 0, 2))[:, None, :, None]
    brt, bit = tg(bb_re)[None, None], tg(bb_im)[None, None]
    kl = jnp.sum((crt * prt - cit * pit) * brt - (crt * pit + cit * prt) * bit, axis=2)
    krow = kl.transpose(0, 3, 2, 1).reshape(L, S5_OCT, LANES, SSM_GROUP)
    lane_group = jnp.arange(LANES) // SSM_GROUP
    same_group = (lane_group[:, None] == lane_group[None, :]).astype(F32)
    kbd = jnp.tile(krow, (1, 1, 1, S5_OCT_GROUPS)) * same_group
    kpad = jnp.concatenate([jnp.zeros_like(kbd[:1]), kbd], axis=0)
    k2 = []
    for dlt in range(L // 2):
        top = jnp.concatenate([kpad[2 * dlt + 1], kpad[2 * dlt + 2]], axis=-1)
        bot = jnp.concatenate([kpad[2 * dlt], kpad[2 * dlt + 1]], axis=-1)
        k2.append(jnp.concatenate([top, bot], axis=-2))
    k2 = jnp.stack(k2, axis=1).astype(BF16)

    def bd_in(m):
        m = m.reshape(og + (SSM_STATE, SSM_GROUP))
        return jnp.einsum("jgpc,gh->jgchp", m, eye).reshape(S5_OCT, LANES, S5_HALF)

    ab_re = a_re[..., None] * bb_re - a_im[..., None] * bb_im
    ab_im = a_re[..., None] * bb_im + a_im[..., None] * bb_re
    f2 = jnp.concatenate([
        jnp.concatenate([bd_in(ab_re), bd_in(ab_im)], axis=-1),
        jnp.concatenate([bd_in(bb_re), bd_in(bb_im)], axis=-1)], axis=-2).astype(BF16)

    def bd_out(m):
        m = m.reshape(og + (SSM_GROUP, SSM_STATE))
        return jnp.einsum("jgcp,gh->jgphc", m, eye).reshape(S5_OCT, S5_HALF, LANES)

    c1r = cr * a_re[:, None, :] - ci * a_im[:, None, :]
    c1i = cr * a_im[:, None, :] + ci * a_re[:, None, :]
    c2 = jnp.concatenate([
        jnp.concatenate([bd_out(cr), bd_out(c1r)], axis=-1),
        jnp.concatenate([-bd_out(ci), -bd_out(c1i)], axis=-1)], axis=-2).astype(BF16)

    def oct_lanes(p):
        return p.reshape((L + 1,) + og + (SSM_STATE,)).transpose(1, 0, 2, 3).reshape(S5_OCT, L + 1, S5_HALF)

    pw = jnp.concatenate([oct_lanes(pr), oct_lanes(pi)], axis=-1)
    pw = jnp.pad(pw, ((0, 0), (0, S5_PW_ROWS - (L + 1)), (0, 0)))
    dsk = d_skip.astype(F32).reshape(S5_OCT, 1, LANES)
    dsk2 = jnp.concatenate([dsk, dsk], axis=-1)
    return f2, k2, c2, pw, dsk2


def _s5_body(u_ref, f2_ref, k2_ref, c2_ref, pw_ref, dsk_ref, h0_ref, z_ref, ht_ref, v_scr, hp_scr, *, bpt, nck):
    L = S5_L
    H = S5_HALF
    tr = bpt * nck

    def cmul(xr, xi, n):
        pr = pw_ref[0, n:n + 1, :H]
        pi = pw_ref[0, n:n + 1, H:]
        return xr * pr - xi * pi, xr * pi + xi * pr

    ufp = [jnp.concatenate([u_ref[pl.ds(2 * s, tr, stride=L), :], u_ref[pl.ds(2 * s + 1, tr, stride=L), :]], axis=1)
           for s in range(L // 2)]
    upair = [u.astype(BF16) for u in ufp]

    vr = vi = None
    for s2 in range(L // 2):
        p = jnp.dot(upair[s2], f2_ref[0], preferred_element_type=F32)
        qr, qi = cmul(p[:, :H], p[:, H:], L - 2 - 2 * s2)
        vr = qr if vr is None else vr + qr
        vi = qi if vi is None else vi + qi
    v_scr[:, :H] = vr
    v_scr[:, H:] = vi

    alr = pw_ref[0, L:L + 1, :H]
    ali = pw_ref[0, L:L + 1, H:]
    for b in range(bpt):
        def step(k, carry, b=b):
            hr, hi = carry
            row = b * nck + k
            hp_scr[pl.ds(row, 1), :H] = hr
            hp_scr[pl.ds(row, 1), H:] = hi
            wr = v_scr[pl.ds(row, 1), :H]
            wi = v_scr[pl.ds(row, 1), H:]
            return hr * alr - hi * ali + wr, hr * ali + hi * alr + wi

        h0r = h0_ref[0, 0, b:b + 1, :H]
        h0i = h0_ref[0, 0, b:b + 1, H:]
        hr, hi = lax.fori_loop(0, nck, step, (h0r, h0i))
        ht_ref[0, 0, b:b + 1, :H] = hr
        ht_ref[0, 0, b:b + 1, H:] = hi

    hpr = hp_scr[:, :H]
    hpi = hp_scr[:, H:]
    dsk = dsk_ref[0]
    for t2 in range(L // 2):
        gr, gi = cmul(hpr, hpi, 2 * t2 + 1)
        g = jnp.concatenate([gr, gi], axis=1).astype(BF16)
        y = jnp.dot(g, c2_ref[0], preferred_element_type=F32)
        for s2 in range(t2 + 1):
            y = y + jnp.dot(upair[s2], k2_ref[0, t2 - s2], preferred_element_type=F32)
        z = jax.nn.gelu(y + dsk * ufp[t2])
        z_ref[pl.ds(2 * t2, tr, stride=L), :] = z[:, :LANES]
        z_ref[pl.ds(2 * t2 + 1, tr, stride=L), :] = z[:, LANES:]


def _s5_scan(u2d, ops, h0, *, batch, seq):
    L = S5_L
    nck = seq // L
    n_rows = batch * nck
    bpt = 1 if nck % SUBLANES == 0 else batch
    tr = bpt * nck
    n_tiles = n_rows // tr
    h0 = h0.reshape(S5_OCT, n_tiles, bpt, 2 * S5_HALF)
    tok_spec = pl.BlockSpec((tr * L, LANES), lambda j, i: (i, j))
    per_oct = lambda a: pl.BlockSpec((1,) + a.shape[1:], lambda j, i: (j,) + (0,) * (a.ndim - 1))
    state_spec = pl.BlockSpec((1, 1, bpt, 2 * S5_HALF), lambda j, i: (j, i, 0, 0))
    z, ht = pl.pallas_call(
        functools.partial(_s5_body, bpt=bpt, nck=nck),
        out_shape=(jax.ShapeDtypeStruct(u2d.shape, F32),
                   jax.ShapeDtypeStruct((S5_OCT, n_tiles, bpt, 2 * S5_HALF), F32)),
        grid=(S5_OCT, n_tiles),
        in_specs=[tok_spec] + [per_oct(a) for a in ops] + [state_spec],
        out_specs=(tok_spec, state_spec),
        scratch_shapes=[pltpu.VMEM((tr, 2 * S5_HALF), F32), pltpu.VMEM((tr, 2 * S5_HALF), F32)],
        compiler_params=_cparams(("arbitrary", "arbitrary")),
        name="s5_scan",
    )(u2d, *ops, h0)
    return z, ht


def _state_to_oct(s_re, s_im):
    b = s_re.shape[0]
    r = s_re.astype(F32).reshape(b, S5_OCT, S5_HALF).transpose(1, 0, 2)
    i = s_im.astype(F32).reshape(b, S5_OCT, S5_HALF).transpose(1, 0, 2)
    return jnp.concatenate([r, i], axis=-1)


def _oct_to_state(ht, batch):
    ht = ht.reshape(S5_OCT, batch, 2 * S5_HALF)
    r = ht[..., :S5_HALF].transpose(1, 0, 2).reshape(batch, SSM_GROUPS, SSM_STATE)
    i = ht[..., S5_HALF:].transpose(1, 0, 2).reshape(batch, SSM_GROUPS, SSM_STATE)
    return r, i


def _glu_body(z_ref, x_ref, w_ref, b_ref, o_ref):
    g = jnp.dot(z_ref[...].astype(BF16), w_ref[...], preferred_element_type=F32) + b_ref[...]
    o_ref[...] = x_ref[...] + g[:, :D_MODEL] * jax.nn.sigmoid(g[:, D_MODEL:])


def _glu_residual(z, x2d, w_glu, b_glu):
    rows = x2d.shape[0]
    tm = min(rows, 512)
    row_spec = pl.BlockSpec((tm, D_MODEL), lambda i: (i, 0))
    return pl.pallas_call(
        _glu_body,
        out_shape=jax.ShapeDtypeStruct(x2d.shape, F32),
        grid=(rows // tm,),
        in_specs=[row_spec, row_spec,
                  pl.BlockSpec((D_MODEL, 2 * D_MODEL), lambda i: (0, 0)),
                  pl.BlockSpec((1, 2 * D_MODEL), lambda i: (0, 0))],
        out_specs=row_spec,
        compiler_params=_cparams(("arbitrary",)),
        name="glu_residual",
    )(z, x2d, w_glu.astype(BF16), b_glu.astype(F32).reshape(1, 2 * D_MODEL))


def _ffn_body(x_ref, xh_ref, cs_ref, g_ref, wa_ref, wb_ref, cw_ref, cb_ref, wd_ref,
              o_ref, co_ref, xn_scr, ha0, hb0, hh0, ha1, hb1, hh1, gt0, gt1, acc_scr, *, tm, tiles_per_seq):
    i = pl.program_id(0)
    first = (i % tiles_per_seq) == 0
    g = g_ref[...]
    xn_scr[...] = _rms(x_ref[...], g).astype(BF16)
    xnh = _rms(xh_ref[...], g).astype(BF16)
    rows = lax.broadcasted_iota(I32, (tm, FFN_FC), 0)
    sets = ((ha0, hb0, hh0, gt0), (ha1, hb1, hh1, gt1))

    def up(c, s):
        ha, hb, hh, _ = sets[s]
        xn = xn_scr[...]
        ha[...] = jnp.dot(xn, wa_ref[0, c], preferred_element_type=F32)
        hb[...] = jnp.dot(xn, wb_ref[0, c], preferred_element_type=F32)
        hh[...] = jnp.dot(xnh, wa_ref[0, c], preferred_element_type=F32)

    def act(c, s):
        ha_ref, hb_ref, hh_ref, gt = sets[s]
        ha = ha_ref[...]
        hh = jnp.where(first, cs_ref[0, c], hh_ref[...])
        co_ref[0, c] = ha[tm - SUBLANES:, :]
        h6 = hh[6:7, :]
        h7 = hh[7:8, :]
        prev1 = jnp.where(rows == 0, h7, pltpu.roll(ha, 1, 0))
        prev2 = jnp.where(rows == 0, h6, jnp.where(rows == 1, h7, pltpu.roll(ha, 2, 0)))
        cw = cw_ref[c]
        cv = cb_ref[c] + cw[0:1, :] * prev2 + cw[1:2, :] * prev1 + cw[2:3, :] * ha
        gt[...] = (jax.nn.gelu(cv) * hb_ref[...]).astype(BF16)

    def down(c, s):
        acc_scr[...] += jnp.dot(sets[s][3][...], wd_ref[c], preferred_element_type=F32)

    acc_scr[...] = jnp.zeros(acc_scr.shape, F32)
    gt1[...] = jnp.zeros(gt1.shape, BF16)
    up(0, 0)

    def body(k, carry):
        c = 2 * k
        up(c + 1, 1)
        act(c, 0)
        down(jnp.maximum(c - 1, 0), 1)
        up(c + 2, 0)
        act(c + 1, 1)
        down(c, 0)
        return carry

    lax.fori_loop(0, (FFN_NC - 1) // 2, body, 0)
    act(FFN_NC - 1, 0)
    down(FFN_NC - 2, 1)
    down(FFN_NC - 1, 0)
    o_ref[...] = x_ref[...] + acc_scr[...]


def _ffn_weights(w_up, conv_w, conv_b, w_down):
    wab = w_up.astype(BF16).reshape(D_MODEL, 2, FFN_NC, FFN_FC).transpose(1, 2, 0, 3)
    cw = jnp.pad(conv_w.astype(F32), ((0, SUBLANES - CONV_W), (0, 0)))
    cw = cw.reshape(SUBLANES, FFN_NC, FFN_FC).transpose(1, 0, 2)
    cb = conv_b.astype(F32).reshape(FFN_NC, 1, FFN_FC)
    wd = w_down.astype(BF16).reshape(FFN_NC, FFN_FC, D_MODEL)
    return wab, cw, cb, wd


def _conv_ffn(x2d, conv_state, g, weights, *, batch, seq):
    wab, cw, cb, wd = weights
    tm = min(seq, FFN_TM)
    tiles_per_seq = seq // tm
    cs = jnp.pad(conv_state.astype(F32), ((0, 0), (SUBLANES - (CONV_W - 1), 0), (0, 0)))
    cs = cs.reshape(batch, SUBLANES, FFN_NC, FFN_FC).transpose(0, 2, 1, 3)
    cs_spec = pl.BlockSpec((1, FFN_NC, SUBLANES, FFN_FC), lambda i: (i // tiles_per_seq, 0, 0, 0))
    hb = tm // SUBLANES
    const3 = lambda i: (0, 0, 0)
    up_set = [pltpu.VMEM((tm, FFN_FC), F32), pltpu.VMEM((tm, FFN_FC), F32), pltpu.VMEM((SUBLANES, FFN_FC), F32)]
    gate = pltpu.VMEM((tm, FFN_FC), BF16)
    scratch = [pltpu.VMEM((tm, D_MODEL), BF16)] + up_set + up_set + [gate, gate, pltpu.VMEM((tm, D_MODEL), F32)]
    out, co = pl.pallas_call(
        functools.partial(_ffn_body, tm=tm, tiles_per_seq=tiles_per_seq),
        out_shape=(jax.ShapeDtypeStruct(x2d.shape, F32),
                   jax.ShapeDtypeStruct((batch, FFN_NC, SUBLANES, FFN_FC), F32)),
        grid=(batch * tiles_per_seq,),
        in_specs=[pl.BlockSpec((tm, D_MODEL), lambda i: (i, 0)),
                  pl.BlockSpec((SUBLANES, D_MODEL), lambda i: (jnp.maximum(i * hb - 1, 0), 0)),
                  cs_spec,
                  pl.BlockSpec((1, D_MODEL), lambda i: (0, 0)),
                  pl.BlockSpec((1, FFN_NC, D_MODEL, FFN_FC), lambda i: (0, 0, 0, 0)),
                  pl.BlockSpec((1, FFN_NC, D_MODEL, FFN_FC), lambda i: (1, 0, 0, 0)),
                  pl.BlockSpec((FFN_NC, SUBLANES, FFN_FC), const3),
                  pl.BlockSpec((FFN_NC, 1, FFN_FC), const3),
                  pl.BlockSpec((FFN_NC, FFN_FC, D_MODEL), const3)],
        out_specs=(pl.BlockSpec((tm, D_MODEL), lambda i: (i, 0)),
                   cs_spec),
        scratch_shapes=scratch,
        compiler_params=_cparams(("arbitrary",)),
        name="conv_ffn",
    )(x2d, x2d, cs, g.astype(F32).reshape(1, D_MODEL), wab, wab, cw, cb, wd)
    new_state = co[:, :, SUBLANES - (CONV_W - 1):, :].transpose(0, 2, 1, 3).reshape(batch, CONV_W - 1, D_FF)
    return out, new_state


def _rope_tables(pos):
    half = ROT_DIM // 2
    inv = ROPE_THETA ** (-jnp.arange(half, dtype=F32) / half)
    ang = pos.astype(F32)[:, None] * inv[None, :]
    cos = jnp.cos(ang)
    sin = jnp.sin(ang)
    lane = jnp.arange(LANES) % HEAD_DIM
    idx = lane % half
    cc = jnp.where(lane[None] < ROT_DIM, cos[:, idx], 1.0)
    s1 = jnp.where((lane[None] >= half) & (lane[None] < ROT_DIM), sin[:, idx], 0.0)
    s2 = jnp.where(lane[None] < half, -sin[:, idx], 0.0)
    return cc.astype(F32), s1.astype(F32), s2.astype(F32)


def _proj_body(x_ref, g_ref, w_ref, qn_ref, kn_ref, cc_ref, s1_ref, s2_ref, bd_ref,
               q_ref, khm_ref, vhm_ref, kf_ref, vf_ref, qi_ref, kib_ref, kif_ref, wi_ref):
    xn = _rms(x_ref[...], g_ref[...]).astype(BF16)
    proj = jnp.dot(xn, w_ref[...], preferred_element_type=F32)
    cc = cc_ref[...]
    s1 = s1_ref[...]
    s2 = s2_ref[...]
    bd = bd_ref[...]

    def rope(y):
        return y * cc + pltpu.roll(y, ROT_DIM // 2, 1) * s1 + pltpu.roll(y, LANES - ROT_DIM // 2, 1) * s2

    def head_norm(y, gain):
        ms = jnp.dot((y * y).astype(BF16), bd, preferred_element_type=F32)
        return y * lax.rsqrt(ms + EPS) * gain

    qn = qn_ref[...]
    kn = kn_ref[...]
    scale = HEAD_DIM ** -0.5 * LOG2E
    one_col = jnp.where(lax.broadcasted_iota(I32, (x_ref.shape[0], HEAD_DIM), 1) == 0, 1.0, 0.0)
    for c in range(ATTN_WIDTH // LANES):
        y = proj[:, c * LANES:(c + 1) * LANES]
        q_ref[:, c * LANES:(c + 1) * LANES] = (rope(head_norm(y, qn)) * scale).astype(BF16)
    for c in range(KV_WIDTH // LANES):
        y = proj[:, OFF_K + c * LANES:OFF_K + (c + 1) * LANES]
        k = rope(head_norm(y, kn))
        kf_ref[:, c * LANES:(c + 1) * LANES] = k
        khm_ref[2 * c] = k[:, :HEAD_DIM].astype(BF16)
        khm_ref[2 * c + 1] = k[:, HEAD_DIM:].astype(BF16)
        v = proj[:, OFF_V + c * LANES:OFF_V + (c + 1) * LANES]
        vf_ref[:, c * LANES:(c + 1) * LANES] = v
        vhm_ref[2 * c] = jnp.concatenate([v[:, :HEAD_DIM], one_col], axis=1).astype(BF16)
        vhm_ref[2 * c + 1] = jnp.concatenate([v[:, HEAD_DIM:], one_col], axis=1).astype(BF16)
    iscale = IDX_DIM ** -0.5
    for c in range(IDX_HEADS * IDX_DIM // LANES):
        y = proj[:, OFF_QI + c * LANES:OFF_QI + (c + 1) * LANES]
        qi_ref[:, c * LANES:(c + 1) * LANES] = (rope(y) * iscale).astype(BF16)
    tail = proj[:, OFF_KI:OFF_KI + LANES]
    ki = rope(tail)[:, :IDX_DIM]
    kif_ref[...] = ki
    kib_ref[...] = ki.astype(BF16)
    wi_ref[...] = tail[:, IDX_DIM:IDX_DIM + IDX_HEADS] * (IDX_HEADS ** -0.5)


def _dsa_project(x2d, g, w_in, q_norm, k_norm, pos):
    rows = x2d.shape[0]
    tm = min(rows, 512)
    w = jnp.pad(w_in.astype(BF16), ((0, 0), (0, IN_COLS_PAD - IN_COLS)))
    if pos.shape[0] < tm:
        pos = jnp.tile(pos, tm // pos.shape[0])
    ntab = pos.shape[0] // tm
    cc, s1, s2 = _rope_tables(pos)
    lane = jnp.arange(LANES)
    bd = jnp.where((lane[:, None] // HEAD_DIM) == (lane[None, :] // HEAD_DIM), 1.0 / HEAD_DIM, 0.0).astype(BF16)
    qn = jnp.tile(q_norm.astype(F32), LANES // HEAD_DIM).reshape(1, LANES)
    kn = jnp.tile(k_norm.astype(F32), LANES // HEAD_DIM).reshape(1, LANES)
    row_spec = lambda width: pl.BlockSpec((tm, width), lambda i: (i, 0))
    const = lambda shape: pl.BlockSpec(shape, lambda i: (0,) * len(shape))
    hm_spec = pl.BlockSpec((N_KV_HEADS, tm, HEAD_DIM), lambda i: (0, i, 0))
    vhm_spec = pl.BlockSpec((N_KV_HEADS, tm, LANES), lambda i: (0, i, 0))
    tab_spec = pl.BlockSpec((tm, LANES), lambda i: (i % ntab, 0))
    return pl.pallas_call(
        _proj_body,
        out_shape=(jax.ShapeDtypeStruct((rows, ATTN_WIDTH), BF16),
                   jax.ShapeDtypeStruct((N_KV_HEADS, rows, HEAD_DIM), BF16),
                   jax.ShapeDtypeStruct((N_KV_HEADS, rows, LANES), BF16),
                   jax.ShapeDtypeStruct((rows, KV_WIDTH), F32),
                   jax.ShapeDtypeStruct((rows, KV_WIDTH), F32),
                   jax.ShapeDtypeStruct((rows, IDX_HEADS * IDX_DIM), BF16),
                   jax.ShapeDtypeStruct((rows, IDX_DIM), BF16),
                   jax.ShapeDtypeStruct((rows, IDX_DIM), F32),
                   jax.ShapeDtypeStruct((rows, IDX_HEADS), F32)),
        grid=(rows // tm,),
        in_specs=[row_spec(D_MODEL), const((1, D_MODEL)), const((D_MODEL, IN_COLS_PAD)),
                  const((1, LANES)), const((1, LANES)),
                  tab_spec, tab_spec, tab_spec, const((LANES, LANES))],
        out_specs=(row_spec(ATTN_WIDTH), hm_spec, vhm_spec, row_spec(KV_WIDTH), row_spec(KV_WIDTH),
                   row_spec(IDX_HEADS * IDX_DIM), row_spec(IDX_DIM), row_spec(IDX_DIM), row_spec(IDX_HEADS)),
        compiler_params=_cparams(("arbitrary",)),
        name="dsa_project",
    )(x2d, g.astype(F32).reshape(1, D_MODEL), w, qn, kn, cc, s1, s2, bd)


def _attn_body(nk_ref, q_ref, qi_ref, wi_ref, lim_ref, x_ref, k_ref, v_ref, ki_ref, wo_ref,
               o_ref, sc_scr, bias_scr, qg_scr, s_scr, p_scr, m_scr, acc_scr, *, tq, tk, topk):
    i = pl.program_id(1)
    nb = nk_ref[i]
    ncol = tk // LANES
    kf = float(topk)
    qi = qi_ref[...].astype(F32)
    wi = wi_ref[...]
    lim = lim_ref[...]
    qih = [qi[:, h * IDX_DIM:(h + 1) * IDX_DIM].astype(BF16) for h in range(IDX_HEADS)]
    wih = [wi[:, h:h + 1] for h in range(IDX_HEADS)]
    nt = (((1,), (1,)), ((), ()))
    rb = min(tq, max(SUBLANES, ATTN_BLOCK_VREGS // ncol * SUBLANES))
    hr = min(tq, COUNT_ROWS)
    lane_tk = lax.broadcasted_iota(I32, (tq, tk), 1)

    def fold(x, op):
        part = x[:, :LANES]
        for c in range(1, ncol):
            part = op(part, x[:, c * LANES:(c + 1) * LANES])
        return part

    def rows_all(x, op):
        return jnp.broadcast_to(op(x, axis=1, keepdims=True), x.shape)

    def tile_cols(x):
        return jnp.concatenate([x] * ncol, axis=1)

    def score_blk(kb, carry, masked):
        mn, mx, cpos, cnn = carry
        off = pl.multiple_of(kb * tk, tk)
        kib = ki_ref[pl.ds(off, tk), :]
        sc = jnp.zeros((tq, tk), F32)
        for h in range(IDX_HEADS):
            lg = lax.dot_general(qih[h], kib, nt, preferred_element_type=F32)
            sc = sc + jnp.maximum(lg, 0.0) * wih[h]
        if masked:
            allowed = lane_tk + off < lim
            scm = jnp.where(allowed, sc, NEG)
            mn = jnp.minimum(mn, fold(jnp.where(allowed, sc, -NEG), jnp.minimum))
        else:
            scm = sc
            mn = jnp.minimum(mn, fold(sc, jnp.minimum))
        sc_scr[kb] = scm
        mx = jnp.maximum(mx, fold(scm, jnp.maximum))
        cpos = cpos + fold(jnp.where(scm > 0.0, 1.0, 0.0), jnp.add)
        cnn = cnn + fold(jnp.where(scm >= 0.0, 1.0, 0.0), jnp.add)
        return mn, mx, cpos, cnn

    zeros = jnp.zeros((tq, LANES), F32)
    stats = lax.fori_loop(0, nb - 1, functools.partial(score_blk, masked=False),
                          (jnp.full((tq, LANES), -NEG, F32), jnp.full((tq, LANES), NEG, F32), zeros, zeros))
    mn, mx, cpos, cnn = score_blk(nb - 1, stats, masked=True)
    smin = rows_all(mn, jnp.min)
    smax = rows_all(mx, jnp.max)
    cpos = rows_all(cpos, jnp.sum)
    cnn = rows_all(cnn, jnp.sum)

    def count_ge(x):
        parts = []
        for r0 in range(0, tq, hr):
            xt = tile_cols(x[r0:r0 + hr])

            def body(kb, acc, r0=r0, xt=xt):
                return acc + fold(jnp.where(sc_scr[kb, r0:r0 + hr, :] >= xt, 1.0, 0.0), jnp.add)

            parts.append(lax.fori_loop(0, nb, body, jnp.zeros((hr, LANES), F32)))
        return rows_all(jnp.concatenate(parts, axis=0), jnp.sum)

    above_zero = cpos >= kf
    at_zero = jnp.logical_and(jnp.logical_not(above_zero), cnn >= kf)
    n_allowed = jnp.broadcast_to(jnp.minimum(lim, nb * tk).astype(F32), (tq, LANES))
    lo0 = jnp.where(above_zero, TINY, jnp.where(at_zero, 0.0, smin))
    hi0 = jnp.where(above_zero, smax + (jnp.abs(smax) + 1e-30) * 1e-6, jnp.where(at_zero, TINY, 0.0))
    cnt0 = jnp.where(above_zero, cpos, jnp.where(at_zero, cnn, n_allowed))
    frozen = at_zero

    def any_over(cnt_lo):
        return jnp.max(jnp.where(jnp.logical_and(cnt_lo > kf, jnp.logical_not(frozen)), 1.0, 0.0)) > 0.5

    def bis_step(lo, hi, cnt_lo):
        mid = 0.5 * (lo + hi)
        cnt = count_ge(mid)
        up = jnp.logical_and(cnt >= kf, jnp.logical_not(frozen))
        dn = jnp.logical_and(cnt < kf, jnp.logical_not(frozen))
        return jnp.where(up, mid, lo), jnp.where(dn, mid, hi), jnp.where(up, cnt, cnt_lo)

    def bis_cond(c):
        return c[4]

    def bis_body(c):
        it, lo, hi, cnt_lo, _ = c
        lo, hi, cnt_lo = bis_step(*bis_step(lo, hi, cnt_lo))
        return it + 2, lo, hi, cnt_lo, jnp.logical_and(any_over(cnt_lo), it + 2 < BISECT_MAX_ITERS)

    _, lo, hi, cnt_lo, _ = lax.while_loop(bis_cond, bis_body, (jnp.int32(0), lo0, hi0, cnt0, any_over(cnt0)))
    ties = jnp.max(jnp.where(cnt_lo > kf, 1.0, 0.0)) > 0.5
    lo_t = tile_cols(lo)
    hi_t = tile_cols(hi)

    @pl.when(jnp.logical_not(ties))
    def _():
        def body(kb, carry):
            s = sc_scr[kb]
            bias_scr[kb] = jnp.where((s >= lo_t) & (s > 0.5 * NEG), 0.0, NEG)
            return carry

        lax.fori_loop(0, nb, body, 0)

    @pl.when(ties)
    def _():
        need = kf - count_ge(hi)
        tri = (lax.broadcasted_iota(I32, (LANES, LANES), 0)
               <= lax.broadcasted_iota(I32, (LANES, LANES), 1)).astype(BF16)

        def body(kb, seen):
            s = sc_scr[kb]
            cols = []
            for c in range(ncol):
                sc = s[:, c * LANES:(c + 1) * LANES]
                tie = (sc >= lo) & (sc < hi)
                tief = jnp.where(tie, 1.0, 0.0)
                incl = jnp.dot(tief.astype(BF16), tri, preferred_element_type=F32)
                take = tie & ((seen + incl - tief) < need)
                sel = ((sc >= hi) | take) & (sc > 0.5 * NEG)
                cols.append(jnp.where(sel, 0.0, NEG))
                seen = seen + incl[:, LANES - 1:]
            bias_scr[kb] = jnp.concatenate(cols, axis=1)
            return seen

        lax.fori_loop(0, nb, body, zeros)

    q = q_ref[...].astype(F32)
    gq = N_HEADS // N_KV_HEADS
    for g in range(N_KV_HEADS):
        for r in range(gq):
            h = g * gq + r
            qg_scr[g, r * tq:(r + 1) * tq, :] = q[:, h * HEAD_DIM:(h + 1) * HEAD_DIM].astype(BF16)
    m_scr[...] = jnp.full(m_scr.shape, NEG, F32)
    acc_scr[...] = jnp.zeros(acc_scr.shape, F32)

    def attn_blk(kb, carry):
        off = pl.multiple_of(kb * tk, tk)
        for g in range(N_KV_HEADS):
            kblk = k_ref[g, pl.ds(off, tk), :]
            vblk = v_ref[g, pl.ds(off, tk), :]
            s_scr[g] = lax.dot_general(qg_scr[g], kblk, nt, preferred_element_type=F32)
            for r0 in range(0, gq * tq, rb):
                rows = slice(r0, r0 + rb)
                s = s_scr[g, rows, :] + bias_scr[kb, r0 % tq:r0 % tq + rb, :]
                m_old = m_scr[g, rows, :]
                m_new = jnp.maximum(m_old, jnp.max(s, axis=1, keepdims=True))
                p_scr[g, rows, :] = jnp.exp2(s - jnp.concatenate([m_new] * ncol, axis=1)).astype(BF16)
                acc_scr[g, rows, :] = jnp.exp2(m_old - m_new) * acc_scr[g, rows, :]
                m_scr[g, rows, :] = m_new
            acc_scr[g] += jnp.dot(p_scr[g], vblk, preferred_element_type=F32)
        return carry

    lax.fori_loop(0, nb, attn_blk, 0)
    outs = []
    for g in range(N_KV_HEADS):
        for r in range(gq):
            acc = acc_scr[g, r * tq:(r + 1) * tq, :]
            outs.append(acc[:, :HEAD_DIM] / acc[:, HEAD_DIM:HEAD_DIM + 1])
    o = jnp.concatenate(outs, axis=1).astype(BF16)
    o_ref[...] = x_ref[...] + jnp.dot(o, wo_ref[...], preferred_element_type=F32)


def _dsa_attend(x2d, q, qi, wi, limit, k_hm, v_hm, ki, w_o, nkeys, *, batch, tq, tk, topk):
    rows = x2d.shape[0]
    per_batch = rows // batch
    nqt = per_batch // tq
    s_len = ki.shape[0] // batch
    nblk = s_len // tk
    gq = N_HEADS // N_KV_HEADS
    row_spec = lambda width: pl.BlockSpec((tq, width), lambda b, i, nk: (b * nqt + i, 0))
    once = pl.Buffered(1)
    k_spec = pl.BlockSpec((N_KV_HEADS, s_len, HEAD_DIM), lambda b, i, nk: (0, b, 0), pipeline_mode=once)
    v_spec = pl.BlockSpec((N_KV_HEADS, s_len, LANES), lambda b, i, nk: (0, b, 0), pipeline_mode=once)
    grid_spec = pltpu.PrefetchScalarGridSpec(
        num_scalar_prefetch=1,
        grid=(batch, nqt),
        in_specs=[row_spec(ATTN_WIDTH), row_spec(IDX_HEADS * IDX_DIM), row_spec(IDX_HEADS), row_spec(1),
                  row_spec(D_MODEL), k_spec, v_spec,
                  pl.BlockSpec((s_len, IDX_DIM), lambda b, i, nk: (b, 0), pipeline_mode=once),
                  pl.BlockSpec((ATTN_WIDTH, D_MODEL), lambda b, i, nk: (0, 0), pipeline_mode=once)],
        out_specs=row_spec(D_MODEL),
        scratch_shapes=[pltpu.VMEM((nblk, tq, tk), F32), pltpu.VMEM((nblk, tq, tk), F32),
                        pltpu.VMEM((N_KV_HEADS, gq * tq, HEAD_DIM), BF16),
                        pltpu.VMEM((N_KV_HEADS, gq * tq, tk), F32), pltpu.VMEM((N_KV_HEADS, gq * tq, tk), BF16),
                        pltpu.VMEM((N_KV_HEADS, gq * tq, LANES), F32), pltpu.VMEM((N_KV_HEADS, gq * tq, LANES), F32)],
    )
    return pl.pallas_call(
        functools.partial(_attn_body, tq=tq, tk=tk, topk=topk),
        out_shape=jax.ShapeDtypeStruct(x2d.shape, F32),
        grid_spec=grid_spec,
        compiler_params=_cparams(("arbitrary", "arbitrary")),
        name="dsa_attend",
    )(nkeys, q, qi, wi, limit, x2d, k_hm, v_hm, ki, w_o.astype(BF16))


def _round_up(x, m):
    return (x + m - 1) // m * m


def kernel(x_prompt, x_sample, state_ssm_re, state_ssm_im, cache_k, cache_v, cache_kidx, cache_conv,
           norm_mix, norm_ffn, ssm_lambda_re, ssm_lambda_im, ssm_log_dt, ssm_b_re, ssm_b_im,
           ssm_c_re, ssm_c_im, ssm_d, ssm_w_glu, ssm_b_glu, attn_w_in, attn_q_norm, attn_k_norm,
           attn_w_o, ffn_w_up, ffn_conv_w, ffn_conv_b, ffn_w_down):
    b_p, t_p, _ = x_prompt.shape
    b_s, t_s, _ = x_sample.shape
    past = cache_k.shape[2]
    xp = x_prompt.astype(F32).reshape(b_p * t_p, D_MODEL)
    xs = x_sample.astype(F32).reshape(b_s * t_s, D_MODEL)

    ops = _s5_operators(ssm_lambda_re[0], ssm_lambda_im[0], ssm_log_dt[0], ssm_b_re[0], ssm_b_im[0],
                        ssm_c_re[0], ssm_c_im[0], ssm_d[0])
    zero_state = jnp.zeros((S5_OCT, b_p, 2 * S5_HALF), F32)
    zp, htp = _s5_scan(_norm(xp, norm_mix[0]), ops, zero_state, batch=b_p, seq=t_p)
    zs, hts = _s5_scan(_norm(xs, norm_mix[0]), ops, _state_to_oct(state_ssm_re[0], state_ssm_im[0]),
                       batch=b_s, seq=t_s)
    xp = _glu_residual(zp, xp, ssm_w_glu[0], ssm_b_glu[0])
    xs = _glu_residual(zs, xs, ssm_w_glu[0], ssm_b_glu[0])
    re_p, im_p = _oct_to_state(htp, b_p)
    re_s, im_s = _oct_to_state(hts, b_s)

    ffn0 = _ffn_weights(ffn_w_up[0], ffn_conv_w[0], ffn_conv_b[0], ffn_w_down[0])
    xp, conv_p0 = _conv_ffn(xp, jnp.zeros((b_p, CONV_W - 1, D_FF), F32), norm_ffn[0], ffn0, batch=b_p, seq=t_p)
    xs, conv_s0 = _conv_ffn(xs, cache_conv[0], norm_ffn[0], ffn0, batch=b_s, seq=t_s)

    pos_p = jnp.arange(t_p, dtype=I32)
    pos_s = past + jnp.arange(t_s, dtype=I32)
    topk_p = min(TOPK_MAX, t_p // 4)
    topk_s = min(TOPK_MAX, (past + t_s) // 4)

    (q_p, khm_p, vhm_p, kf_p, vf_p, qi_p, kib_p, kif_p, wi_p) = _dsa_project(
        xp, norm_mix[1], attn_w_in[0], attn_q_norm[0], attn_k_norm[0], pos_p)
    tq_p = min(t_p, 256)
    tk_p = min(t_p, 512)
    lim_p = jnp.tile((pos_p // CHUNK + 1) * CHUNK, b_p).reshape(b_p * t_p, 1)
    q_end = (jnp.arange(t_p // tq_p, dtype=I32) + 1) * tq_p
    nk_p = jnp.minimum((q_end + tk_p - 1) // tk_p, t_p // tk_p).astype(I32)
    xp = _dsa_attend(xp, q_p, qi_p, wi_p, lim_p, khm_p, vhm_p, kib_p, attn_w_o[0], nk_p,
                     batch=b_p, tq=tq_p, tk=tk_p, topk=topk_p)

    (q_s, khm_s, vhm_s, kf_s, vf_s, qi_s, kib_s, kif_s, wi_s) = _dsa_project(
        xs, norm_mix[1], attn_w_in[0], attn_q_norm[0], attn_k_norm[0], pos_s)
    tk_s = 11 * LANES
    s_all = past + t_s
    s_pad = _round_up(s_all, tk_s)

    def with_cache(cache, new_hm, extra=None):
        heads, w = new_hm.shape[0], new_hm.shape[-1]
        c = jnp.pad(cache.astype(BF16).transpose(2, 0, 1, 3),
                    ((0, 0), (0, 0), (0, s_pad - past), (0, w - cache.shape[-1])))
        if extra is not None:
            c = c + extra
        c = lax.dynamic_update_slice(c, new_hm.reshape(heads, b_s, t_s, w), (0, 0, past, 0))
        return c.reshape(heads, b_s * s_pad, w)

    one_lane = (jnp.arange(LANES) == HEAD_DIM).astype(BF16)
    k_all = with_cache(cache_k[0], khm_s)
    v_all = with_cache(cache_v[0], vhm_s, extra=one_lane)
    ki_all = with_cache(cache_kidx[0][:, :, None, :], kib_s[None])[0]
    lim_s = jnp.full((b_s * t_s, 1), s_all, I32)
    nk_s = jnp.full((1,), s_pad // tk_s, I32)
    xs = _dsa_attend(xs, q_s, qi_s, wi_s, lim_s, k_all, v_all, ki_all, attn_w_o[0], nk_s,
                     batch=b_s, tq=t_s, tk=tk_s, topk=topk_s)

    ffn1 = _ffn_weights(ffn_w_up[1], ffn_conv_w[1], ffn_conv_b[1], ffn_w_down[1])
    xp, conv_p1 = _conv_ffn(xp, jnp.zeros((b_p, CONV_W - 1, D_FF), F32), norm_ffn[1], ffn1, batch=b_p, seq=t_p)
    xs, conv_s1 = _conv_ffn(xs, cache_conv[1], norm_ffn[1], ffn1, batch=b_s, seq=t_s)

    dt = x_prompt.dtype
    kv_shape_p = (1, b_p, t_p, N_KV_HEADS, HEAD_DIM)
    kv_shape_s = (1, b_s, t_s, N_KV_HEADS, HEAD_DIM)
    return (xp.reshape(x_prompt.shape).astype(dt), xs.reshape(x_sample.shape).astype(x_sample.dtype),
            re_p[None].astype(dt), im_p[None].astype(dt),
            re_s[None].astype(state_ssm_re.dtype), im_s[None].astype(state_ssm_im.dtype),
            kf_p.reshape(kv_shape_p).astype(dt), vf_p.reshape(kv_shape_p).astype(dt),
            kif_p.reshape(1, b_p, t_p, IDX_DIM).astype(dt),
            kf_s.reshape(kv_shape_s).astype(dt), vf_s.reshape(kv_shape_s).astype(dt),
            kif_s.reshape(1, b_s, t_s, IDX_DIM).astype(dt),
            jnp.stack([conv_p0, conv_p1]).astype(dt), jnp.stack([conv_s0, conv_s1]).astype(dt))
```

```python
import functools
import math

import jax
import jax.numpy as jnp
from jax import lax
from jax.experimental import pallas as pl
from jax.experimental.pallas import tpu as pltpu

F32 = jnp.float32
BF16 = jnp.bfloat16
I32 = jnp.int32

D_MODEL = 1024
CHUNK = 64
SSM_GROUP = 16
SSM_GROUPS = D_MODEL // SSM_GROUP
SSM_STATE = 64
N_HEADS = 16
N_KV_HEADS = 4
HEAD_DIM = 64
ATTN_WIDTH = N_HEADS * HEAD_DIM
KV_WIDTH = N_KV_HEADS * HEAD_DIM
IDX_HEADS = 8
IDX_DIM = 64
TOPK_MAX = 256
ROT_DIM = HEAD_DIM // 4
ROPE_THETA = 500000.0
OFF_K = ATTN_WIDTH
OFF_V = OFF_K + KV_WIDTH
OFF_QI = OFF_V + KV_WIDTH
OFF_KI = OFF_QI + IDX_HEADS * IDX_DIM
OFF_WI = OFF_KI + IDX_DIM
IN_COLS = OFF_WI + IDX_HEADS
D_FF = 2816
CONV_W = 3
EPS = 1e-6
NEG = -1e30

LANES = 128
SUBLANES = 8
VMEM_LIMIT_BYTES = 56 * 1024 * 1024

S5_L = 16
S5_OCT = D_MODEL // LANES
S5_OCT_GROUPS = LANES // SSM_GROUP
S5_HALF = S5_OCT_GROUPS * SSM_STATE
S5_PW_ROWS = 24
FFN_TM = 1024
FFN_FC = 256
FFN_NC = D_FF // FFN_FC
assert FFN_NC % 2 == 1
IN_COLS_PAD = 2176
LOG2E = math.log2(math.e)
BISECT_MAX_ITERS = 40
ATTN_BLOCK_VREGS = 32
TINY = 1.1754944e-38
COUNT_ROWS = 128


def _cparams(sem):
    return pltpu.CompilerParams(dimension_semantics=sem, vmem_limit_bytes=VMEM_LIMIT_BYTES)


def _rms(x, g):
    ms = jnp.mean(x * x, axis=-1, keepdims=True)
    return x * lax.rsqrt(ms + EPS) * g


def _norm_body(x_ref, g_ref, o_ref):
    o_ref[...] = _rms(x_ref[...], g_ref[...])


def _norm(x2d, g):
    rows = x2d.shape[0]
    tm = min(rows, 1024)
    return pl.pallas_call(
        _norm_body,
        out_shape=jax.ShapeDtypeStruct((rows, D_MODEL), F32),
        grid=(rows // tm,),
        in_specs=[pl.BlockSpec((tm, D_MODEL), lambda i: (i, 0)),
                  pl.BlockSpec((1, D_MODEL), lambda i: (0, 0))],
        out_specs=pl.BlockSpec((tm, D_MODEL), lambda i: (i, 0)),
        compiler_params=_cparams(("arbitrary",)),
        name="rmsnorm",
    )(x2d, g.astype(F32).reshape(1, D_MODEL))


def _s5_operators(lam_re, lam_im, log_dt, b_re, b_im, c_re, c_im, d_skip):
    L = S5_L
    lre = jnp.minimum(lam_re.astype(F32), -1e-4)
    lim = lam_im.astype(F32)
    dt = jnp.exp(log_dt.astype(F32))[:, None]
    n = jnp.arange(L + 1, dtype=F32)[:, None, None]
    mag = jnp.exp(n * (lre * dt)[None])
    ang = n * (lim * dt)[None]
    pr = mag * jnp.cos(ang)
    pi = mag * jnp.sin(ang)
    a_re, a_im = pr[1], pi[1]
    den = lre * lre + lim * lim
    n_re = a_re - 1.0
    f_re = (n_re * lre + a_im * lim) / den
    f_im = (a_im * lre - n_re * lim) / den
    br = b_re.astype(F32)
    bi = b_im.astype(F32)
    bb_re = f_re[..., None] * br - f_im[..., None] * bi
    bb_im = f_re[..., None] * bi + f_im[..., None] * br
    cr = c_re.astype(F32)
    ci = c_im.astype(F32)
    eye = jnp.eye(S5_OCT_GROUPS, dtype=F32)
    og = (S5_OCT, S5_OCT_GROUPS)

    tg = lambda m: jnp.moveaxis(m, 0, -1)
    crt, cit = tg(cr)[None, :, :, None], tg(ci)[None, :, :, None]
    prt = tg(pr[:L].transpose(1, 0, 2))[:, None, :, None]
    pit = tg(pi[:L].transpose(1, 0, 2))[:, None, :, None]
    brt, bit = tg(bb_re)[None, None], tg(bb_im)[None, None]
    kl = jnp.sum((crt * prt - cit * pit) * brt - (crt * pit + cit * prt) * bit, axis=2)
    krow = kl.transpose(0, 3, 2, 1).reshape(L, S5_OCT, LANES, SSM_GROUP)
    lane_group = jnp.arange(LANES) // SSM_GROUP
    same_group = (lane_group[:, None] == lane_group[None, :]).astype(F32)
    kbd = jnp.tile(krow, (1, 1, 1, S5_OCT_GROUPS)) * same_group
    kpad = jnp.concatenate([jnp.zeros_like(kbd[:1]), kbd], axis=0)
    k2 = []
    for dlt in range(L // 2):
        top = jnp.concatenate([kpad[2 * dlt + 1], kpad[2 * dlt + 2]], axis=-1)
        bot = jnp.concatenate([kpad[2 * dlt], kpad[2 * dlt + 1]], axis=-1)
        k2.append(jnp.concatenate([top, bot], axis=-2))
    k2 = jnp.stack(k2, axis=1).astype(BF16)

    def bd_in(m):
        m = m.reshape(og + (SSM_STATE, SSM_GROUP))
        return jnp.einsum("jgpc,gh->jgchp", m, eye).reshape(S5_OCT, LANES, S5_HALF)

    ab_re = a_re[..., None] * bb_re - a_im[..., None] * bb_im
    ab_im = a_re[..., None] * bb_im + a_im[..., None] * bb_re
    f2 = jnp.concatenate([
        jnp.concatenate([bd_in(ab_re), bd_in(ab_im)], axis=-1),
        jnp.concatenate([bd_in(bb_re), bd_in(bb_im)], axis=-1)], axis=-2).astype(BF16)

    def bd_out(m):
        m = m.reshape(og + (SSM_GROUP, SSM_STATE))
        return jnp.einsum("jgcp,gh->jgphc", m, eye).reshape(S5_OCT, S5_HALF, LANES)

    c1r = cr * a_re[:, None, :] - ci * a_im[:, None, :]
    c1i = cr * a_im[:, None, :] + ci * a_re[:, None, :]
    c2 = jnp.concatenate([
        jnp.concatenate([bd_out(cr), bd_out(c1r)], axis=-1),
        jnp.concatenate([-bd_out(ci), -bd_out(c1i)], axis=-1)], axis=-2).astype(BF16)

    def oct_lanes(p):
        return p.reshape((L + 1,) + og + (SSM_STATE,)).transpose(1, 0, 2, 3).reshape(S5_OCT, L + 1, S5_HALF)

    pw = jnp.concatenate([oct_lanes(pr), oct_lanes(pi)], axis=-1)
    pw = jnp.pad(pw, ((0, 0), (0, S5_PW_ROWS - (L + 1)), (0, 0)))
    dsk = d_skip.astype(F32).reshape(S5_OCT, 1, LANES)
    dsk2 = jnp.concatenate([dsk, dsk], axis=-1)
    return f2, k2, c2, pw, dsk2


def _s5_body(u_ref, f2_ref, k2_ref, c2_ref, pw_ref, dsk_ref, h0_ref, z_ref, ht_ref, v_scr, hp_scr, *, bpt, nck):
    L = S5_L
    H = S5_HALF
    tr = bpt * nck

    def cmul(xr, xi, n):
        pr = pw_ref[0, n:n + 1, :H]
        pi = pw_ref[0, n:n + 1, H:]
        return xr * pr - xi * pi, xr * pi + xi * pr

    ufp = [jnp.concatenate([u_ref[pl.ds(2 * s, tr, stride=L), :], u_ref[pl.ds(2 * s + 1, tr, stride=L), :]], axis=1)
           for s in range(L // 2)]
    upair = [u.astype(BF16) for u in ufp]

    vr = vi = None
    for s2 in range(L // 2):
        p = jnp.dot(upair[s2], f2_ref[0], preferred_element_type=F32)
        qr, qi = cmul(p[:, :H], p[:, H:], L - 2 - 2 * s2)
        vr = qr if vr is None else vr + qr
        vi = qi if vi is None else vi + qi
    v_scr[:, :H] = vr
    v_scr[:, H:] = vi

    alr = pw_ref[0, L:L + 1, :H]
    ali = pw_ref[0, L:L + 1, H:]
    for b in range(bpt):
        def step(k, carry, b=b):
            hr, hi = carry
            row = b * nck + k
            hp_scr[pl.ds(row, 1), :H] = hr
            hp_scr[pl.ds(row, 1), H:] = hi
            wr = v_scr[pl.ds(row, 1), :H]
            wi = v_scr[pl.ds(row, 1), H:]
            return hr * alr - hi * ali + wr, hr * ali + hi * alr + wi

        h0r = h0_ref[0, 0, b:b + 1, :H]
        h0i = h0_ref[0, 0, b:b + 1, H:]
        hr, hi = lax.fori_loop(0, nck, step, (h0r, h0i))
        ht_ref[0, 0, b:b + 1, :H] = hr
        ht_ref[0, 0, b:b + 1, H:] = hi

    hpr = hp_scr[:, :H]
    hpi = hp_scr[:, H:]
    dsk = dsk_ref[0]
    for t2 in range(L // 2):
        gr, gi = cmul(hpr, hpi, 2 * t2 + 1)
        g = jnp.concatenate([gr, gi], axis=1).astype(BF16)
        y = jnp.dot(g, c2_ref[0], preferred_element_type=F32)
        for s2 in range(t2 + 1):
            y = y + jnp.dot(upair[s2], k2_ref[0, t2 - s2], preferred_element_type=F32)
        z = jax.nn.gelu(y + dsk * ufp[t2])
        z_ref[pl.ds(2 * t2, tr, stride=L), :] = z[:, :LANES]
        z_ref[pl.ds(2 * t2 + 1, tr, stride=L), :] = z[:, LANES:]


def _s5_scan(u2d, ops, h0, *, batch, seq):
    L = S5_L
    nck = seq // L
    n_rows = batch * nck
    bpt = 1 if nck % SUBLANES == 0 else batch
    tr = bpt * nck
    n_tiles = n_rows // tr
    h0 = h0.reshape(S5_OCT, n_tiles, bpt, 2 * S5_HALF)
    tok_spec = pl.BlockSpec((tr * L, LANES), lambda j, i: (i, j))
    per_oct = lambda a: pl.BlockSpec((1,) + a.shape[1:], lambda j, i: (j,) + (0,) * (a.ndim - 1))
    state_spec = pl.BlockSpec((1, 1, bpt, 2 * S5_HALF), lambda j, i: (j, i, 0, 0))
    z, ht = pl.pallas_call(
        functools.partial(_s5_body, bpt=bpt, nck=nck),
        out_shape=(jax.ShapeDtypeStruct(u2d.shape, F32),
                   jax.ShapeDtypeStruct((S5_OCT, n_tiles, bpt, 2 * S5_HALF), F32)),
        grid=(S5_OCT, n_tiles),
        in_specs=[tok_spec] + [per_oct(a) for a in ops] + [state_spec],
        out_specs=(tok_spec, state_spec),
        scratch_shapes=[pltpu.VMEM((tr, 2 * S5_HALF), F32), pltpu.VMEM((tr, 2 * S5_HALF), F32)],
        compiler_params=_cparams(("arbitrary", "arbitrary")),
        name="s5_scan",
    )(u2d, *ops, h0)
    return z, ht


def _state_to_oct(s_re, s_im):
    b = s_re.shape[0]
    r = s_re.astype(F32).reshape(b, S5_OCT, S5_HALF).transpose(1, 0, 2)
    i = s_im.astype(F32).reshape(b, S5_OCT, S5_HALF).transpose(1, 0, 2)
    return jnp.concatenate([r, i], axis=-1)


def _oct_to_state(ht, batch):
    ht = ht.reshape(S5_OCT, batch, 2 * S5_HALF)
    r = ht[..., :S5_HALF].transpose(1, 0, 2).reshape(batch, SSM_GROUPS, SSM_STATE)
    i = ht[..., S5_HALF:].transpose(1, 0, 2).reshape(batch, SSM_GROUPS, SSM_STATE)
    return r, i


def _glu_body(z_ref, x_ref, w_ref, b_ref, o_ref):
    g = jnp.dot(z_ref[...].astype(BF16), w_ref[...], preferred_element_type=F32) + b_ref[...]
    o_ref[...] = x_ref[...] + g[:, :D_MODEL] * jax.nn.sigmoid(g[:, D_MODEL:])


def _glu_residual(z, x2d, w_glu, b_glu):
    rows = x2d.shape[0]
    tm = min(rows, 512)
    row_spec = pl.BlockSpec((tm, D_MODEL), lambda i: (i, 0))
    return pl.pallas_call(
        _glu_body,
        out_shape=jax.ShapeDtypeStruct(x2d.shape, F32),
        grid=(rows // tm,),
        in_specs=[row_spec, row_spec,
                  pl.BlockSpec((D_MODEL, 2 * D_MODEL), lambda i: (0, 0)),
                  pl.BlockSpec((1, 2 * D_MODEL), lambda i: (0, 0))],
        out_specs=row_spec,
        compiler_params=_cparams(("arbitrary",)),
        name="glu_residual",
    )(z, x2d, w_glu.astype(BF16), b_glu.astype(F32).reshape(1, 2 * D_MODEL))


def _ffn_body(x_ref, xh_ref, cs_ref, g_ref, wa_ref, wb_ref, cw_ref, cb_ref, wd_ref,
              o_ref, co_ref, xn_scr, ha0, hb0, hh0, ha1, hb1, hh1, gt0, gt1, acc_scr, *, tm, tiles_per_seq):
    i = pl.program_id(0)
    first = (i % tiles_per_seq) == 0
    g = g_ref[...]
    xn_scr[...] = _rms(x_ref[...], g).astype(BF16)
    xnh = _rms(xh_ref[...], g).astype(BF16)
    rows = lax.broadcasted_iota(I32, (tm, FFN_FC), 0)
    sets = ((ha0, hb0, hh0, gt0), (ha1, hb1, hh1, gt1))

    def up(c, s):
        ha, hb, hh, _ = sets[s]
        xn = xn_scr[...]
        ha[...] = jnp.dot(xn, wa_ref[0, c], preferred_element_type=F32)
        hb[...] = jnp.dot(xn, wb_ref[0, c], preferred_element_type=F32)
        hh[...] = jnp.dot(xnh, wa_ref[0, c], preferred_element_type=F32)

    def act(c, s):
        ha_ref, hb_ref, hh_ref, gt = sets[s]
        ha = ha_ref[...]
        hh = jnp.where(first, cs_ref[0, c], hh_ref[...])
        co_ref[0, c] = ha[tm - SUBLANES:, :]
        h6 = hh[6:7, :]
        h7 = hh[7:8, :]
        prev1 = jnp.where(rows == 0, h7, pltpu.roll(ha, 1, 0))
        prev2 = jnp.where(rows == 0, h6, jnp.where(rows == 1, h7, pltpu.roll(ha, 2, 0)))
        cw = cw_ref[c]
        cv = cb_ref[c] + cw[0:1, :] * prev2 + cw[1:2, :] * prev1 + cw[2:3, :] * ha
        gt[...] = (jax.nn.gelu(cv) * hb_ref[...]).astype(BF16)

    def down(c, s):
        acc_scr[...] += jnp.dot(sets[s][3][...], wd_ref[c], preferred_element_type=F32)

    acc_scr[...] = jnp.zeros(acc_scr.shape, F32)
    gt1[...] = jnp.zeros(gt1.shape, BF16)
    up(0, 0)

    def body(k, carry):
        c = 2 * k
        up(c + 1, 1)
        act(c, 0)
        down(jnp.maximum(c - 1, 0), 1)
        up(c + 2, 0)
        act(c + 1, 1)
        down(c, 0)
        return carry

    lax.fori_loop(0, (FFN_NC - 1) // 2, body, 0)
    act(FFN_NC - 1, 0)
    down(FFN_NC - 2, 1)
    down(FFN_NC - 1, 0)
    o_ref[...] = x_ref[...] + acc_scr[...]


def _ffn_weights(w_up, conv_w, conv_b, w_down):
    wab = w_up.astype(BF16).reshape(D_MODEL, 2, FFN_NC, FFN_FC).transpose(1, 2, 0, 3)
    cw = jnp.pad(conv_w.astype(F32), ((0, SUBLANES - CONV_W), (0, 0)))
    cw = cw.reshape(SUBLANES, FFN_NC, FFN_FC).transpose(1, 0, 2)
    cb = conv_b.astype(F32).reshape(FFN_NC, 1, FFN_FC)
    wd = w_down.astype(BF16).reshape(FFN_NC, FFN_FC, D_MODEL)
    return wab, cw, cb, wd


def _conv_ffn(x2d, conv_state, g, weights, *, batch, seq):
    wab, cw, cb, wd = weights
    tm = min(seq, FFN_TM)
    tiles_per_seq = seq // tm
    cs = jnp.pad(conv_state.astype(F32), ((0, 0), (SUBLANES - (CONV_W - 1), 0), (0, 0)))
    cs = cs.reshape(batch, SUBLANES, FFN_NC, FFN_FC).transpose(0, 2, 1, 3)
    cs_spec = pl.BlockSpec((1, FFN_NC, SUBLANES, FFN_FC), lambda i: (i // tiles_per_seq, 0, 0, 0))
    hb = tm // SUBLANES
    const3 = lambda i: (0, 0, 0)
    up_set = [pltpu.VMEM((tm, FFN_FC), F32), pltpu.VMEM((tm, FFN_FC), F32), pltpu.VMEM((SUBLANES, FFN_FC), F32)]
    gate = pltpu.VMEM((tm, FFN_FC), BF16)
    scratch = [pltpu.VMEM((tm, D_MODEL), BF16)] + up_set + up_set + [gate, gate, pltpu.VMEM((tm, D_MODEL), F32)]
    out, co = pl.pallas_call(
        functools.partial(_ffn_body, tm=tm, tiles_per_seq=tiles_per_seq),
        out_shape=(jax.ShapeDtypeStruct(x2d.shape, F32),
                   jax.ShapeDtypeStruct((batch, FFN_NC, SUBLANES, FFN_FC), F32)),
        grid=(batch * tiles_per_seq,),
        in_specs=[pl.BlockSpec((tm, D_MODEL), lambda i: (i, 0)),
                  pl.BlockSpec((SUBLANES, D_MODEL), lambda i: (jnp.maximum(i * hb - 1, 0), 0)),
                  cs_spec,
                  pl.BlockSpec((1, D_MODEL), lambda i: (0, 0)),
                  pl.BlockSpec((1, FFN_NC, D_MODEL, FFN_FC), lambda i: (0, 0, 0, 0),
                               pipeline_mode=pl.Buffered(1)),
                  pl.BlockSpec((1, FFN_NC, D_MODEL, FFN_FC), lambda i: (1, 0, 0, 0),
                               pipeline_mode=pl.Buffered(1)),
                  pl.BlockSpec((FFN_NC, SUBLANES, FFN_FC), const3),
                  pl.BlockSpec((FFN_NC, 1, FFN_FC), const3),
                  pl.BlockSpec((FFN_NC, FFN_FC, D_MODEL), const3, pipeline_mode=pl.Buffered(1))],
        out_specs=(pl.BlockSpec((tm, D_MODEL), lambda i: (i, 0)),
                   cs_spec),
        scratch_shapes=scratch,
        compiler_params=_cparams(("arbitrary",)),
        name="conv_ffn",
    )(x2d, x2d, cs, g.astype(F32).reshape(1, D_MODEL), wab, wab, cw, cb, wd)
    new_state = co[:, :, SUBLANES - (CONV_W - 1):, :].transpose(0, 2, 1, 3).reshape(batch, CONV_W - 1, D_FF)
    return out, new_state


def _rope_tables(pos):
    half = ROT_DIM // 2
    inv = ROPE_THETA ** (-jnp.arange(half, dtype=F32) / half)
    ang = pos.astype(F32)[:, None] * inv[None, :]
    cos = jnp.cos(ang)
    sin = jnp.sin(ang)
    lane = jnp.arange(LANES) % HEAD_DIM
    idx = lane % half
    cc = jnp.where(lane[None] < ROT_DIM, cos[:, idx], 1.0)
    s1 = jnp.where((lane[None] >= half) & (lane[None] < ROT_DIM), sin[:, idx], 0.0)
    s2 = jnp.where(lane[None] < half, -sin[:, idx], 0.0)
    return cc.astype(F32), s1.astype(F32), s2.astype(F32)


def _proj_body(x_ref, g_ref, w_ref, qn_ref, kn_ref, cc_ref, s1_ref, s2_ref, bd_ref,
               q_ref, khm_ref, vhm_ref, kf_ref, vf_ref, qi_ref, kib_ref, kif_ref, wi_ref):
    xn = _rms(x_ref[...], g_ref[...]).astype(BF16)
    proj = jnp.dot(xn, w_ref[...], preferred_element_type=F32)
    cc = cc_ref[...]
    s1 = s1_ref[...]
    s2 = s2_ref[...]
    bd = bd_ref[...]

    def rope(y):
        return y * cc + pltpu.roll(y, ROT_DIM // 2, 1) * s1 + pltpu.roll(y, LANES - ROT_DIM // 2, 1) * s2

    def head_norm(y, gain):
        ms = jnp.dot((y * y).astype(BF16), bd, preferred_element_type=F32)
        return y * lax.rsqrt(ms + EPS) * gain

    qn = qn_ref[...]
    kn = kn_ref[...]
    scale = HEAD_DIM ** -0.5 * LOG2E
    one_col = jnp.where(lax.broadcasted_iota(I32, (x_ref.shape[0], HEAD_DIM), 1) == 0, 1.0, 0.0)
    for c in range(ATTN_WIDTH // LANES):
        y = proj[:, c * LANES:(c + 1) * LANES]
        q_ref[:, c * LANES:(c + 1) * LANES] = (rope(head_norm(y, qn)) * scale).astype(BF16)
    for c in range(KV_WIDTH // LANES):
        y = proj[:, OFF_K + c * LANES:OFF_K + (c + 1) * LANES]
        k = rope(head_norm(y, kn))
        kf_ref[:, c * LANES:(c + 1) * LANES] = k
        khm_ref[2 * c] = k[:, :HEAD_DIM].astype(BF16)
        khm_ref[2 * c + 1] = k[:, HEAD_DIM:].astype(BF16)
        v = proj[:, OFF_V + c * LANES:OFF_V + (c + 1) * LANES]
        vf_ref[:, c * LANES:(c + 1) * LANES] = v
        vhm_ref[2 * c] = jnp.concatenate([v[:, :HEAD_DIM], one_col], axis=1).astype(BF16)
        vhm_ref[2 * c + 1] = jnp.concatenate([v[:, HEAD_DIM:], one_col], axis=1).astype(BF16)
    iscale = IDX_DIM ** -0.5
    for c in range(IDX_HEADS * IDX_DIM // LANES):
        y = proj[:, OFF_QI + c * LANES:OFF_QI + (c + 1) * LANES]
        qi_ref[:, c * LANES:(c + 1) * LANES] = (rope(y) * iscale).astype(BF16)
    tail = proj[:, OFF_KI:OFF_KI + LANES]
    ki = rope(tail)[:, :IDX_DIM]
    kif_ref[...] = ki
    kib_ref[...] = ki.astype(BF16)
    wi_ref[...] = tail[:, IDX_DIM:IDX_DIM + IDX_HEADS] * (IDX_HEADS ** -0.5)


def _dsa_project(x2d, g, w_in, q_norm, k_norm, pos):
    rows = x2d.shape[0]
    tm = min(rows, 512)
    w = jnp.pad(w_in.astype(BF16), ((0, 0), (0, IN_COLS_PAD - IN_COLS)))
    if pos.shape[0] < tm:
        pos = jnp.tile(pos, tm // pos.shape[0])
    ntab = pos.shape[0] // tm
    cc, s1, s2 = _rope_tables(pos)
    lane = jnp.arange(LANES)
    bd = jnp.where((lane[:, None] // HEAD_DIM) == (lane[None, :] // HEAD_DIM), 1.0 / HEAD_DIM, 0.0).astype(BF16)
    qn = jnp.tile(q_norm.astype(F32), LANES // HEAD_DIM).reshape(1, LANES)
    kn = jnp.tile(k_norm.astype(F32), LANES // HEAD_DIM).reshape(1, LANES)
    row_spec = lambda width: pl.BlockSpec((tm, width), lambda i: (i, 0))
    const = lambda shape: pl.BlockSpec(shape, lambda i: (0,) * len(shape))
    hm_spec = pl.BlockSpec((N_KV_HEADS, tm, HEAD_DIM), lambda i: (0, i, 0))
    vhm_spec = pl.BlockSpec((N_KV_HEADS, tm, LANES), lambda i: (0, i, 0))
    tab_spec = pl.BlockSpec((tm, LANES), lambda i: (i % ntab, 0))
    return pl.pallas_call(
        _proj_body,
        out_shape=(jax.ShapeDtypeStruct((rows, ATTN_WIDTH), BF16),
                   jax.ShapeDtypeStruct((N_KV_HEADS, rows, HEAD_DIM), BF16),
                   jax.ShapeDtypeStruct((N_KV_HEADS, rows, LANES), BF16),
                   jax.ShapeDtypeStruct((rows, KV_WIDTH), F32),
                   jax.ShapeDtypeStruct((rows, KV_WIDTH), F32),
                   jax.ShapeDtypeStruct((rows, IDX_HEADS * IDX_DIM), BF16),
                   jax.ShapeDtypeStruct((rows, IDX_DIM), BF16),
                   jax.ShapeDtypeStruct((rows, IDX_DIM), F32),
                   jax.ShapeDtypeStruct((rows, IDX_HEADS), F32)),
        grid=(rows // tm,),
        in_specs=[row_spec(D_MODEL), const((1, D_MODEL)), const((D_MODEL, IN_COLS_PAD)),
                  const((1, LANES)), const((1, LANES)),
                  tab_spec, tab_spec, tab_spec, const((LANES, LANES))],
        out_specs=(row_spec(ATTN_WIDTH), hm_spec, vhm_spec, row_spec(KV_WIDTH), row_spec(KV_WIDTH),
                   row_spec(IDX_HEADS * IDX_DIM), row_spec(IDX_DIM), row_spec(IDX_DIM), row_spec(IDX_HEADS)),
        compiler_params=_cparams(("arbitrary",)),
        name="dsa_project",
    )(x2d, g.astype(F32).reshape(1, D_MODEL), w, qn, kn, cc, s1, s2, bd)


def _attn_body(nk_ref, q_ref, qi_ref, wi_ref, lim_ref, x_ref, k_ref, v_ref, ki_ref, wo_ref,
               o_ref, sc_scr, bias_scr, qg_scr, s_scr, p_scr, m_scr, acc_scr, *, tq, tk, topk):
    i = pl.program_id(1)
    nb = nk_ref[i]
    ncol = tk // LANES
    kf = float(topk)
    qi = qi_ref[...].astype(F32)
    wi = wi_ref[...]
    lim = lim_ref[...]
    qih = [qi[:, h * IDX_DIM:(h + 1) * IDX_DIM].astype(BF16) for h in range(IDX_HEADS)]
    wih = [wi[:, h:h + 1] for h in range(IDX_HEADS)]
    nt = (((1,), (1,)), ((), ()))
    rb = min(tq, max(SUBLANES, ATTN_BLOCK_VREGS // ncol * SUBLANES))
    hr = min(tq, COUNT_ROWS)
    lane_tk = lax.broadcasted_iota(I32, (tq, tk), 1)

    def fold(x, op):
        part = x[:, :LANES]
        for c in range(1, ncol):
            part = op(part, x[:, c * LANES:(c + 1) * LANES])
        return part

    def rows_all(x, op):
        return jnp.broadcast_to(op(x, axis=1, keepdims=True), x.shape)

    def tile_cols(x):
        return jnp.concatenate([x] * ncol, axis=1)

    def score_blk(kb, carry, masked):
        mn, mx, cpos, cnn = carry
        off = pl.multiple_of(kb * tk, tk)
        kib = ki_ref[pl.ds(off, tk), :]
        sc = jnp.zeros((tq, tk), F32)
        for h in range(IDX_HEADS):
            lg = lax.dot_general(qih[h], kib, nt, preferred_element_type=F32)
            sc = sc + jnp.maximum(lg, 0.0) * wih[h]
        if masked:
            allowed = lane_tk + off < lim
            scm = jnp.where(allowed, sc, NEG)
            mn = jnp.minimum(mn, fold(jnp.where(allowed, sc, -NEG), jnp.minimum))
        else:
            scm = sc
            mn = jnp.minimum(mn, fold(sc, jnp.minimum))
        sc_scr[kb] = scm
        mx = jnp.maximum(mx, fold(scm, jnp.maximum))
        cpos = cpos + fold(jnp.where(scm > 0.0, 1.0, 0.0), jnp.add)
        cnn = cnn + fold(jnp.where(scm >= 0.0, 1.0, 0.0), jnp.add)
        return mn, mx, cpos, cnn

    zeros = jnp.zeros((tq, LANES), F32)
    stats = lax.fori_loop(0, nb - 1, functools.partial(score_blk, masked=False),
                          (jnp.full((tq, LANES), -NEG, F32), jnp.full((tq, LANES), NEG, F32), zeros, zeros))
    mn, mx, cpos, cnn = score_blk(nb - 1, stats, masked=True)
    smin = rows_all(mn, jnp.min)
    smax = rows_all(mx, jnp.max)
    cpos = rows_all(cpos, jnp.sum)
    cnn = rows_all(cnn, jnp.sum)

    def count_ge(x):
        parts = []
        for r0 in range(0, tq, hr):
            xt = tile_cols(x[r0:r0 + hr])

            def body(kb, acc, r0=r0, xt=xt):
                return acc + fold(jnp.where(sc_scr[kb, r0:r0 + hr, :] >= xt, 1.0, 0.0), jnp.add)

            parts.append(lax.fori_loop(0, nb, body, jnp.zeros((hr, LANES), F32)))
        return rows_all(jnp.concatenate(parts, axis=0), jnp.sum)

    above_zero = cpos >= kf
    at_zero = jnp.logical_and(jnp.logical_not(above_zero), cnn >= kf)
    n_allowed = jnp.broadcast_to(jnp.minimum(lim, nb * tk).astype(F32), (tq, LANES))
    lo0 = jnp.where(above_zero, TINY, jnp.where(at_zero, 0.0, smin))
    hi0 = jnp.where(above_zero, smax + (jnp.abs(smax) + 1e-30) * 1e-6, jnp.where(at_zero, TINY, 0.0))
    cnt0 = jnp.where(above_zero, cpos, jnp.where(at_zero, cnn, n_allowed))
    frozen = at_zero

    def any_over(cnt_lo):
        return jnp.max(jnp.where(jnp.logical_and(cnt_lo > kf, jnp.logical_not(frozen)), 1.0, 0.0)) > 0.5

    def bis_step(lo, hi, cnt_lo):
        mid = 0.5 * (lo + hi)
        cnt = count_ge(mid)
        up = jnp.logical_and(cnt >= kf, jnp.logical_not(frozen))
        dn = jnp.logical_and(cnt < kf, jnp.logical_not(frozen))
        return jnp.where(up, mid, lo), jnp.where(dn, mid, hi), jnp.where(up, cnt, cnt_lo)

    def bis_cond(c):
        return c[4]

    def bis_body(c):
        it, lo, hi, cnt_lo, _ = c
        lo, hi, cnt_lo = bis_step(*bis_step(lo, hi, cnt_lo))
        return it + 2, lo, hi, cnt_lo, jnp.logical_and(any_over(cnt_lo), it + 2 < BISECT_MAX_ITERS)

    _, lo, hi, cnt_lo, _ = lax.while_loop(bis_cond, bis_body, (jnp.int32(0), lo0, hi0, cnt0, any_over(cnt0)))
    ties = jnp.max(jnp.where(cnt_lo > kf, 1.0, 0.0)) > 0.5
    lo_t = tile_cols(lo)
    hi_t = tile_cols(hi)

    @pl.when(jnp.logical_not(ties))
    def _():
        def body(kb, carry):
            s = sc_scr[kb]
            bias_scr[kb] = jnp.where((s >= lo_t) & (s > 0.5 * NEG), 0.0, NEG)
            return carry

        lax.fori_loop(0, nb, body, 0)

    @pl.when(ties)
    def _():
        need = kf - count_ge(hi)
        tri = (lax.broadcasted_iota(I32, (LANES, LANES), 0)
               <= lax.broadcasted_iota(I32, (LANES, LANES), 1)).astype(BF16)

        def body(kb, seen):
            s = sc_scr[kb]
            cols = []
            for c in range(ncol):
                sc = s[:, c * LANES:(c + 1) * LANES]
                tie = (sc >= lo) & (sc < hi)
                tief = jnp.where(tie, 1.0, 0.0)
                incl = jnp.dot(tief.astype(BF16), tri, preferred_element_type=F32)
                take = tie & ((seen + incl - tief) < need)
                sel = ((sc >= hi) | take) & (sc > 0.5 * NEG)
                cols.append(jnp.where(sel, 0.0, NEG))
                seen = seen + incl[:, LANES - 1:]
            bias_scr[kb] = jnp.concatenate(cols, axis=1)
            return seen

        lax.fori_loop(0, nb, body, zeros)

    q = q_ref[...].astype(F32)
    gq = N_HEADS // N_KV_HEADS
    for g in range(N_KV_HEADS):
        for r in range(gq):
            h = g * gq + r
            qg_scr[g, r * tq:(r + 1) * tq, :] = q[:, h * HEAD_DIM:(h + 1) * HEAD_DIM].astype(BF16)
    m_scr[...] = jnp.full(m_scr.shape, NEG, F32)
    acc_scr[...] = jnp.zeros(acc_scr.shape, F32)

    def attn_blk(kb, carry):
        off = pl.multiple_of(kb * tk, tk)
        for g in range(N_KV_HEADS):
            kblk = k_ref[g, pl.ds(off, tk), :]
            vblk = v_ref[g, pl.ds(off, tk), :]
            s_scr[g] = lax.dot_general(qg_scr[g], kblk, nt, preferred_element_type=F32)
            for r0 in range(0, gq * tq, rb):
                rows = slice(r0, r0 + rb)
                s = s_scr[g, rows, :] + bias_scr[kb, r0 % tq:r0 % tq + rb, :]
                m_old = m_scr[g, rows, :]
                m_new = jnp.maximum(m_old, jnp.max(s, axis=1, keepdims=True))
                p_scr[g, rows, :] = jnp.exp2(s - jnp.concatenate([m_new] * ncol, axis=1)).astype(BF16)
                acc_scr[g, rows, :] = jnp.exp2(m_old - m_new) * acc_scr[g, rows, :]
                m_scr[g, rows, :] = m_new
            acc_scr[g] += jnp.dot(p_scr[g], vblk, preferred_element_type=F32)
        return carry

    lax.fori_loop(0, nb, attn_blk, 0)
    outs = []
    for g in range(N_KV_HEADS):
        for r in range(gq):
            acc = acc_scr[g, r * tq:(r + 1) * tq, :]
            outs.append(acc[:, :HEAD_DIM] / acc[:, HEAD_DIM:HEAD_DIM + 1])
    o = jnp.concatenate(outs, axis=1).astype(BF16)
    o_ref[...] = x_ref[...] + jnp.dot(o, wo_ref[...], preferred_element_type=F32)


def _dsa_attend(x2d, q, qi, wi, limit, k_hm, v_hm, ki, w_o, nkeys, *, batch, tq, tk, topk):
    rows = x2d.shape[0]
    per_batch = rows // batch
    nqt = per_batch // tq
    s_len = ki.shape[0] // batch
    nblk = s_len // tk
    gq = N_HEADS // N_KV_HEADS
    row_spec = lambda width: pl.BlockSpec((tq, width), lambda b, i, nk: (b * nqt + i, 0))
    once = pl.Buffered(1)
    k_spec = pl.BlockSpec((N_KV_HEADS, s_len, HEAD_DIM), lambda b, i, nk: (0, b, 0), pipeline_mode=once)
    v_spec = pl.BlockSpec((N_KV_HEADS, s_len, LANES), lambda b, i, nk: (0, b, 0), pipeline_mode=once)
    grid_spec = pltpu.PrefetchScalarGridSpec(
        num_scalar_prefetch=1,
        grid=(batch, nqt),
        in_specs=[row_spec(ATTN_WIDTH), row_spec(IDX_HEADS * IDX_DIM), row_spec(IDX_HEADS), row_spec(1),
                  row_spec(D_MODEL), k_spec, v_spec,
                  pl.BlockSpec((s_len, IDX_DIM), lambda b, i, nk: (b, 0), pipeline_mode=once),
                  pl.BlockSpec((ATTN_WIDTH, D_MODEL), lambda b, i, nk: (0, 0), pipeline_mode=once)],
        out_specs=row_spec(D_MODEL),
        scratch_shapes=[pltpu.VMEM((nblk, tq, tk), F32), pltpu.VMEM((nblk, tq, tk), F32),
                        pltpu.VMEM((N_KV_HEADS, gq * tq, HEAD_DIM), BF16),
                        pltpu.VMEM((N_KV_HEADS, gq * tq, tk), F32), pltpu.VMEM((N_KV_HEADS, gq * tq, tk), BF16),
                        pltpu.VMEM((N_KV_HEADS, gq * tq, LANES), F32), pltpu.VMEM((N_KV_HEADS, gq * tq, LANES), F32)],
    )
    return pl.pallas_call(
        functools.partial(_attn_body, tq=tq, tk=tk, topk=topk),
        out_shape=jax.ShapeDtypeStruct(x2d.shape, F32),
        grid_spec=grid_spec,
        compiler_params=_cparams(("arbitrary", "arbitrary")),
        name="dsa_attend",
    )(nkeys, q, qi, wi, limit, x2d, k_hm, v_hm, ki, w_o.astype(BF16))


def _round_up(x, m):
    return (x + m - 1) // m * m


def kernel(x_prompt, x_sample, state_ssm_re, state_ssm_im, cache_k, cache_v, cache_kidx, cache_conv,
           norm_mix, norm_ffn, ssm_lambda_re, ssm_lambda_im, ssm_log_dt, ssm_b_re, ssm_b_im,
           ssm_c_re, ssm_c_im, ssm_d, ssm_w_glu, ssm_b_glu, attn_w_in, attn_q_norm, attn_k_norm,
           attn_w_o, ffn_w_up, ffn_conv_w, ffn_conv_b, ffn_w_down):
    b_p, t_p, _ = x_prompt.shape
    b_s, t_s, _ = x_sample.shape
    past = cache_k.shape[2]
    xp = x_prompt.astype(F32).reshape(b_p * t_p, D_MODEL)
    xs = x_sample.astype(F32).reshape(b_s * t_s, D_MODEL)

    ops = _s5_operators(ssm_lambda_re[0], ssm_lambda_im[0], ssm_log_dt[0], ssm_b_re[0], ssm_b_im[0],
                        ssm_c_re[0], ssm_c_im[0], ssm_d[0])
    zero_state = jnp.zeros((S5_OCT, b_p, 2 * S5_HALF), F32)
    zp, htp = _s5_scan(_norm(xp, norm_mix[0]), ops, zero_state, batch=b_p, seq=t_p)
    zs, hts = _s5_scan(_norm(xs, norm_mix[0]), ops, _state_to_oct(state_ssm_re[0], state_ssm_im[0]),
                       batch=b_s, seq=t_s)
    xp = _glu_residual(zp, xp, ssm_w_glu[0], ssm_b_glu[0])
    xs = _glu_residual(zs, xs, ssm_w_glu[0], ssm_b_glu[0])
    re_p, im_p = _oct_to_state(htp, b_p)
    re_s, im_s = _oct_to_state(hts, b_s)

    ffn0 = _ffn_weights(ffn_w_up[0], ffn_conv_w[0], ffn_conv_b[0], ffn_w_down[0])
    xp, conv_p0 = _conv_ffn(xp, jnp.zeros((b_p, CONV_W - 1, D_FF), F32), norm_ffn[0], ffn0, batch=b_p, seq=t_p)
    xs, conv_s0 = _conv_ffn(xs, cache_conv[0], norm_ffn[0], ffn0, batch=b_s, seq=t_s)

    pos_p = jnp.arange(t_p, dtype=I32)
    pos_s = past + jnp.arange(t_s, dtype=I32)
    topk_p = min(TOPK_MAX, t_p // 4)
    topk_s = min(TOPK_MAX, (past + t_s) // 4)

    (q_p, khm_p, vhm_p, kf_p, vf_p, qi_p, kib_p, kif_p, wi_p) = _dsa_project(
        xp, norm_mix[1], attn_w_in[0], attn_q_norm[0], attn_k_norm[0], pos_p)
    tq_p = min(t_p, 256)
    tk_p = min(t_p, 512)
    lim_p = jnp.tile((pos_p // CHUNK + 1) * CHUNK, b_p).reshape(b_p * t_p, 1)
    q_end = (jnp.arange(t_p // tq_p, dtype=I32) + 1) * tq_p
    nk_p = jnp.minimum((q_end + tk_p - 1) // tk_p, t_p // tk_p).astype(I32)
    xp = _dsa_attend(xp, q_p, qi_p, wi_p, lim_p, khm_p, vhm_p, kib_p, attn_w_o[0], nk_p,
                     batch=b_p, tq=tq_p, tk=tk_p, topk=topk_p)

    (q_s, khm_s, vhm_s, kf_s, vf_s, qi_s, kib_s, kif_s, wi_s) = _dsa_project(
        xs, norm_mix[1], attn_w_in[0], attn_q_norm[0], attn_k_norm[0], pos_s)
    tk_s = 11 * LANES
    s_all = past + t_s
    s_pad = _round_up(s_all, tk_s)

    def with_cache(cache, new_hm, extra=None):
        heads, w = new_hm.shape[0], new_hm.shape[-1]
        c = jnp.pad(cache.astype(BF16).transpose(2, 0, 1, 3),
                    ((0, 0), (0, 0), (0, s_pad - past), (0, w - cache.shape[-1])))
        if extra is not None:
            c = c + extra
        c = lax.dynamic_update_slice(c, new_hm.reshape(heads, b_s, t_s, w), (0, 0, past, 0))
        return c.reshape(heads, b_s * s_pad, w)

    one_lane = (jnp.arange(LANES) == HEAD_DIM).astype(BF16)
    k_all = with_cache(cache_k[0], khm_s)
    v_all = with_cache(cache_v[0], vhm_s, extra=one_lane)
    ki_all = with_cache(cache_kidx[0][:, :, None, :], kib_s[None])[0]
    lim_s = jnp.full((b_s * t_s, 1), s_all, I32)
    nk_s = jnp.full((1,), s_pad // tk_s, I32)
    xs = _dsa_attend(xs, q_s, qi_s, wi_s, lim_s, k_all, v_all, ki_all, attn_w_o[0], nk_s,
                     batch=b_s, tq=t_s, tk=tk_s, topk=topk_s)

    ffn1 = _ffn_weights(ffn_w_up[1], ffn_conv_w[1], ffn_conv_b[1], ffn_w_down[1])
    xp, conv_p1 = _conv_ffn(xp, jnp.zeros((b_p, CONV_W - 1, D_FF), F32), norm_ffn[1], ffn1, batch=b_p, seq=t_p)
    xs, conv_s1 = _conv_ffn(xs, cache_conv[1], norm_ffn[1], ffn1, batch=b_s, seq=t_s)

    dt = x_prompt.dtype
    kv_shape_p = (1, b_p, t_p, N_KV_HEADS, HEAD_DIM)
    kv_shape_s = (1, b_s, t_s, N_KV_HEADS, HEAD_DIM)
    return (xp.reshape(x_prompt.shape).astype(dt), xs.reshape(x_sample.shape).astype(x_sample.dtype),
            re_p[None].astype(dt), im_p[None].astype(dt),
            re_s[None].astype(state_ssm_re.dtype), im_s[None].astype(state_ssm_im.dtype),
            kf_p.reshape(kv_shape_p).astype(dt), vf_p.reshape(kv_shape_p).astype(dt),
            kif_p.reshape(1, b_p, t_p, IDX_DIM).astype(dt),
            kf_s.reshape(kv_shape_s).astype(dt), vf_s.reshape(kv_shape_s).astype(dt),
            kif_s.reshape(1, b_s, t_s, IDX_DIM).astype(dt),
            jnp.stack([conv_p0, conv_p1]).astype(dt), jnp.stack([conv_s0, conv_s1]).astype(dt))
```
